```python
import math
import jax, jax.numpy as jnp
from jax import lax
import numpy as np

D_MODEL = 1024
BATCH = 1
SEQ = 16384
DEPTH = 1
DEC_BATCH = 32
DEC_SEQ = 8
PAST_LEN = 16384
PAGE_SIZE = 128

D_CONV = D_MODEL // 2
CONV_W = 3
N_HEADS = 4
D_QK = 64
D_V = 2 * D_QK
D_ATT = N_HEADS * D_V
D_QKH = N_HEADS * 2 * D_QK
D_MIX = D_CONV + D_ATT
D_IN = 3 * D_CONV + 2 * D_QKH + N_HEADS * D_V
N_BUCKETS = 32
MAX_EXACT = N_BUCKETS // 2
MAX_DIST = 128
N_EXPERTS = 32
TOP_K = 4
D_FF = D_MODEL
SWIGLU_LIMIT = 7.0
SWIGLU_ALPHA = 1.702
MOE_BLOCK = 256
Q_BLOCK = 128
EPS = 1e-6
NEG_INF = -1e30

kernel_name = "hymba_conv_diffattn_moe_step"


def rms_norm(x, g):
    x32 = x.astype(jnp.float32)
    y = x32 * lax.rsqrt(jnp.mean(x32 * x32, axis=-1, keepdims=True) + EPS)
    return (y * g.astype(jnp.float32)).astype(x.dtype)


def rel_bucket(q_pos, k_pos):
    n = jnp.maximum(q_pos[:, None] - k_pos[None, :], 0)
    nf = jnp.maximum(n, 1).astype(jnp.float32)
    large = MAX_EXACT + (jnp.log(nf / MAX_EXACT) / math.log(MAX_DIST / MAX_EXACT)
                         * (N_BUCKETS - MAX_EXACT)).astype(jnp.int32)
    large = jnp.minimum(large, N_BUCKETS - 1)
    return jnp.where(n < MAX_EXACT, n, large)


def diff_attn(q, k, v, q_pos, k_pos, rel_table, lam):
    s = jnp.einsum("bqhmd,bkhmd->bhmqk", q.astype(jnp.float32), k.astype(jnp.float32)) * (D_QK ** -0.5)
    bias = rel_table.astype(jnp.float32)[rel_bucket(q_pos, k_pos)]
    s = s + jnp.transpose(bias, (2, 0, 1))[None, :, None]
    s = jnp.where((k_pos[None, :] <= q_pos[:, None]), s, NEG_INF)
    p = jax.nn.softmax(s, axis=-1)
    a = p[:, :, 0] - lam * p[:, :, 1]
    return jnp.einsum("bhqk,bkhd->bqhd", a, v.astype(jnp.float32))


def attend_prompt(q, k, v, lam, rel_table):
    B, T = q.shape[0], q.shape[1]
    nb = T // Q_BLOCK
    qb = jnp.moveaxis(q.reshape(B, nb, Q_BLOCK, N_HEADS, 2, D_QK), 1, 0)
    pos_b = jnp.arange(T, dtype=jnp.int32).reshape(nb, Q_BLOCK)
    k_pos = jnp.arange(T, dtype=jnp.int32)
    o = lax.map(lambda a: diff_attn(a[0], k, v, a[1], k_pos, rel_table, lam), (qb, pos_b))
    return jnp.moveaxis(o, 0, 1).reshape(B, T, N_HEADS, D_V)


def attend_sample(q, k, v, lam, rel_table, past_k, past_v):
    T = q.shape[1]
    k_all = jnp.concatenate([past_k.astype(k.dtype), k], axis=1)
    v_all = jnp.concatenate([past_v.astype(v.dtype), v], axis=1)
    q_pos = PAST_LEN + jnp.arange(T, dtype=jnp.int32)
    k_pos = jnp.arange(PAST_LEN + T, dtype=jnp.int32)
    return diff_attn(q, k_all, v_all, q_pos, k_pos, rel_table, lam)


def moe(h, w_router, b_router, w_gate, b_gate, w_up, b_up, w_down, b_down):
    B, T, D = h.shape
    xt = h.reshape(-1, D)
    n = xt.shape[0]
    nk = n * TOP_K
    logits = (xt @ w_router + b_router).astype(jnp.float32)
    top_val, top_idx = lax.top_k(logits, TOP_K)
    gates = jax.nn.softmax(top_val, axis=-1).reshape(-1)
    e_flat = top_idx.reshape(-1)
    tok_flat = jnp.repeat(jnp.arange(n, dtype=jnp.int32), TOP_K)
    order = jnp.argsort(e_flat)
    e_sorted = e_flat[order]
    tok_sorted = tok_flat[order]
    counts = jnp.bincount(e_flat, length=N_EXPERTS)
    start = jnp.cumsum(counts) - counts
    padded = (counts + MOE_BLOCK - 1) // MOE_BLOCK * MOE_BLOCK
    pend = jnp.cumsum(padded)
    pstart = pend - padded
    dest = pstart[e_sorted] + (jnp.arange(nk, dtype=jnp.int32) - start[e_sorted])
    n_blocks = -(-nk // MOE_BLOCK) + N_EXPERTS
    rows = n_blocks * MOE_BLOCK
    buf = jnp.zeros((rows, D), h.dtype).at[dest].set(xt[tok_sorted])
    blk_e = jnp.minimum(jnp.searchsorted(pend, jnp.arange(n_blocks) * MOE_BLOCK, side="right"), N_EXPERTS - 1)

    def expert_block(args):
        xb, e = args
        g = jnp.minimum(xb @ w_gate[e] + b_gate[e], SWIGLU_LIMIT)
        u = jnp.clip(xb @ w_up[e] + b_up[e], -SWIGLU_LIMIT, SWIGLU_LIMIT)
        a = g * jax.nn.sigmoid(SWIGLU_ALPHA * g) * (u + 1)
        return a @ w_down[e] + b_down[e]

    out = lax.map(expert_block, (buf.reshape(n_blocks, MOE_BLOCK, D), blk_e)).reshape(rows, D)
    y_assign = out[dest].astype(jnp.float32) * gates[order][:, None]
    y = jax.ops.segment_sum(y_assign, tok_sorted, num_segments=n)
    return y.reshape(B, T, D).astype(h.dtype)


def trunk_layer(x, c, conv_prev, attend, lam_init, w_ada, b_ada, g_mix, w_in, conv_w,
                q_norm_w, k_norm_w, lam_q1, lam_k1, lam_q2, lam_k2, subln_w, w_out,
                g_ffn, w_router, b_router, w_gate, b_gate, w_up, b_up, w_down, b_down):
    B, T, _ = x.shape
    mod = (jax.nn.silu(c) @ w_ada + b_ada)[:, None, :]
    sh1, sc1, gt1, sh2, sc2, gt2 = jnp.split(mod, 6, axis=-1)
    h = rms_norm(x, g_mix) * (1 + sc1) + sh1
    z = h @ w_in
    o1 = 3 * D_CONV
    gb, gc, xt, q, k, v = jnp.split(z, [D_CONV, 2 * D_CONV, o1, o1 + D_QKH, o1 + 2 * D_QKH], axis=-1)
    u = gc * xt
    u_ext = jnp.concatenate([conv_prev.astype(u.dtype), u], axis=1)
    y_conv = sum(conv_w[j] * u_ext[:, j:j + T] for j in range(CONV_W))
    y_conv = gb * y_conv
    new_conv = u_ext[:, -(CONV_W - 1):]
    q = rms_norm(q.reshape(B, T, N_HEADS, 2, D_QK), q_norm_w)
    k = rms_norm(k.reshape(B, T, N_HEADS, 2, D_QK), k_norm_w)
    v = v.reshape(B, T, N_HEADS, D_V)
    f32 = jnp.float32
    lam = (jnp.exp(jnp.dot(lam_q1.astype(f32), lam_k1.astype(f32)))
           - jnp.exp(jnp.dot(lam_q2.astype(f32), lam_k2.astype(f32))) + lam_init)
    o = attend(q, k, v, lam)
    o = (rms_norm(o, subln_w) * (1.0 - lam_init)).astype(x.dtype)
    mix = jnp.concatenate([y_conv, o.reshape(B, T, D_ATT)], axis=-1) @ w_out
    x = x + gt1 * mix
    h2 = rms_norm(x, g_ffn) * (1 + sc2) + sh2
    x = x + gt2 * moe(h2, w_router, b_router, w_gate, b_gate, w_up, b_up, w_down, b_down)
    return x, new_conv, k.reshape(B, T, N_HEADS, 2 * D_QK), v


def setup_inputs(seed: int = 0) -> dict:
    key = jax.random.key(seed)
    ks = iter(jax.random.split(key, 40))
    nrm = lambda shape, s: jax.random.normal(next(ks), shape, jnp.float32) * s
    gain = lambda shape: 1.0 + nrm(shape, 0.02)
    n_pages = PAST_LEN // PAGE_SIZE
    n_used = DEC_BATCH * n_pages
    n_phys = n_used + n_used // 4
    page_table = jax.random.permutation(next(ks), n_phys)[:n_used].reshape(DEC_BATCH, n_pages).astype(jnp.int32)
    L = DEPTH
    return {
        "x_prompt": nrm((BATCH, SEQ, D_MODEL), 1.0),
        "x_sample": nrm((DEC_BATCH, DEC_SEQ, D_MODEL), 1.0),
        "cache_k": nrm((L, n_phys, PAGE_SIZE, N_HEADS, 2 * D_QK), 1.0),
        "cache_v": nrm((L, n_phys, PAGE_SIZE, N_HEADS, D_V), 1.0),
        "state_conv": nrm((L, DEC_BATCH, CONV_W - 1, D_CONV), 1.0),
        "page_table": page_table,
        "c_prompt": nrm((BATCH, D_MODEL), 1.0),
        "c_sample": nrm((DEC_BATCH, D_MODEL), 1.0),
        "rel_table": nrm((N_BUCKETS, N_HEADS), 0.5),
        "w_ada": nrm((L, D_MODEL, 6 * D_MODEL), 0.5 * D_MODEL ** -0.5),
        "b_ada": nrm((L, 6 * D_MODEL), 0.02),
        "g_mix": gain((L, D_MODEL)),
        "w_in": nrm((L, D_MODEL, D_IN), D_MODEL ** -0.5),
        "conv_w": nrm((L, CONV_W, D_CONV), CONV_W ** -0.5),
        "q_norm_w": gain((L, D_QK)),
        "k_norm_w": gain((L, D_QK)),
        "lam_q1": nrm((L, D_QK), 0.1),
        "lam_k1": nrm((L, D_QK), 0.1),
        "lam_q2": nrm((L, D_QK), 0.1),
        "lam_k2": nrm((L, D_QK), 0.1),
        "subln_w": gain((L, D_V)),
        "w_out": nrm((L, D_MIX, D_MODEL), D_MIX ** -0.5),
        "g_ffn": gain((L, D_MODEL)),
        "w_router": nrm((L, D_MODEL, N_EXPERTS), D_MODEL ** -0.5),
        "b_router": nrm((L, N_EXPERTS), 0.01),
        "w_gate": nrm((L, N_EXPERTS, D_MODEL, D_FF), D_MODEL ** -0.5),
        "b_gate": nrm((L, N_EXPERTS, D_FF), 0.01),
        "w_up": nrm((L, N_EXPERTS, D_MODEL, D_FF), D_MODEL ** -0.5),
        "b_up": nrm((L, N_EXPERTS, D_FF), 0.01),
        "w_down": nrm((L, N_EXPERTS, D_FF, D_MODEL), D_FF ** -0.5),
        "b_down": nrm((L, N_EXPERTS, D_MODEL), 0.01),
    }


def reference(x_prompt, x_sample, cache_k, cache_v, state_conv, page_table, c_prompt, c_sample,
              rel_table, w_ada, b_ada, g_mix, w_in, conv_w, q_norm_w, k_norm_w,
              lam_q1, lam_k1, lam_q2, lam_k2, subln_w, w_out, g_ffn, w_router, b_router,
              w_gate, b_gate, w_up, b_up, w_down, b_down):
    n_pages = PAST_LEN // PAGE_SIZE
    xp, xs = x_prompt, x_sample
    kp_l, vp_l, cp_l, ks_l, vs_l, cs_l = [], [], [], [], [], []
    for l in range(DEPTH):
        lam_init = 0.8 - 0.6 * math.exp(-0.3 * l)
        w = (w_ada[l], b_ada[l], g_mix[l], w_in[l], conv_w[l], q_norm_w[l], k_norm_w[l],
             lam_q1[l], lam_k1[l], lam_q2[l], lam_k2[l], subln_w[l], w_out[l], g_ffn[l],
             w_router[l], b_router[l], w_gate[l], b_gate[l], w_up[l], b_up[l], w_down[l], b_down[l])
        conv0 = jnp.zeros((xp.shape[0], CONV_W - 1, D_CONV), xp.dtype)
        att_p = lambda q, k, v, lam: attend_prompt(q, k, v, lam, rel_table)
        xp, cp, kp, vp = trunk_layer(xp, c_prompt, conv0, att_p, lam_init, *w)
        past_k = cache_k[l][page_table].reshape(DEC_BATCH, n_pages * PAGE_SIZE, N_HEADS, 2, D_QK)
        past_v = cache_v[l][page_table].reshape(DEC_BATCH, n_pages * PAGE_SIZE, N_HEADS, D_V)
        att_s = lambda q, k, v, lam, pk=past_k, pv=past_v: attend_sample(q, k, v, lam, rel_table, pk, pv)
        xs, cs, ksn, vsn = trunk_layer(xs, c_sample, state_conv[l], att_s, lam_init, *w)
        kp_l.append(kp); vp_l.append(vp); cp_l.append(cp)
        ks_l.append(ksn); vs_l.append(vsn); cs_l.append(cs)
    k_prompt = jnp.stack(kp_l); v_prompt = jnp.stack(vp_l); conv_prompt = jnp.stack(cp_l)
    k_sample = jnp.stack(ks_l); v_sample = jnp.stack(vs_l); conv_sample = jnp.stack(cs_l)
    return (xp, xs, k_prompt, v_prompt, conv_prompt, k_sample, v_sample, conv_sample)
```

```python
import functools
import math

import numpy as np
import jax
import jax.numpy as jnp
from jax import lax
from jax.experimental import pallas as pl
from jax.experimental.pallas import tpu as pltpu

F32 = jnp.float32
BF16 = jnp.bfloat16
I32 = jnp.int32

D_MODEL = 1024
D_CONV = 512
CONV_W = 3
N_HEADS = 4
D_QK = 64
D_V = 128
D_ATT = N_HEADS * D_V
D_QKH = N_HEADS * 2 * D_QK
D_IN = 3 * D_CONV + 2 * D_QKH + D_ATT
N_BUCKETS = 32
MAX_EXACT = 16
MAX_DIST = 128
N_EXPERTS = 32
TOP_K = 4
SWIGLU_LIMIT = 7.0
SWIGLU_ALPHA = 1.702
PAGE_SIZE = 128
EPS = 1e-6
NEG_INF = -1e30
LAM_INIT = 0.8 - 0.6 * math.exp(-0.3 * 0)

LANES = 128
SUBLANES = 8
VMEM_LIMIT = 56 * 1024 * 1024

TM_IN = 512
TQ = 512
TK = 512
PAGES_PER_STEP = 8
TM_OUT = 256
MOE_BLOCK = 256
TM_ROW = 128


def _bucket_thresholds():
    n = np.arange(0, 4 * MAX_DIST)
    nf = np.maximum(n, 1).astype(np.float32)
    val = np.log(nf / np.float32(MAX_EXACT)) / np.float32(math.log(MAX_DIST / MAX_EXACT)) * np.float32(N_BUCKETS - MAX_EXACT)
    large = np.minimum(MAX_EXACT + val.astype(np.int32), N_BUCKETS - 1)
    bucket = np.where(n < MAX_EXACT, n, large)
    assert np.all(np.diff(bucket) >= 0) and bucket[-1] == N_BUCKETS - 1
    thr = [int(np.argmax(bucket >= b)) for b in range(N_BUCKETS)]
    return thr


BUCKET_THR = _bucket_thresholds()
FAR_DIST = BUCKET_THR[N_BUCKETS - 1]


def _cparams(sem):
    return pltpu.CompilerParams(dimension_semantics=sem, vmem_limit_bytes=VMEM_LIMIT)


def _adaln_kernel(c_ref, w_ref, b_ref, o_ref):
    c = c_ref[...]
    s = c * jax.nn.sigmoid(c)
    o_ref[...] = jnp.dot(s, w_ref[...], preferred_element_type=F32,
                         precision=lax.Precision.HIGHEST) + b_ref[...]


def _adaln(c_all, w_ada, b_ada):
    rows = c_all.shape[0]
    n = w_ada.shape[1]
    bn = D_MODEL
    return pl.pallas_call(
        _adaln_kernel,
        grid=(n // bn,),
        in_specs=[pl.BlockSpec((rows, D_MODEL), lambda j: (0, 0)),
                  pl.BlockSpec((D_MODEL, bn), lambda j: (0, j)),
                  pl.BlockSpec((1, bn), lambda j: (0, j))],
        out_specs=pl.BlockSpec((rows, bn), lambda j: (0, j)),
        out_shape=jax.ShapeDtypeStruct((rows, n), F32),
        compiler_params=_cparams(("arbitrary",)),
        name="adaln",
    )(c_all, w_ada, b_ada.reshape(1, n))


def _group_rms(xb, w_row):
    lane = lax.broadcasted_iota(I32, xb.shape, 1)
    lo_half = lane < D_QK
    sq = xb * xb
    lo = jnp.sum(jnp.where(lo_half, sq, 0.0), axis=1, keepdims=True)
    hi = jnp.sum(jnp.where(lo_half, 0.0, sq), axis=1, keepdims=True)
    ms = jnp.where(lo_half, lo, hi) * (1.0 / D_QK)
    return xb * lax.rsqrt(ms + EPS) * w_row


def _inproj_kernel(x_ref, sc_ref, sh_ref, g_ref, w_ref, cw_ref, qw_ref, kw_ref, p2_ref, p1_ref,
                   yc_ref, q_ref, k_ref, v_ref, kb_ref, vb_ref, u_ref, carry_ref, *, seq_len):
    tm = x_ref.shape[0]
    x = x_ref[...]
    ms = jnp.mean(x * x, axis=1, keepdims=True)
    h = x * lax.rsqrt(ms + EPS) * g_ref[...]
    h = h * (1.0 + sc_ref[...]) + sh_ref[...]
    z = jnp.dot(h.astype(BF16), w_ref[...], preferred_element_type=F32)
    gb = z[:, 0:D_CONV]
    u = z[:, D_CONV:2 * D_CONV] * z[:, 2 * D_CONV:3 * D_CONV]
    row = lax.broadcasted_iota(I32, (tm, D_CONV), 0)
    u1 = pltpu.roll(u, 1, 0)
    u2 = pltpu.roll(u, 2, 0)
    if seq_len is None:
        @pl.when(pl.program_id(0) == 0)
        def _():
            carry_ref[...] = jnp.zeros_like(carry_ref)
        prev2 = carry_ref[SUBLANES - 2:SUBLANES - 1, :]
        prev1 = carry_ref[SUBLANES - 1:SUBLANES, :]
        u1 = jnp.where(row == 0, prev1, u1)
        u2 = jnp.where(row == 0, prev2, jnp.where(row == 1, prev1, u2))
        carry_ref[...] = u[tm - SUBLANES:tm, :]
        u_ref[...] = u[tm - SUBLANES:tm, :]
    else:
        assert seq_len & (seq_len - 1) == 0
        pos = row & (seq_len - 1)
        u1 = jnp.where(pos == 0, p1_ref[...], u1)
        u2 = jnp.where(pos == 0, p2_ref[...], jnp.where(pos == 1, p1_ref[...], u2))
        u_ref[...] = u
    cw = cw_ref[...]
    yc = gb * (cw[0:1, :] * u2 + cw[1:2, :] * u1 + cw[2:3, :] * u)
    yc_ref[...] = yc.astype(yc_ref.dtype)
    o1 = 3 * D_CONV
    for hh in range(N_HEADS):
        c0 = o1 + hh * LANES
        qn = _group_rms(z[:, c0:c0 + LANES], qw_ref[...])
        q_ref[:, hh * LANES:(hh + 1) * LANES] = (qn * (D_QK ** -0.5)).astype(q_ref.dtype)
        c1 = o1 + D_QKH + hh * LANES
        kn = _group_rms(z[:, c1:c1 + LANES], kw_ref[...])
        k_ref[:, hh * LANES:(hh + 1) * LANES] = kn
        kb_ref[:, hh * LANES:(hh + 1) * LANES] = kn.astype(BF16)
    v = z[:, o1 + 2 * D_QKH:]
    v_ref[...] = v
    vb_ref[...] = v.astype(BF16)


def _inproj(x, sc, sh, g_mix, w_in_bf, conv_w, qw, kw, prev2, prev1, *, tm, seq_len):
    n = x.shape[0]
    grid = (n // tm,)
    per_row = sc.shape[0] != 1
    mod_spec = pl.BlockSpec((tm, D_MODEL), lambda i: (i, 0)) if per_row else pl.BlockSpec((1, D_MODEL), lambda i: (0, 0))
    if seq_len is None:
        prev2 = jnp.zeros((SUBLANES, D_CONV), F32)
        prev1 = prev2
        prev_spec = pl.BlockSpec((SUBLANES, D_CONV), lambda i: (0, 0))
        u_shape = jax.ShapeDtypeStruct((SUBLANES, D_CONV), F32)
        u_spec = pl.BlockSpec((SUBLANES, D_CONV), lambda i: (0, 0))
    else:
        prev_spec = pl.BlockSpec((tm, D_CONV), lambda i: (i, 0))
        u_shape = jax.ShapeDtypeStruct((n, D_CONV), F32)
        u_spec = pl.BlockSpec((tm, D_CONV), lambda i: (i, 0))
    const = lambda shape: pl.BlockSpec(shape, lambda i: (0, 0))
    rows = lambda w: pl.BlockSpec((tm, w), lambda i: (i, 0))
    qw2 = jnp.tile(qw.reshape(1, D_QK), (1, 2))
    kw2 = jnp.tile(kw.reshape(1, D_QK), (1, 2))
    return pl.pallas_call(
        functools.partial(_inproj_kernel, seq_len=seq_len),
        grid=grid,
        in_specs=[rows(D_MODEL), mod_spec, mod_spec, const((1, D_MODEL)), const((D_MODEL, D_IN)),
                  const((CONV_W, D_CONV)), const((1, LANES)), const((1, LANES)), prev_spec, prev_spec],
        out_specs=[rows(D_CONV), rows(D_QKH), rows(D_QKH), rows(D_ATT), rows(D_QKH), rows(D_ATT), u_spec],
        out_shape=[jax.ShapeDtypeStruct((n, D_CONV), BF16),
                   jax.ShapeDtypeStruct((n, D_QKH), BF16),
                   jax.ShapeDtypeStruct((n, D_QKH), F32),
                   jax.ShapeDtypeStruct((n, D_ATT), F32),
                   jax.ShapeDtypeStruct((n, D_QKH), BF16),
                   jax.ShapeDtypeStruct((n, D_ATT), BF16),
                   u_shape],
        scratch_shapes=[pltpu.VMEM((SUBLANES, D_CONV), F32)],
        compiler_params=_cparams(("arbitrary",)),
        name="inproj_seq" if seq_len is None else "inproj_batch",
    )(x, sc, sh, g_mix.reshape(1, D_MODEL), w_in_bf, conv_w, qw2, kw2, prev2, prev1)


def _bias_from_dist(dist, table_of_bucket):
    b = jnp.zeros(dist.shape, F32) + table_of_bucket(0)
    for bk in range(1, N_BUCKETS):
        b = jnp.where(dist >= BUCKET_THR[bk], table_of_bucket(bk), b)
    return b - table_of_bucket(N_BUCKETS - 1)


def _lambda_value(lam_ref):
    lq1, lk1, lq2, lk2 = (lam_ref[i:i + 1, :] for i in range(4))
    return (jnp.exp(jnp.sum(lq1 * lk1, axis=1, keepdims=True))
            - jnp.exp(jnp.sum(lq2 * lk2, axis=1, keepdims=True)) + LAM_INIT)


def _sub_norm(o, sw_row):
    ms = jnp.mean(o * o, axis=1, keepdims=True)
    return o * lax.rsqrt(ms + EPS) * sw_row * (1.0 - LAM_INIT)


def _attn_prompt_kernel(tbl_ref, q_ref, k_ref, v_ref, lam_ref, sw_ref, o_ref,
                        bdiag_ref, bprev_ref, m_ref, l_ref, acc_ref):
    h = pl.program_id(0)
    i = pl.program_id(1)
    tq = q_ref.shape[0]
    tk = TK
    assert tq == tk and tk >= FAR_DIST

    @pl.when(i == 0)
    def _():
        r = lax.broadcasted_iota(I32, (tq, tk), 0)
        c = lax.broadcasted_iota(I32, (tq, tk), 1)
        d0 = r - c
        tb = lambda b: tbl_ref[b, h]
        bdiag_ref[...] = jnp.where(d0 >= 0, _bias_from_dist(jnp.maximum(d0, 0), tb), NEG_INF)
        bprev_ref[...] = _bias_from_dist(d0 + tk, tb)

    m_ref[...] = jnp.full(m_ref.shape, -jnp.inf, F32)
    l_ref[...] = jnp.zeros(l_ref.shape, F32)
    acc_ref[...] = jnp.zeros(acc_ref.shape, F32)

    q = q_ref[...]
    lane = lax.broadcasted_iota(I32, q.shape, 1)
    zero = jnp.zeros_like(q)
    q_maps = (jnp.where(lane < D_QK, q, zero), jnp.where(lane < D_QK, zero, q))

    def step(j, bias_ref):
        start = pl.multiple_of(j * tk, tk)
        kb = k_ref[pl.ds(start, tk), :]
        vb = v_ref[pl.ds(start, tk), :]
        for mi in range(2):
            s = lax.dot_general(q_maps[mi], kb, (((1,), (1,)), ((), ())), preferred_element_type=F32)
            if bias_ref is not None:
                s = s + bias_ref[...]
            m_prev = m_ref[mi]
            m_new = jnp.maximum(m_prev, jnp.max(s, axis=1, keepdims=True))
            alpha = jnp.exp(m_prev - m_new)
            p = jnp.exp(s - jnp.concatenate([m_new] * (tk // LANES), axis=1))
            l_ref[mi] = alpha * l_ref[mi] + jnp.sum(p, axis=1, keepdims=True)
            acc_ref[mi] = alpha * acc_ref[mi] + jnp.dot(p.astype(BF16), vb, preferred_element_type=F32)
            m_ref[mi] = m_new

    def far_body(j, carry):
        step(j, None)
        return carry

    lax.fori_loop(0, jnp.maximum(i - 1, 0), far_body, 0)

    @pl.when(i > 0)
    def _():
        step(i - 1, bprev_ref)

    step(i, bdiag_ref)

    lam = _lambda_value(lam_ref)
    o = acc_ref[0] / l_ref[0] - lam * (acc_ref[1] / l_ref[1])
    o_ref[...] = _sub_norm(o, sw_ref[...]).astype(o_ref.dtype)


def _attn_prompt(q_bf, k_bf, v_bf, rel_table, lam_vecs, subln_w):
    t = q_bf.shape[0]
    nq = t // TQ
    grid_spec = pltpu.PrefetchScalarGridSpec(
        num_scalar_prefetch=0,
        grid=(N_HEADS, nq),
        in_specs=[pl.BlockSpec(memory_space=pltpu.SMEM),
                  pl.BlockSpec((TQ, LANES), lambda h, i: (i, h)),
                  pl.BlockSpec((t, LANES), lambda h, i: (0, h)),
                  pl.BlockSpec((t, D_V), lambda h, i: (0, h)),
                  pl.BlockSpec((4, D_QK), lambda h, i: (0, 0)),
                  pl.BlockSpec((1, D_V), lambda h, i: (0, 0))],
        out_specs=pl.BlockSpec((TQ, D_V), lambda h, i: (i, h)),
        scratch_shapes=[pltpu.VMEM((TQ, TK), F32), pltpu.VMEM((TQ, TK), F32),
                        pltpu.VMEM((2, TQ, LANES), F32), pltpu.VMEM((2, TQ, LANES), F32),
                        pltpu.VMEM((2, TQ, D_V), F32)],
    )
    return pl.pallas_call(
        _attn_prompt_kernel,
        grid_spec=grid_spec,
        out_shape=jax.ShapeDtypeStruct((t, D_ATT), BF16),
        compiler_params=_cparams(("arbitrary", "arbitrary")),
        name="attn_prompt",
    )(rel_table, q_bf, k_bf, v_bf, lam_vecs, subln_w.reshape(1, D_V))


Q_ROWS = N_HEADS * 2 * 8


def _attn_sample_kernel(pt_ref, q_ref, kn_ref, vn_ref, tblr_ref, lam_ref, sw_ref, *rest,
                        n_groups, dec_seq, past_len):
    pg = PAGES_PER_STEP
    k_refs = rest[:pg]
    v_refs = rest[pg:2 * pg]
    o_ref = rest[2 * pg]
    qf_ref, qb_ref, knp_ref, vnp_ref, blast_ref, bnew_ref, m_ref, l_ref, acc_ref = rest[2 * pg + 1:]
    b = pl.program_id(0)
    g = pl.program_id(1)
    assert dec_seq == SUBLANES and PAGE_SIZE >= FAR_DIST + dec_seq

    @pl.when(jnp.logical_and(b == 0, g == 0))
    def _():
        r = lax.broadcasted_iota(I32, (Q_ROWS, PAGE_SIZE), 0)
        c = lax.broadcasted_iota(I32, (Q_ROWS, PAGE_SIZE), 1)
        tok = r & (dec_seq - 1)
        tb = lambda bk: tblr_ref[:, bk:bk + 1]
        blast_ref[...] = _bias_from_dist(tok + PAGE_SIZE - c, tb)
        d = tok - c
        bnew_ref[...] = jnp.where(jnp.logical_and(d >= 0, c < dec_seq),
                                  _bias_from_dist(jnp.maximum(d, 0), tb), NEG_INF)

    @pl.when(g == 0)
    def _():
        q = q_ref[0]
        lane = lax.broadcasted_iota(I32, (dec_seq, LANES), 1)
        qf_ref[...] = jnp.zeros(qf_ref.shape, F32)
        for hh in range(N_HEADS):
            qh = q[:, hh * LANES:(hh + 1) * LANES]
            r0 = hh * 2 * dec_seq
            qf_ref[r0:r0 + dec_seq, hh * LANES:(hh + 1) * LANES] = jnp.where(lane < D_QK, qh, 0.0)
            qf_ref[r0 + dec_seq:r0 + 2 * dec_seq, hh * LANES:(hh + 1) * LANES] = jnp.where(lane < D_QK, 0.0, qh)
        qb_ref[...] = qf_ref[...].astype(BF16)
        knp_ref[...] = jnp.zeros(knp_ref.shape, F32)
        vnp_ref[...] = jnp.zeros(vnp_ref.shape, F32)
        knp_ref[0:dec_seq, :] = kn_ref[0]
        vnp_ref[0:dec_seq, :] = vn_ref[0]
        m_ref[...] = jnp.full(m_ref.shape, -jnp.inf, F32)
        l_ref[...] = jnp.zeros(l_ref.shape, F32)
        acc_ref[...] = jnp.zeros(acc_ref.shape, F32)

    qb = qb_ref[...]
    is_last = g == n_groups - 1

    def scores(k_page):
        return lax.dot_general(qb, k_page.astype(BF16), (((1,), (1,)), ((), ())), preferred_element_type=F32)

    def update(s_list, v_list):
        m_prev = m_ref[...]
        m_cur = functools.reduce(jnp.maximum, [jnp.max(s, axis=1, keepdims=True) for s in s_list])
        m_new = jnp.maximum(m_prev, m_cur)
        alpha = jnp.exp(m_prev - m_new)
        l_new = alpha * l_ref[...]
        pv = None
        for s, v_page in zip(s_list, v_list):
            p = jnp.exp(s - m_new)
            l_new = l_new + jnp.sum(p, axis=1, keepdims=True)
            d = jnp.dot(p.astype(BF16), v_page.astype(BF16), preferred_element_type=F32)
            pv = d if pv is None else pv + d
        acc_ref[...] = jnp.concatenate([alpha] * N_HEADS, axis=1) * acc_ref[...] + pv
        l_ref[...] = l_new
        m_ref[...] = m_new

    s_list = [scores(k_refs[p][0]) for p in range(pg)]
    s_list[pg - 1] = s_list[pg - 1] + jnp.where(is_last, blast_ref[...], 0.0)
    update(s_list, [v_refs[p][0] for p in range(pg)])

    @pl.when(is_last)
    def _():
        update([scores(knp_ref[...]) + bnew_ref[...]], [vnp_ref[...]])
        lam = _lambda_value(lam_ref)
        for hh in range(N_HEADS):
            r0 = hh * 2 * dec_seq
            cols = slice(hh * D_V, (hh + 1) * D_V)
            o1 = acc_ref[r0:r0 + dec_seq, cols] / l_ref[r0:r0 + dec_seq, :]
            o2 = acc_ref[r0 + dec_seq:r0 + 2 * dec_seq, cols] / l_ref[r0 + dec_seq:r0 + 2 * dec_seq, :]
            o_ref[0, :, cols] = _sub_norm(o1 - lam * o2, sw_ref[...])


def _attn_sample(q_s, k_s, v_s, cache_k2, cache_v2, page_table, rel_table, lam_vecs, subln_w):
    bsz, dec_seq, _ = q_s.shape
    n_pages = page_table.shape[1]
    pg = PAGES_PER_STEP
    n_groups = n_pages // pg
    tbl_rows = jnp.repeat(rel_table.T, 2 * dec_seq, axis=0)
    tbl_rows = jnp.pad(tbl_rows, ((0, 0), (0, LANES - N_BUCKETS)))
    per_b = lambda shape: pl.BlockSpec(shape, lambda b, g, pt: (b, 0, 0))
    const2 = lambda shape: pl.BlockSpec(shape, lambda b, g, pt: (0, 0))

    def page_spec(p):
        return pl.BlockSpec((1, PAGE_SIZE, D_ATT), lambda b, g, pt: (pt[b, g * pg + p], 0, 0))

    grid_spec = pltpu.PrefetchScalarGridSpec(
        num_scalar_prefetch=1,
        grid=(bsz, n_groups),
        in_specs=[per_b((1, dec_seq, D_QKH)), per_b((1, dec_seq, D_QKH)), per_b((1, dec_seq, D_ATT)),
                  const2((Q_ROWS, LANES)), const2((4, D_QK)), const2((1, D_V))]
                 + [page_spec(p) for p in range(pg)] + [page_spec(p) for p in range(pg)],
        out_specs=per_b((1, dec_seq, D_ATT)),
        scratch_shapes=[pltpu.VMEM((Q_ROWS, D_QKH), F32), pltpu.VMEM((Q_ROWS, D_QKH), BF16),
                        pltpu.VMEM((PAGE_SIZE, D_QKH), F32), pltpu.VMEM((PAGE_SIZE, D_ATT), F32),
                        pltpu.VMEM((Q_ROWS, PAGE_SIZE), F32), pltpu.VMEM((Q_ROWS, PAGE_SIZE), F32),
                        pltpu.VMEM((Q_ROWS, LANES), F32), pltpu.VMEM((Q_ROWS, LANES), F32),
                        pltpu.VMEM((Q_ROWS, D_ATT), F32)],
    )
    return pl.pallas_call(
        functools.partial(_attn_sample_kernel, n_groups=n_groups, dec_seq=dec_seq,
                          past_len=n_pages * PAGE_SIZE),
        grid_spec=grid_spec,
        out_shape=jax.ShapeDtypeStruct((bsz, dec_seq, D_ATT), F32),
        compiler_params=_cparams(("arbitrary", "arbitrary")),
        name="attn_sample",
    )(page_table, q_s, k_s, v_s, tbl_rows, lam_vecs, subln_w.reshape(1, D_V),
      *([cache_k2] * pg), *([cache_v2] * pg))


def _split_bf16(a):
    hi = a.astype(BF16)
    lo = (a - hi.astype(F32)).astype(BF16)
    return hi, lo


def _outproj_kernel(x_ref, yc_ref, o_ref, gt_ref, sc_ref, sh_ref, g_ref, wo_ref, wr_ref, br_ref, cin_ref,
                    h2_alias_ref, x1_ref, h2_ref, idx_ref, gate_ref, rank_ref, cnt_ref):
    del h2_alias_ref
    tm = x_ref.shape[0]

    @pl.when(pl.program_id(0) == 0)
    def _():
        cnt_ref[...] = cin_ref[...]

    mix = (jnp.dot(yc_ref[...].astype(BF16), wo_ref[0:D_CONV, :], preferred_element_type=F32)
           + jnp.dot(o_ref[...].astype(BF16), wo_ref[D_CONV:, :], preferred_element_type=F32))
    x1 = x_ref[...] + gt_ref[...] * mix
    x1_ref[...] = x1
    ms = jnp.mean(x1 * x1, axis=1, keepdims=True)
    h2 = x1 * lax.rsqrt(ms + EPS) * g_ref[...]
    h2 = h2 * (1.0 + sc_ref[...]) + sh_ref[...]
    h2_ref[...] = h2
    h_hi, h_lo = _split_bf16(h2)
    w_hi, w_lo = _split_bf16(wr_ref[...])
    logits = (jnp.dot(h_hi, w_hi, preferred_element_type=F32)
              + jnp.dot(h_hi, w_lo, preferred_element_type=F32)
              + jnp.dot(h_lo, w_hi, preferred_element_type=F32)) + br_ref[...]
    lane = lax.broadcasted_iota(I32, (tm, LANES), 1)
    lane_f = lane.astype(F32)
    vals, ids = [], []
    cur = logits
    for _ in range(TOP_K):
        mx = jnp.max(cur, axis=1, keepdims=True)
        ik = jnp.min(jnp.where(cur == mx, lane_f, float(LANES)), axis=1, keepdims=True)
        vals.append(mx)
        ids.append(ik)
        cur = jnp.where(lane_f == ik, -jnp.inf, cur)
    es = [jnp.exp(v - vals[0]) for v in vals]
    denom = functools.reduce(lambda a, c: a + c, es)
    sel = jnp.zeros((tm, LANES), F32)
    idx_out = jnp.zeros((tm, LANES), F32)
    gate_out = jnp.zeros((tm, LANES), F32)
    for k in range(TOP_K):
        sel = sel + jnp.where(lane_f == ids[k], 1.0, 0.0)
        idx_out = jnp.where(lane == k, ids[k], idx_out)
        gate_out = jnp.where(lane == k, es[k] / denom, gate_out)
    r = lax.broadcasted_iota(I32, (tm, tm), 0)
    c = lax.broadcasted_iota(I32, (tm, tm), 1)
    lower = jnp.where(r > c, 1.0, 0.0).astype(BF16)
    before = jnp.dot(lower, sel.astype(BF16), preferred_element_type=F32) + cnt_ref[...]
    rank_out = jnp.zeros((tm, LANES), F32)
    for k in range(TOP_K):
        rk = jnp.sum(jnp.where(lane_f == ids[k], before, 0.0), axis=1, keepdims=True)
        rank_out = jnp.where(lane == k, rk, rank_out)
    cnt_ref[...] = cnt_ref[...] + jnp.sum(sel, axis=0, keepdims=True)
    idx_ref[...] = idx_out.astype(I32)
    gate_ref[...] = gate_out
    rank_ref[...] = rank_out.astype(I32)


def _outproj(x, yc, o, gt, sc, sh, g_ffn, w_out_bf, w_router_pad, b_router_pad, cnt_in, h2_buf, row0, *, tm):
    n = x.shape[0]
    per_row = gt.shape[0] != 1
    mod_spec = pl.BlockSpec((tm, D_MODEL), lambda i: (i, 0)) if per_row else pl.BlockSpec((1, D_MODEL), lambda i: (0, 0))
    const = lambda shape: pl.BlockSpec(shape, lambda i: (0, 0))
    rows = lambda w: pl.BlockSpec((tm, w), lambda i: (i, 0))
    blk0 = row0 // tm
    assert row0 % tm == 0
    return pl.pallas_call(
        _outproj_kernel,
        grid=(n // tm,),
        in_specs=[rows(D_MODEL), rows(D_CONV), rows(D_ATT), mod_spec, mod_spec, mod_spec, const((1, D_MODEL)),
                  const((D_MODEL, D_MODEL)), const((D_MODEL, LANES)), const((1, LANES)), const((1, LANES)),
                  pl.BlockSpec(memory_space=pl.ANY)],
        out_specs=[rows(D_MODEL), pl.BlockSpec((tm, D_MODEL), lambda i: (i + blk0, 0)),
                   rows(LANES), rows(LANES), rows(LANES), const((1, LANES))],
        out_shape=[jax.ShapeDtypeStruct((n, D_MODEL), F32),
                   jax.ShapeDtypeStruct(h2_buf.shape, F32),
                   jax.ShapeDtypeStruct((n, LANES), I32),
                   jax.ShapeDtypeStruct((n, LANES), F32),
                   jax.ShapeDtypeStruct((n, LANES), I32),
                   jax.ShapeDtypeStruct((1, LANES), F32)],
        input_output_aliases={11: 1},
        compiler_params=_cparams(("arbitrary",)),
        name="outproj",
    )(x, yc, o, gt, sc, sh, g_ffn.reshape(1, D_MODEL), w_out_bf, w_router_pad, b_router_pad, cnt_in, h2_buf)


def _fetch_indices(idx_hbm, idx_smem, isem, i, n_steps):
    slot = i % 2

    def copy(step, sl):
        return pltpu.make_async_copy(idx_hbm.at[step], idx_smem.at[sl], isem.at[sl])

    @pl.when(i == 0)
    def _():
        copy(0, 0).start()

    copy(i, slot).wait()

    @pl.when(i + 1 < n_steps)
    def _():
        copy(i + 1, 1 - slot).start()

    return slot


def _push_kernel(dest_hbm, h_ref, buf_in, buf_out, idx_smem, isem, rsem, *, n_steps):
    del buf_in
    i = pl.program_id(0)
    tm = h_ref.shape[0]
    slot = _fetch_indices(dest_hbm, idx_smem, isem, i, n_steps)

    def row_copy(t, dst_row):
        return pltpu.make_async_copy(h_ref.at[pl.ds(t, 1), :], buf_out.at[pl.ds(dst_row, 1), :], rsem)

    def start_body(t, carry):
        for k in range(TOP_K):
            row_copy(t, idx_smem[slot, k * tm + t]).start()
        return carry

    lax.fori_loop(0, tm, start_body, 0)

    def wait_body(t, carry):
        for k in range(TOP_K):
            row_copy(t, 0).wait()
        return carry

    lax.fori_loop(0, tm, wait_body, 0)


def _moe_push(h2_buf, dest_tiles, buf_rows):
    n = h2_buf.shape[0]
    tm = TM_ROW
    n_steps = n // tm
    buf0 = jnp.zeros((buf_rows, D_MODEL), F32)
    return pl.pallas_call(
        functools.partial(_push_kernel, n_steps=n_steps),
        grid=(n_steps,),
        in_specs=[pl.BlockSpec(memory_space=pl.ANY),
                  pl.BlockSpec((tm, D_MODEL), lambda i: (i, 0)),
                  pl.BlockSpec(memory_space=pl.ANY)],
        out_specs=pl.BlockSpec(memory_space=pl.ANY),
        out_shape=jax.ShapeDtypeStruct((buf_rows, D_MODEL), F32),
        scratch_shapes=[pltpu.SMEM((2, TOP_K * tm), I32), pltpu.SemaphoreType.DMA((2,)), pltpu.SemaphoreType.DMA(())],
        input_output_aliases={2: 0},
        compiler_params=_cparams(("arbitrary",)),
        name="moe_push",
    )(dest_tiles, h2_buf, buf0)


def _expert_kernel(be_ref, nu_ref, x_ref, wg_ref, bg_ref, wu_ref, bu_ref, wd_ref, bd_ref, o_ref,
                   wg_bf, wu_bf, wd_bf):
    i = pl.program_id(0)
    n_used = nu_ref[0]
    prev = be_ref[jnp.maximum(i - 1, 0)]
    new_expert = jnp.logical_or(i == 0, be_ref[i] != prev)

    @pl.when(jnp.logical_and(i < n_used, new_expert))
    def _():
        wg_bf[...] = wg_ref[0].astype(BF16)
        wu_bf[...] = wu_ref[0].astype(BF16)
        wd_bf[...] = wd_ref[0].astype(BF16)

    @pl.when(i < n_used)
    def _():
        x = x_ref[...].astype(BF16)
        g = jnp.dot(x, wg_bf[...], preferred_element_type=F32) + bg_ref[0]
        u = jnp.dot(x, wu_bf[...], preferred_element_type=F32) + bu_ref[0]
        g = jnp.minimum(g, SWIGLU_LIMIT)
        u = jnp.clip(u, -SWIGLU_LIMIT, SWIGLU_LIMIT)
        a = g * jax.nn.sigmoid(SWIGLU_ALPHA * g) * (u + 1.0)
        o_ref[...] = jnp.dot(a.astype(BF16), wd_bf[...], preferred_element_type=F32) + bd_ref[0]

    @pl.when(i >= n_used)
    def _():
        o_ref[...] = jnp.zeros(o_ref.shape, F32)


def _moe_expert(buf, blk_e, n_used, w_gate, b_gate, w_up, b_up, w_down, b_down):
    rows = buf.shape[0]
    n_blocks = rows // MOE_BLOCK
    d_ff = w_gate.shape[2]

    def blk(i, be, nu):
        return jnp.minimum(i, nu[0] - 1)

    xspec = pl.BlockSpec((MOE_BLOCK, D_MODEL), lambda i, be, nu: (blk(i, be, nu), 0))
    wspec = lambda a, b: pl.BlockSpec((1, a, b), lambda i, be, nu: (be[blk(i, be, nu)], 0, 0))
    grid_spec = pltpu.PrefetchScalarGridSpec(
        num_scalar_prefetch=2,
        grid=(n_blocks,),
        in_specs=[xspec, wspec(D_MODEL, d_ff), wspec(1, d_ff), wspec(D_MODEL, d_ff), wspec(1, d_ff),
                  wspec(d_ff, D_MODEL), wspec(1, D_MODEL)],
        out_specs=pl.BlockSpec((MOE_BLOCK, D_MODEL), lambda i, be, nu: (i, 0)),
        scratch_shapes=[pltpu.VMEM((D_MODEL, d_ff), BF16), pltpu.VMEM((D_MODEL, d_ff), BF16),
                        pltpu.VMEM((d_ff, D_MODEL), BF16)],
    )
    return pl.pallas_call(
        _expert_kernel,
        grid_spec=grid_spec,
        out_shape=jax.ShapeDtypeStruct((rows, D_MODEL), F32),
        compiler_params=_cparams(("arbitrary",)),
        name="moe_expert",
    )(blk_e, n_used, buf, w_gate, b_gate.reshape(N_EXPERTS, 1, d_ff), w_up, b_up.reshape(N_EXPERTS, 1, d_ff),
      w_down, b_down.reshape(N_EXPERTS, 1, D_MODEL))


def _combine_kernel(dest_hbm, out_hbm, x1_ref, gate_ref, gt_ref, y_ref, idx_smem, rows_ref, isem, rsem, *, n_steps):
    i = pl.program_id(0)
    tm = x1_ref.shape[0]
    slot = _fetch_indices(dest_hbm, idx_smem, isem, i, n_steps)

    def row_copy(src_row, r):
        return pltpu.make_async_copy(out_hbm.at[pl.ds(src_row, 1), :], rows_ref.at[pl.ds(r, 1), :], rsem)

    def start_body(r, carry):
        row_copy(idx_smem[slot, r], r).start()
        return carry

    lax.fori_loop(0, TOP_K * tm, start_body, 0)

    def wait_body(r, carry):
        row_copy(0, r).wait()
        return carry

    lax.fori_loop(0, TOP_K * tm, wait_body, 0)

    gates = gate_ref[...]
    y = jnp.zeros((tm, D_MODEL), F32)
    for k in range(TOP_K):
        y = y + gates[:, k:k + 1] * rows_ref[k * tm:(k + 1) * tm, :]
    y_ref[...] = x1_ref[...] + gt_ref[...] * y


def _moe_combine(out_rows, dest_tiles, x1, gates, gt):
    n = x1.shape[0]
    tm = TM_ROW
    n_steps = n // tm
    per_row = gt.shape[0] != 1
    mod_spec = pl.BlockSpec((tm, D_MODEL), lambda i: (i, 0)) if per_row else pl.BlockSpec((1, D_MODEL), lambda i: (0, 0))
    return pl.pallas_call(
        functools.partial(_combine_kernel, n_steps=n_steps),
        grid=(n_steps,),
        in_specs=[pl.BlockSpec(memory_space=pl.ANY), pl.BlockSpec(memory_space=pl.ANY),
                  pl.BlockSpec((tm, D_MODEL), lambda i: (i, 0)),
                  pl.BlockSpec((tm, LANES), lambda i: (i, 0)),
                  mod_spec],
        out_specs=pl.BlockSpec((tm, D_MODEL), lambda i: (i, 0)),
        out_shape=jax.ShapeDtypeStruct((n, D_MODEL), F32),
        scratch_shapes=[pltpu.SMEM((2, TOP_K * tm), I32), pltpu.VMEM((TOP_K * tm, D_MODEL), F32),
                        pltpu.SemaphoreType.DMA((2,)), pltpu.SemaphoreType.DMA(())],
        compiler_params=_cparams(("arbitrary",)),
        name="moe_combine",
    )(dest_tiles, out_rows, x1, gates, gt)


def _dest_tiles(dest, tm):
    n = dest.shape[0]
    return dest.reshape(n // tm, tm, TOP_K).transpose(0, 2, 1).reshape(n // tm, TOP_K * tm)


def kernel(x_prompt, x_sample, cache_k, cache_v, state_conv, page_table, c_prompt, c_sample, rel_table, w_ada, b_ada, g_mix, w_in, conv_w, q_norm_w, k_norm_w, lam_q1, lam_k1, lam_q2, lam_k2, subln_w, w_out, g_ffn, w_router, b_router, w_gate, b_gate, w_up, b_up, w_down, b_down):
    assert w_ada.shape[0] == 1, "single-layer trunk"
    bp, t_p, _ = x_prompt.shape
    bs, t_s, _ = x_sample.shape
    assert bp == 1
    n_s = bs * t_s
    n_all = t_p + n_s
    l = 0

    n_c = bp + bs
    c_rows = -(-n_c // SUBLANES) * SUBLANES
    c_all = jnp.concatenate([c_prompt, c_sample, jnp.zeros((c_rows - n_c, D_MODEL), F32)], axis=0)
    mod = _adaln(c_all, w_ada[l], b_ada[l])
    mod_p = [mod[0:1, j * D_MODEL:(j + 1) * D_MODEL] for j in range(6)]
    mod_s = [jnp.repeat(mod[1:1 + bs, j * D_MODEL:(j + 1) * D_MODEL], t_s, axis=0) for j in range(6)]

    w_in_bf = w_in[l].astype(BF16)
    w_out_bf = w_out[l].astype(BF16)
    lam_vecs = jnp.stack([lam_q1[l], lam_k1[l], lam_q2[l], lam_k2[l]])

    xp = x_prompt.reshape(t_p, D_MODEL)
    xs = x_sample.reshape(n_s, D_MODEL)
    yc_p, q_p, k_p, v_p, kb_p, vb_p, u_tail = _inproj(
        xp, mod_p[1], mod_p[0], g_mix[l], w_in_bf, conv_w[l], q_norm_w[l], k_norm_w[l], None, None,
        tm=TM_IN, seq_len=None)
    st = state_conv[l]
    prev2 = jnp.repeat(st[:, 0, :], t_s, axis=0)
    prev1 = jnp.repeat(st[:, 1, :], t_s, axis=0)
    yc_s, q_s, k_s, v_s, _, _, u_s = _inproj(
        xs, mod_s[1], mod_s[0], g_mix[l], w_in_bf, conv_w[l], q_norm_w[l], k_norm_w[l], prev2, prev1,
        tm=n_s, seq_len=t_s)

    o_p = _attn_prompt(q_p, kb_p, vb_p, rel_table, lam_vecs, subln_w[l])
    n_phys = cache_k.shape[1]
    o_s = _attn_sample(q_s.astype(F32).reshape(bs, t_s, D_QKH), k_s.reshape(bs, t_s, D_QKH),
                       v_s.reshape(bs, t_s, D_ATT),
                       cache_k[l].reshape(n_phys, PAGE_SIZE, D_QKH), cache_v[l].reshape(n_phys, PAGE_SIZE, D_ATT),
                       page_table, rel_table, lam_vecs, subln_w[l]).reshape(n_s, D_ATT)

    w_router_pad = jnp.pad(w_router[l], ((0, 0), (0, LANES - N_EXPERTS)))
    b_router_pad = jnp.concatenate([b_router[l], jnp.full((LANES - N_EXPERTS,), NEG_INF, F32)]).reshape(1, LANES)
    h2_buf = jnp.zeros((n_all, D_MODEL), F32)
    cnt0 = jnp.zeros((1, LANES), F32)
    x1_p, h2_buf, idx_p, gate_p, rank_p, cnt1 = _outproj(
        xp, yc_p, o_p, mod_p[2], mod_p[4], mod_p[3], g_ffn[l], w_out_bf, w_router_pad, b_router_pad, cnt0,
        h2_buf, 0, tm=TM_OUT)
    x1_s, h2_buf, idx_s, gate_s, rank_s, cnt2 = _outproj(
        xs, yc_s, o_s, mod_s[2], mod_s[4], mod_s[3], g_ffn[l], w_out_bf, w_router_pad, b_router_pad, cnt1,
        h2_buf, t_p, tm=TM_OUT)

    counts = cnt2[0, :N_EXPERTS].astype(I32)
    padded = (counts + MOE_BLOCK - 1) // MOE_BLOCK * MOE_BLOCK
    pend = jnp.cumsum(padded)
    pstart = pend - padded
    n_blocks = (n_all * TOP_K) // MOE_BLOCK + N_EXPERTS
    idx_all = jnp.concatenate([idx_p[:, :TOP_K], idx_s[:, :TOP_K]], axis=0)
    rank_all = jnp.concatenate([rank_p[:, :TOP_K], rank_s[:, :TOP_K]], axis=0)
    dest = pstart[idx_all] + rank_all
    blk_e = jnp.minimum(jnp.searchsorted(pend, jnp.arange(n_blocks, dtype=I32) * MOE_BLOCK, side="right"),
                        N_EXPERTS - 1).astype(I32)
    n_used = (pend[-1] // MOE_BLOCK).astype(I32).reshape(1)

    buf = _moe_push(h2_buf, _dest_tiles(dest, TM_ROW), n_blocks * MOE_BLOCK)
    out_rows = _moe_expert(buf, blk_e, n_used, w_gate[l], b_gate[l], w_up[l], b_up[l], w_down[l], b_down[l])
    y_p = _moe_combine(out_rows, _dest_tiles(dest[:t_p], TM_ROW), x1_p, gate_p, mod_p[5])
    y_s = _moe_combine(out_rows, _dest_tiles(dest[t_p:], TM_ROW), x1_s, gate_s, mod_s[5])

    k_prompt = k_p.reshape(1, bp, t_p, N_HEADS, 2 * D_QK)
    v_prompt = v_p.reshape(1, bp, t_p, N_HEADS, D_V)
    conv_prompt = u_tail[SUBLANES - (CONV_W - 1):, :].reshape(1, bp, CONV_W - 1, D_CONV)
    k_sample = k_s.reshape(1, bs, t_s, N_HEADS, 2 * D_QK)
    v_sample = v_s.reshape(1, bs, t_s, N_HEADS, D_V)
    conv_sample = u_s.reshape(bs, t_s, D_CONV)[:, t_s - (CONV_W - 1):, :].reshape(1, bs, CONV_W - 1, D_CONV)
    return (y_p.reshape(bp, t_p, D_MODEL), y_s.reshape(bs, t_s, D_MODEL),
            k_prompt, v_prompt, conv_prompt, k_sample, v_sample, conv_sample)
```

```python
import functools
import math

import numpy as np
import jax
import jax.numpy as jnp
from jax import lax
from jax.experimental import pallas as pl
from jax.experimental.pallas import tpu as pltpu

F32 = jnp.float32
BF16 = jnp.bfloat16
I32 = jnp.int32

D_MODEL = 1024
D_CONV = 512
CONV_W = 3
N_HEADS = 4
D_QK = 64
D_V = 128
D_ATT = N_HEADS * D_V
D_QKH = N_HEADS * 2 * D_QK
D_IN = 3 * D_CONV + 2 * D_QKH + D_ATT
N_BUCKETS = 32
MAX_EXACT = 16
MAX_DIST = 128
N_EXPERTS = 32
TOP_K = 4
SWIGLU_LIMIT = 7.0
SWIGLU_ALPHA = 1.702
PAGE_SIZE = 128
EPS = 1e-6
NEG_INF = -1e30
LAM_INIT = 0.8 - 0.6 * math.exp(-0.3 * 0)

LANES = 128
SUBLANES = 8
VMEM_LIMIT = 56 * 1024 * 1024

TM_IN = 512
TQ = 512
TK = 512
PAGES_PER_STEP = 16
TM_OUT = 256
MOE_BLOCK = 256
TM_ROW = 128
DMA_UNROLL = 4


def _bucket_thresholds():
    n = np.arange(0, 4 * MAX_DIST)
    nf = np.maximum(n, 1).astype(np.float32)
    val = np.log(nf / np.float32(MAX_EXACT)) / np.float32(math.log(MAX_DIST / MAX_EXACT)) * np.float32(N_BUCKETS - MAX_EXACT)
    large = np.minimum(MAX_EXACT + val.astype(np.int32), N_BUCKETS - 1)
    bucket = np.where(n < MAX_EXACT, n, large)
    assert np.all(np.diff(bucket) >= 0) and bucket[-1] == N_BUCKETS - 1
    thr = [int(np.argmax(bucket >= b)) for b in range(N_BUCKETS)]
    return thr


BUCKET_THR = _bucket_thresholds()
FAR_DIST = BUCKET_THR[N_BUCKETS - 1]


def _log2(n):
    assert n > 0 and n & (n - 1) == 0
    return n.bit_length() - 1


def _cparams(sem):
    return pltpu.CompilerParams(dimension_semantics=sem, vmem_limit_bytes=VMEM_LIMIT)


def _adaln_kernel(c_ref, w_ref, b_ref, o_ref):
    c = c_ref[...]
    s = c * jax.nn.sigmoid(c)
    o_ref[...] = jnp.dot(s, w_ref[...], preferred_element_type=F32,
                         precision=lax.Precision.HIGHEST) + b_ref[...]


def _adaln(c_all, w_ada, b_ada):
    rows = c_all.shape[0]
    n = w_ada.shape[1]
    bn = D_MODEL
    return pl.pallas_call(
        _adaln_kernel,
        grid=(n // bn,),
        in_specs=[pl.BlockSpec((rows, D_MODEL), lambda j: (0, 0)),
                  pl.BlockSpec((D_MODEL, bn), lambda j: (0, j)),
                  pl.BlockSpec((1, bn), lambda j: (0, j))],
        out_specs=pl.BlockSpec((rows, bn), lambda j: (0, j)),
        out_shape=jax.ShapeDtypeStruct((rows, n), F32),
        compiler_params=_cparams(("arbitrary",)),
        name="adaln",
    )(c_all, w_ada, b_ada.reshape(1, n))


def _group_rms(xb, w_row):
    lane = lax.broadcasted_iota(I32, xb.shape, 1)
    lo_half = lane < D_QK
    sq = xb * xb
    lo = jnp.sum(jnp.where(lo_half, sq, 0.0), axis=1, keepdims=True)
    hi = jnp.sum(jnp.where(lo_half, 0.0, sq), axis=1, keepdims=True)
    ms = jnp.where(lo_half, lo, hi) * (1.0 / D_QK)
    return xb * lax.rsqrt(ms + EPS) * w_row


def _inproj_kernel(x_ref, sc_ref, sh_ref, g_ref, w_ref, cw_ref, qw_ref, kw_ref, p2_ref, p1_ref,
                   yc_ref, q_ref, k_ref, v_ref, kb_ref, vb_ref, u_ref, carry_ref, *, seq_len):
    tm = x_ref.shape[0]
    x = x_ref[...]
    ms = jnp.mean(x * x, axis=1, keepdims=True)
    h = x * lax.rsqrt(ms + EPS) * g_ref[...]
    h = h * (1.0 + sc_ref[...]) + sh_ref[...]
    z = jnp.dot(h.astype(BF16), w_ref[...], preferred_element_type=F32)
    gb = z[:, 0:D_CONV]
    u = z[:, D_CONV:2 * D_CONV] * z[:, 2 * D_CONV:3 * D_CONV]
    row = lax.broadcasted_iota(I32, (tm, D_CONV), 0)
    u1 = pltpu.roll(u, 1, 0)
    u2 = pltpu.roll(u, 2, 0)
    if seq_len is None:
        @pl.when(pl.program_id(0) == 0)
        def _():
            carry_ref[...] = jnp.zeros_like(carry_ref)
        prev2 = carry_ref[SUBLANES - 2:SUBLANES - 1, :]
        prev1 = carry_ref[SUBLANES - 1:SUBLANES, :]
        u1 = jnp.where(row == 0, prev1, u1)
        u2 = jnp.where(row == 0, prev2, jnp.where(row == 1, prev1, u2))
        carry_ref[...] = u[tm - SUBLANES:tm, :]
        u_ref[...] = u[tm - SUBLANES:tm, :]
    else:
        pos = row & (seq_len - 1)
        u1 = jnp.where(pos == 0, p1_ref[...], u1)
        u2 = jnp.where(pos == 0, p2_ref[...], jnp.where(pos == 1, p1_ref[...], u2))
        u_ref[...] = u
    cw = cw_ref[...]
    yc = gb * (cw[0:1, :] * u2 + cw[1:2, :] * u1 + cw[2:3, :] * u)
    yc_ref[...] = yc.astype(yc_ref.dtype)
    o1 = 3 * D_CONV
    for hh in range(N_HEADS):
        c0 = o1 + hh * LANES
        qn = _group_rms(z[:, c0:c0 + LANES], qw_ref[...])
        q_ref[:, hh * LANES:(hh + 1) * LANES] = (qn * (D_QK ** -0.5)).astype(q_ref.dtype)
        c1 = o1 + D_QKH + hh * LANES
        kn = _group_rms(z[:, c1:c1 + LANES], kw_ref[...])
        k_ref[pl.ds(hh, tm, stride=N_HEADS), :] = kn
        kb_ref[:, hh * LANES:(hh + 1) * LANES] = kn.astype(BF16)
    v = z[:, o1 + 2 * D_QKH:]
    for hh in range(N_HEADS):
        v_ref[pl.ds(hh, tm, stride=N_HEADS), :] = v[:, hh * D_V:(hh + 1) * D_V]
    vb_ref[...] = v.astype(BF16)


def _inproj(x, sc, sh, g_mix, w_in_bf, conv_w, qw, kw, prev2, prev1, *, tm, seq_len):
    n = x.shape[0]
    grid = (n // tm,)
    per_row = sc.shape[0] != 1
    mod_spec = pl.BlockSpec((tm, D_MODEL), lambda i: (i, 0)) if per_row else pl.BlockSpec((1, D_MODEL), lambda i: (0, 0))
    if seq_len is None:
        prev2 = jnp.zeros((SUBLANES, D_CONV), F32)
        prev1 = prev2
        prev_spec = pl.BlockSpec((SUBLANES, D_CONV), lambda i: (0, 0))
        u_shape = jax.ShapeDtypeStruct((SUBLANES, D_CONV), F32)
        u_spec = pl.BlockSpec((SUBLANES, D_CONV), lambda i: (0, 0))
    else:
        assert seq_len & (seq_len - 1) == 0
        prev_spec = pl.BlockSpec((tm, D_CONV), lambda i: (i, 0))
        u_shape = jax.ShapeDtypeStruct((n, D_CONV), F32)
        u_spec = pl.BlockSpec((tm, D_CONV), lambda i: (i, 0))
    const = lambda shape: pl.BlockSpec(shape, lambda i: (0, 0))
    rows = lambda w: pl.BlockSpec((tm, w), lambda i: (i, 0))
    head_rows = pl.BlockSpec((tm * N_HEADS, LANES), lambda i: (i, 0))
    qw2 = jnp.tile(qw.reshape(1, D_QK), (1, 2))
    kw2 = jnp.tile(kw.reshape(1, D_QK), (1, 2))
    return pl.pallas_call(
        functools.partial(_inproj_kernel, seq_len=seq_len),
        grid=grid,
        in_specs=[rows(D_MODEL), mod_spec, mod_spec, const((1, D_MODEL)), const((D_MODEL, D_IN)),
                  const((CONV_W, D_CONV)), const((1, LANES)), const((1, LANES)), prev_spec, prev_spec],
        out_specs=[rows(D_CONV), rows(D_QKH), head_rows, head_rows, rows(D_QKH), rows(D_ATT), u_spec],
        out_shape=[jax.ShapeDtypeStruct((n, D_CONV), BF16),
                   jax.ShapeDtypeStruct((n, D_QKH), BF16),
                   jax.ShapeDtypeStruct((n * N_HEADS, 2 * D_QK), F32),
                   jax.ShapeDtypeStruct((n * N_HEADS, D_V), F32),
                   jax.ShapeDtypeStruct((n, D_QKH), BF16),
                   jax.ShapeDtypeStruct((n, D_ATT), BF16),
                   u_shape],
        scratch_shapes=[pltpu.VMEM((SUBLANES, D_CONV), F32)],
        compiler_params=_cparams(("arbitrary",)),
        name="inproj_seq" if seq_len is None else "inproj_batch",
    )(x, sc, sh, g_mix.reshape(1, D_MODEL), w_in_bf, conv_w, qw2, kw2, prev2, prev1)


def _bias_from_dist(dist, table_of_bucket):
    b = jnp.zeros(dist.shape, F32) + table_of_bucket(0)
    for bk in range(1, N_BUCKETS):
        b = jnp.where(dist >= BUCKET_THR[bk], table_of_bucket(bk), b)
    return b - table_of_bucket(N_BUCKETS - 1)


def _lambda_value(lam_ref):
    lq1, lk1, lq2, lk2 = (lam_ref[i:i + 1, :] for i in range(4))
    return (jnp.exp(jnp.sum(lq1 * lk1, axis=1, keepdims=True))
            - jnp.exp(jnp.sum(lq2 * lk2, axis=1, keepdims=True)) + LAM_INIT)


def _sub_norm(o, sw_row):
    ms = jnp.mean(o * o, axis=1, keepdims=True)
    return o * lax.rsqrt(ms + EPS) * sw_row * (1.0 - LAM_INIT)


def _attn_prompt_kernel(tbl_ref, q_ref, k_ref, v_ref, lam_ref, sw_ref, o_ref,
                        bdiag_ref, bprev_ref, m_ref, l_ref, acc_ref):
    h = pl.program_id(0)
    i = pl.program_id(1)
    tq = q_ref.shape[0]
    tk = TK
    assert tq == tk and tk >= FAR_DIST

    @pl.when(i == 0)
    def _():
        r = lax.broadcasted_iota(I32, (tq, tk), 0)
        c = lax.broadcasted_iota(I32, (tq, tk), 1)
        d0 = r - c
        tb = lambda b: tbl_ref[b, h]
        bdiag_ref[...] = jnp.where(d0 >= 0, _bias_from_dist(jnp.maximum(d0, 0), tb), NEG_INF)
        bprev_ref[...] = _bias_from_dist(d0 + tk, tb)

    m_ref[...] = jnp.full(m_ref.shape, -jnp.inf, F32)
    l_ref[...] = jnp.zeros(l_ref.shape, F32)
    acc_ref[...] = jnp.zeros(acc_ref.shape, F32)

    q = q_ref[...]
    lane = lax.broadcasted_iota(I32, q.shape, 1)
    zero = jnp.zeros_like(q)
    q_maps = (jnp.where(lane < D_QK, q, zero), jnp.where(lane < D_QK, zero, q))

    def step(j, bias_ref):
        start = pl.multiple_of(j * tk, tk)
        kb = k_ref[pl.ds(start, tk), :]
        vb = v_ref[pl.ds(start, tk), :]
        for mi in range(2):
            s = lax.dot_general(q_maps[mi], kb, (((1,), (1,)), ((), ())), preferred_element_type=F32)
            if bias_ref is not None:
                s = s + bias_ref[...]
            m_prev = m_ref[mi]
            m_new = jnp.maximum(m_prev, jnp.max(s, axis=1, keepdims=True))
            alpha = jnp.exp(m_prev - m_new)
            p = jnp.exp(s - jnp.concatenate([m_new] * (tk // LANES), axis=1))
            l_ref[mi] = alpha * l_ref[mi] + jnp.sum(p, axis=1, keepdims=True)
            acc_ref[mi] = alpha * acc_ref[mi] + jnp.dot(p.astype(BF16), vb, preferred_element_type=F32)
            m_ref[mi] = m_new

    def far_body(j, carry):
        step(j, None)
        return carry

    lax.fori_loop(0, jnp.maximum(i - 1, 0), far_body, 0)

    @pl.when(i > 0)
    def _():
        step(i - 1, bprev_ref)

    step(i, bdiag_ref)

    lam = _lambda_value(lam_ref)
    o = acc_ref[0] / l_ref[0] - lam * (acc_ref[1] / l_ref[1])
    o_ref[...] = _sub_norm(o, sw_ref[...]).astype(o_ref.dtype)


def _attn_prompt(q_bf, k_bf, v_bf, rel_table, lam_vecs, subln_w):
    t = q_bf.shape[0]
    nq = t // TQ
    grid_spec = pltpu.PrefetchScalarGridSpec(
        num_scalar_prefetch=0,
        grid=(N_HEADS, nq),
        in_specs=[pl.BlockSpec(memory_space=pltpu.SMEM),
                  pl.BlockSpec((TQ, LANES), lambda h, i: (i, h)),
                  pl.BlockSpec((t, LANES), lambda h, i: (0, h)),
                  pl.BlockSpec((t, D_V), lambda h, i: (0, h)),
                  pl.BlockSpec((4, D_QK), lambda h, i: (0, 0)),
                  pl.BlockSpec((1, D_V), lambda h, i: (0, 0))],
        out_specs=pl.BlockSpec((TQ, D_V), lambda h, i: (i, h)),
        scratch_shapes=[pltpu.VMEM((TQ, TK), F32), pltpu.VMEM((TQ, TK), F32),
                        pltpu.VMEM((2, TQ, LANES), F32), pltpu.VMEM((2, TQ, LANES), F32),
                        pltpu.VMEM((2, TQ, D_V), F32)],
    )
    return pl.pallas_call(
        _attn_prompt_kernel,
        grid_spec=grid_spec,
        out_shape=jax.ShapeDtypeStruct((t, D_ATT), BF16),
        compiler_params=_cparams(("arbitrary", "arbitrary")),
        name="attn_prompt",
    )(rel_table, q_bf, k_bf, v_bf, lam_vecs, subln_w.reshape(1, D_V))


PAGE_ROWS = PAGE_SIZE * N_HEADS


def _attn_sample_kernel(pt_ref, q_ref, kn_ref, vn_ref, tblr_ref, lam_ref, sw_ref, *rest, n_groups, dec_seq):
    pg = PAGES_PER_STEP
    k_refs = rest[:pg]
    v_refs = rest[pg:2 * pg]
    o_ref = rest[2 * pg]
    qf_ref, qb_ref, knp_ref, vnp_ref, bmask_ref, blast_ref, bnew_ref, m_ref, l_ref, acc_ref = rest[2 * pg + 1:]
    b = pl.program_id(0)
    g = pl.program_id(1)
    q_rows = N_HEADS * 2 * dec_seq
    new_rows = dec_seq * N_HEADS
    head_shift = _log2(2 * dec_seq)
    key_shift = _log2(N_HEADS)
    assert PAGE_SIZE + 1 >= FAR_DIST and new_rows <= LANES

    @pl.when(jnp.logical_and(b == 0, g == 0))
    def _():
        tb = lambda bk: tblr_ref[:, bk:bk + 1]
        r = lax.broadcasted_iota(I32, (q_rows, PAGE_ROWS), 0)
        c = lax.broadcasted_iota(I32, (q_rows, PAGE_ROWS), 1)
        same_head = (c & (N_HEADS - 1)) == (r >> head_shift)
        tok = r & (dec_seq - 1)
        key = c >> key_shift
        bmask_ref[...] = jnp.where(same_head, 0.0, NEG_INF)
        blast_ref[...] = jnp.where(same_head, _bias_from_dist(tok + PAGE_SIZE - key, tb), NEG_INF)
        r2 = lax.broadcasted_iota(I32, (q_rows, LANES), 0)
        c2 = lax.broadcasted_iota(I32, (q_rows, LANES), 1)
        d2 = (r2 & (dec_seq - 1)) - (c2 >> key_shift)
        ok2 = jnp.logical_and((c2 & (N_HEADS - 1)) == (r2 >> head_shift),
                              jnp.logical_and(d2 >= 0, c2 < new_rows))
        bnew_ref[...] = jnp.where(ok2, _bias_from_dist(jnp.maximum(d2, 0), tb), NEG_INF)

    @pl.when(g == 0)
    def _():
        q = q_ref[0]
        lane = lax.broadcasted_iota(I32, (dec_seq, LANES), 1)
        for hh in range(N_HEADS):
            qh = q[:, hh * LANES:(hh + 1) * LANES]
            r0 = hh * 2 * dec_seq
            qf_ref[r0:r0 + dec_seq, :] = jnp.where(lane < D_QK, qh, 0.0)
            qf_ref[r0 + dec_seq:r0 + 2 * dec_seq, :] = jnp.where(lane < D_QK, 0.0, qh)
        qb_ref[...] = qf_ref[...].astype(BF16)
        knp_ref[...] = jnp.zeros(knp_ref.shape, F32)
        vnp_ref[...] = jnp.zeros(vnp_ref.shape, F32)
        knp_ref[0:new_rows, :] = kn_ref[0]
        vnp_ref[0:new_rows, :] = vn_ref[0]
        m_ref[...] = jnp.full(m_ref.shape, -jnp.inf, F32)
        l_ref[...] = jnp.zeros(l_ref.shape, F32)
        acc_ref[...] = jnp.zeros(acc_ref.shape, F32)

    qb = qb_ref[...]
    is_last = g == n_groups - 1

    def scores(k_rows):
        return lax.dot_general(qb, k_rows.astype(BF16), (((1,), (1,)), ((), ())), preferred_element_type=F32)

    def update(s_list, v_list):
        m_prev = m_ref[...]
        m_cur = functools.reduce(jnp.maximum, [jnp.max(s, axis=1, keepdims=True) for s in s_list])
        m_new = jnp.maximum(m_prev, m_cur)
        alpha = jnp.exp(m_prev - m_new)
        l_new = alpha * l_ref[...]
        pv = None
        for s, v_rows in zip(s_list, v_list):
            p = jnp.exp(s - jnp.concatenate([m_new] * (s.shape[1] // LANES), axis=1))
            l_new = l_new + jnp.sum(p, axis=1, keepdims=True)
            d = jnp.dot(p.astype(BF16), v_rows.astype(BF16), preferred_element_type=F32)
            pv = d if pv is None else pv + d
        acc_ref[...] = alpha * acc_ref[...] + pv
        l_ref[...] = l_new
        m_ref[...] = m_new

    s_list = [scores(k_refs[p][...]) + bmask_ref[...] for p in range(pg - 1)]
    s_list.append(scores(k_refs[pg - 1][...]) + jnp.where(is_last, blast_ref[...], bmask_ref[...]))
    update(s_list, [v_refs[p][...] for p in range(pg)])

    @pl.when(is_last)
    def _():
        update([scores(knp_ref[...]) + bnew_ref[...]], [vnp_ref[...]])
        lam = _lambda_value(lam_ref)
        for hh in range(N_HEADS):
            r0 = hh * 2 * dec_seq
            o1 = acc_ref[r0:r0 + dec_seq, :] / l_ref[r0:r0 + dec_seq, :]
            o2 = acc_ref[r0 + dec_seq:r0 + 2 * dec_seq, :] / l_ref[r0 + dec_seq:r0 + 2 * dec_seq, :]
            o_ref[0, :, hh * D_V:(hh + 1) * D_V] = _sub_norm(o1 - lam * o2, sw_ref[...])


def _attn_sample(q_s, kn_rows, vn_rows, cache_k_rows, cache_v_rows, page_table, rel_table, lam_vecs, subln_w):
    bsz, dec_seq, _ = q_s.shape
    n_pages = page_table.shape[1]
    pg = PAGES_PER_STEP
    assert n_pages % pg == 0
    n_groups = n_pages // pg
    q_rows = N_HEADS * 2 * dec_seq
    new_rows = dec_seq * N_HEADS
    tbl_rows = jnp.repeat(rel_table.T, 2 * dec_seq, axis=0)
    tbl_rows = jnp.pad(tbl_rows, ((0, 0), (0, LANES - N_BUCKETS)))
    per_b = lambda shape: pl.BlockSpec(shape, lambda b, g, pt: (b, 0, 0))
    const2 = lambda shape: pl.BlockSpec(shape, lambda b, g, pt: (0, 0))

    def page_spec(p):
        return pl.BlockSpec((PAGE_ROWS, LANES), lambda b, g, pt: (pt[b, g * pg + p], 0))

    grid_spec = pltpu.PrefetchScalarGridSpec(
        num_scalar_prefetch=1,
        grid=(bsz, n_groups),
        in_specs=[per_b((1, dec_seq, D_QKH)), per_b((1, new_rows, LANES)), per_b((1, new_rows, LANES)),
                  const2((q_rows, LANES)), const2((4, D_QK)), const2((1, D_V))]
                 + [page_spec(p) for p in range(pg)] + [page_spec(p) for p in range(pg)],
        out_specs=per_b((1, dec_seq, D_ATT)),
        scratch_shapes=[pltpu.VMEM((q_rows, LANES), F32), pltpu.VMEM((q_rows, LANES), BF16),
                        pltpu.VMEM((LANES, LANES), F32), pltpu.VMEM((LANES, D_V), F32),
                        pltpu.VMEM((q_rows, PAGE_ROWS), F32), pltpu.VMEM((q_rows, PAGE_ROWS), F32),
                        pltpu.VMEM((q_rows, LANES), F32),
                        pltpu.VMEM((q_rows, LANES), F32), pltpu.VMEM((q_rows, LANES), F32),
                        pltpu.VMEM((q_rows, D_V), F32)],
    )
    return pl.pallas_call(
        functools.partial(_attn_sample_kernel, n_groups=n_groups, dec_seq=dec_seq),
        grid_spec=grid_spec,
        out_shape=jax.ShapeDtypeStruct((bsz, dec_seq, D_ATT), F32),
        compiler_params=_cparams(("arbitrary", "arbitrary")),
        name="attn_sample",
    )(page_table, q_s, kn_rows, vn_rows, tbl_rows, lam_vecs, subln_w.reshape(1, D_V),
      *([cache_k_rows] * pg), *([cache_v_rows] * pg))


def _split_bf16(a):
    hi = a.astype(BF16)
    lo = (a - hi.astype(F32)).astype(BF16)
    return hi, lo


def _outproj_kernel(x_ref, yc_ref, o_ref, gt_ref, sc_ref, sh_ref, g_ref, wo_ref, wr_ref, br_ref, cin_ref,
                    x1_ref, h2_ref, idx_ref, gate_ref, rank_ref, cnt_ref):
    tm = x_ref.shape[0]

    @pl.when(pl.program_id(0) == 0)
    def _():
        cnt_ref[...] = cin_ref[...]

    mix = (jnp.dot(yc_ref[...].astype(BF16), wo_ref[0:D_CONV, :], preferred_element_type=F32)
           + jnp.dot(o_ref[...].astype(BF16), wo_ref[D_CONV:, :], preferred_element_type=F32))
    x1 = x_ref[...] + gt_ref[...] * mix
    x1_ref[...] = x1
    ms = jnp.mean(x1 * x1, axis=1, keepdims=True)
    h2 = x1 * lax.rsqrt(ms + EPS) * g_ref[...]
    h2 = h2 * (1.0 + sc_ref[...]) + sh_ref[...]
    h2_ref[...] = h2
    h_hi, h_lo = _split_bf16(h2)
    w_hi, w_lo = _split_bf16(wr_ref[...])
    logits = (jnp.dot(h_hi, w_hi, preferred_element_type=F32)
              + jnp.dot(h_hi, w_lo, preferred_element_type=F32)
              + jnp.dot(h_lo, w_hi, preferred_element_type=F32)) + br_ref[...]
    lane = lax.broadcasted_iota(I32, (tm, LANES), 1)
    lane_f = lane.astype(F32)
    vals, ids = [], []
    cur = logits
    for _ in range(TOP_K):
        mx = jnp.max(cur, axis=1, keepdims=True)
        ik = jnp.min(jnp.where(cur == mx, lane_f, float(LANES)), axis=1, keepdims=True)
        vals.append(mx)
        ids.append(ik)
        cur = jnp.where(lane_f == ik, -jnp.inf, cur)
    es = [jnp.exp(v - vals[0]) for v in vals]
    denom = functools.reduce(lambda a, c: a + c, es)
    sel = jnp.zeros((tm, LANES), F32)
    idx_out = jnp.zeros((tm, LANES), F32)
    gate_out = jnp.zeros((tm, LANES), F32)
    for k in range(TOP_K):
        sel = sel + jnp.where(lane_f == ids[k], 1.0, 0.0)
        idx_out = jnp.where(lane == k, ids[k], idx_out)
        gate_out = jnp.where(lane == k, es[k] / denom, gate_out)
    r = lax.broadcasted_iota(I32, (tm, tm), 0)
    c = lax.broadcasted_iota(I32, (tm, tm), 1)
    lower = jnp.where(r > c, 1.0, 0.0).astype(BF16)
    before = jnp.dot(lower, sel.astype(BF16), preferred_element_type=F32) + cnt_ref[...]
    rank_out = jnp.zeros((tm, LANES), F32)
    for k in range(TOP_K):
        rk = jnp.sum(jnp.where(lane_f == ids[k], before, 0.0), axis=1, keepdims=True)
        rank_out = jnp.where(lane == k, rk, rank_out)
    cnt_ref[...] = cnt_ref[...] + jnp.sum(sel, axis=0, keepdims=True)
    idx_ref[...] = idx_out.astype(I32)
    gate_ref[...] = gate_out
    rank_ref[...] = rank_out.astype(I32)


def _outproj(x, yc, o, gt, sc, sh, g_ffn, w_out_bf, w_router_pad, b_router_pad, cnt_in, *, tm):
    n = x.shape[0]
    per_row = gt.shape[0] != 1
    mod_spec = pl.BlockSpec((tm, D_MODEL), lambda i: (i, 0)) if per_row else pl.BlockSpec((1, D_MODEL), lambda i: (0, 0))
    const = lambda shape: pl.BlockSpec(shape, lambda i: (0, 0))
    rows = lambda w: pl.BlockSpec((tm, w), lambda i: (i, 0))
    return pl.pallas_call(
        _outproj_kernel,
        grid=(n // tm,),
        in_specs=[rows(D_MODEL), rows(D_CONV), rows(D_ATT), mod_spec, mod_spec, mod_spec, const((1, D_MODEL)),
                  const((D_MODEL, D_MODEL)), const((D_MODEL, LANES)), const((1, LANES)), const((1, LANES))],
        out_specs=[rows(D_MODEL), rows(D_MODEL), rows(LANES), rows(LANES), rows(LANES), const((1, LANES))],
        out_shape=[jax.ShapeDtypeStruct((n, D_MODEL), F32),
                   jax.ShapeDtypeStruct((n, D_MODEL), F32),
                   jax.ShapeDtypeStruct((n, LANES), I32),
                   jax.ShapeDtypeStruct((n, LANES), F32),
                   jax.ShapeDtypeStruct((n, LANES), I32),
                   jax.ShapeDtypeStruct((1, LANES), F32)],
        compiler_params=_cparams(("arbitrary",)),
        name="outproj",
    )(x, yc, o, gt, sc, sh, g_ffn.reshape(1, D_MODEL), w_out_bf, w_router_pad, b_router_pad, cnt_in)


def _push_kernel(pend_ref, cnt_ref, dest_hbm, hp_ref, hs_ref, buf_out, idx_smem, zero_ref, isem, rsem, zsem,
                 *, n_steps, n_p_steps, n_blocks):
    i = pl.program_id(0)
    tm = hp_ref.shape[0]

    @pl.when(i == 0)
    def _():
        zero_ref[...] = jnp.zeros(zero_ref.shape, F32)
        blk_shift = _log2(MOE_BLOCK)
        n_used = pend_ref[N_EXPERTS - 1] >> blk_shift

        def zero_copy(block):
            start = pl.multiple_of(block * MOE_BLOCK, MOE_BLOCK)
            return pltpu.make_async_copy(zero_ref, buf_out.at[pl.ds(start, MOE_BLOCK), :], zsem)

        def targets(e):
            return ((cnt_ref[e] > 0, (pend_ref[e] >> blk_shift) - 1), (n_used + e < n_blocks, n_used + e))

        for wait in (False, True):
            for e in range(N_EXPERTS):
                for cond, block in targets(e):
                    @pl.when(cond)
                    def _():
                        if wait:
                            zero_copy(block).wait()
                        else:
                            zero_copy(block).start()

    def idx_copy(step, sl):
        return pltpu.make_async_copy(dest_hbm.at[step], idx_smem.at[sl], isem.at[sl])

    slot = i % 2

    @pl.when(i == 0)
    def _():
        idx_copy(0, 0).start()

    idx_copy(i, slot).wait()

    @pl.when(i + 1 < n_steps)
    def _():
        idx_copy(i + 1, 1 - slot).start()

    def push_tile(h_ref):
        def row_copy(t, dst_row):
            return pltpu.make_async_copy(h_ref.at[pl.ds(t, 1), :], buf_out.at[pl.ds(dst_row, 1), :], rsem)

        def start_body(t, carry):
            for k in range(TOP_K):
                row_copy(t, idx_smem[slot, k * tm + t]).start(priority=k % 2)
            return carry

        lax.fori_loop(0, tm, start_body, 0, unroll=DMA_UNROLL)

        def wait_body(t, carry):
            for k in range(TOP_K):
                row_copy(t, 0).wait()
            return carry

        lax.fori_loop(0, tm, wait_body, 0, unroll=DMA_UNROLL)

    @pl.when(i < n_p_steps)
    def _():
        push_tile(hp_ref)

    @pl.when(i >= n_p_steps)
    def _():
        push_tile(hs_ref)


def _moe_push(h2_p, h2_s, dest_tiles, pend, counts, *, n_blocks):
    tm = TM_ROW
    n_p_steps = h2_p.shape[0] // tm
    n_steps = n_p_steps + h2_s.shape[0] // tm
    assert dest_tiles.shape[0] == n_steps
    any_spec = pl.BlockSpec(memory_space=pl.ANY)
    grid_spec = pltpu.PrefetchScalarGridSpec(
        num_scalar_prefetch=2,
        grid=(n_steps,),
        in_specs=[any_spec,
                  pl.BlockSpec((tm, D_MODEL), lambda i, pe, cn: (jnp.minimum(i, n_p_steps - 1), 0)),
                  pl.BlockSpec((tm, D_MODEL), lambda i, pe, cn: (jnp.maximum(i - n_p_steps, 0), 0))],
        out_specs=any_spec,
        scratch_shapes=[pltpu.SMEM((2, TOP_K * tm), I32), pltpu.VMEM((MOE_BLOCK, D_MODEL), F32),
                        pltpu.SemaphoreType.DMA((2,)), pltpu.SemaphoreType.DMA(()), pltpu.SemaphoreType.DMA(())],
    )
    return pl.pallas_call(
        functools.partial(_push_kernel, n_steps=n_steps, n_p_steps=n_p_steps, n_blocks=n_blocks),
        grid_spec=grid_spec,
        out_shape=jax.ShapeDtypeStruct((n_blocks * MOE_BLOCK, D_MODEL), F32),
        compiler_params=_cparams(("arbitrary",)),
        name="moe_push",
    )(pend, counts, dest_tiles, h2_p, h2_s)


def _expert_kernel(be_ref, nu_ref, x_ref, wg_ref, bg_ref, wu_ref, bu_ref, wd_ref, bd_ref, o_ref,
                   wg_bf, wu_bf, wd_bf):
    i = pl.program_id(0)
    n_used = nu_ref[0]
    prev = be_ref[jnp.maximum(i - 1, 0)]
    new_expert = jnp.logical_or(i == 0, be_ref[i] != prev)

    @pl.when(jnp.logical_and(i < n_used, new_expert))
    def _():
        wg_bf[...] = wg_ref[0].astype(BF16)
        wu_bf[...] = wu_ref[0].astype(BF16)
        wd_bf[...] = wd_ref[0].astype(BF16)

    @pl.when(i < n_used)
    def _():
        x = x_ref[...].astype(BF16)
        g = jnp.dot(x, wg_bf[...], preferred_element_type=F32) + bg_ref[0]
        u = jnp.dot(x, wu_bf[...], preferred_element_type=F32) + bu_ref[0]
        g = jnp.minimum(g, SWIGLU_LIMIT)
        u = jnp.clip(u, -SWIGLU_LIMIT, SWIGLU_LIMIT)
        a = g * jax.nn.sigmoid(SWIGLU_ALPHA * g) * (u + 1.0)
        o_ref[...] = jnp.dot(a.astype(BF16), wd_bf[...], preferred_element_type=F32) + bd_ref[0]

    @pl.when(i >= n_used)
    def _():
        o_ref[...] = jnp.zeros(o_ref.shape, F32)


def _moe_expert(buf, blk_e, n_used, w_gate, b_gate, w_up, b_up, w_down, b_down):
    rows = buf.shape[0]
    n_blocks = rows // MOE_BLOCK
    d_ff = w_gate.shape[2]

    def blk(i, be, nu):
        return jnp.minimum(i, nu[0] - 1)

    xspec = pl.BlockSpec((MOE_BLOCK, D_MODEL), lambda i, be, nu: (blk(i, be, nu), 0))
    wspec = lambda a, b: pl.BlockSpec((1, a, b), lambda i, be, nu: (be[blk(i, be, nu)], 0, 0))
    grid_spec = pltpu.PrefetchScalarGridSpec(
        num_scalar_prefetch=2,
        grid=(n_blocks,),
        in_specs=[xspec, wspec(D_MODEL, d_ff), wspec(1, d_ff), wspec(D_MODEL, d_ff), wspec(1, d_ff),
                  wspec(d_ff, D_MODEL), wspec(1, D_MODEL)],
        out_specs=pl.BlockSpec((MOE_BLOCK, D_MODEL), lambda i, be, nu: (i, 0)),
        scratch_shapes=[pltpu.VMEM((D_MODEL, d_ff), BF16), pltpu.VMEM((D_MODEL, d_ff), BF16),
                        pltpu.VMEM((d_ff, D_MODEL), BF16)],
    )
    return pl.pallas_call(
        _expert_kernel,
        grid_spec=grid_spec,
        out_shape=jax.ShapeDtypeStruct((rows, D_MODEL), F32),
        compiler_params=_cparams(("arbitrary",)),
        name="moe_expert",
    )(blk_e, n_used, buf, w_gate, b_gate.reshape(N_EXPERTS, 1, d_ff), w_up, b_up.reshape(N_EXPERTS, 1, d_ff),
      w_down, b_down.reshape(N_EXPERTS, 1, D_MODEL))


def _combine_kernel(dest_hbm, out_hbm, x1_ref, gate_ref, gt_ref, y_ref, idx_smem, rows_ref, isem, rsem, *, n_steps):
    i = pl.program_id(0)
    tm = x1_ref.shape[0]
    n_rows = TOP_K * tm
    slot = i % 2

    def idx_copy(step, sl):
        return pltpu.make_async_copy(dest_hbm.at[step], idx_smem.at[sl], isem.at[sl])

    def row_copy(src_row, sl, r):
        return pltpu.make_async_copy(out_hbm.at[pl.ds(src_row, 1), :], rows_ref.at[sl, pl.ds(r, 1), :], rsem.at[sl])

    def issue_rows(sl):
        def body(r2, carry):
            for par in range(2):
                r = 2 * r2 + par
                row_copy(idx_smem[sl, r], sl, r).start(priority=par)
            return carry
        lax.fori_loop(0, n_rows // 2, body, 0, unroll=DMA_UNROLL)

    @pl.when(i == 0)
    def _():
        idx_copy(0, 0).start()
        idx_copy(0, 0).wait()
        issue_rows(0)
        if n_steps > 1:
            idx_copy(1, 1).start()

    @pl.when(i + 1 < n_steps)
    def _():
        idx_copy(i + 1, 1 - slot).wait()
        issue_rows(1 - slot)

    @pl.when(i + 2 < n_steps)
    def _():
        idx_copy(i + 2, slot).start()

    def wait_body(r, carry):
        row_copy(0, slot, r).wait()
        return carry

    lax.fori_loop(0, n_rows, wait_body, 0, unroll=DMA_UNROLL)

    gates = gate_ref[...]
    y = jnp.zeros((tm, D_MODEL), F32)
    for k in range(TOP_K):
        y = y + gates[:, k:k + 1] * rows_ref[slot, k * tm:(k + 1) * tm, :]
    y_ref[...] = x1_ref[...] + gt_ref[...] * y


def _moe_combine(out_rows, dest_tiles, x1, gates, gt):
    n = x1.shape[0]
    tm = TM_ROW
    n_steps = n // tm
    per_row = gt.shape[0] != 1
    mod_spec = pl.BlockSpec((tm, D_MODEL), lambda i: (i, 0)) if per_row else pl.BlockSpec((1, D_MODEL), lambda i: (0, 0))
    return pl.pallas_call(
        functools.partial(_combine_kernel, n_steps=n_steps),
        grid=(n_steps,),
        in_specs=[pl.BlockSpec(memory_space=pl.ANY), pl.BlockSpec(memory_space=pl.ANY),
                  pl.BlockSpec((tm, D_MODEL), lambda i: (i, 0)),
                  pl.BlockSpec((tm, LANES), lambda i: (i, 0)),
                  mod_spec],
        out_specs=pl.BlockSpec((tm, D_MODEL), lambda i: (i, 0)),
        out_shape=jax.ShapeDtypeStruct((n, D_MODEL), F32),
        scratch_shapes=[pltpu.SMEM((2, TOP_K * tm), I32), pltpu.VMEM((2, TOP_K * tm, D_MODEL), F32),
                        pltpu.SemaphoreType.DMA((2,)), pltpu.SemaphoreType.DMA((2,))],
        compiler_params=_cparams(("arbitrary",)),
        name="moe_combine",
    )(dest_tiles, out_rows, x1, gates, gt)


def _dest_tiles(dest, tm):
    n = dest.shape[0]
    return dest.reshape(n // tm, tm, TOP_K).transpose(0, 2, 1).reshape(n // tm, TOP_K * tm)


def kernel(x_prompt, x_sample, cache_k, cache_v, state_conv, page_table, c_prompt, c_sample, rel_table, w_ada, b_ada, g_mix, w_in, conv_w, q_norm_w, k_norm_w, lam_q1, lam_k1, lam_q2, lam_k2, subln_w, w_out, g_ffn, w_router, b_router, w_gate, b_gate, w_up, b_up, w_down, b_down):
    assert w_ada.shape[0] == 1, "single-layer trunk"
    bp, t_p, _ = x_prompt.shape
    bs, t_s, _ = x_sample.shape
    assert bp == 1
    n_s = bs * t_s
    n_all = t_p + n_s
    l = 0

    n_c = bp + bs
    c_rows = -(-n_c // SUBLANES) * SUBLANES
    c_all = jnp.concatenate([c_prompt, c_sample, jnp.zeros((c_rows - n_c, D_MODEL), F32)], axis=0)
    mod = _adaln(c_all, w_ada[l], b_ada[l]).reshape(c_rows, 6, D_MODEL)
    mod_p = [mod[0:1, j, :] for j in range(6)]
    per_token = jnp.broadcast_to(mod[1:1 + bs, None, :, :], (bs, t_s, 6, D_MODEL)).reshape(n_s, 6, D_MODEL)
    mod_s = [per_token[:, j, :] for j in range(6)]

    w_in_bf = w_in[l].astype(BF16)
    w_out_bf = w_out[l].astype(BF16)
    lam_vecs = jnp.stack([lam_q1[l], lam_k1[l], lam_q2[l], lam_k2[l]])

    xp = x_prompt.reshape(t_p, D_MODEL)
    xs = x_sample.reshape(n_s, D_MODEL)
    yc_p, q_p, k_p, v_p, kb_p, vb_p, u_tail = _inproj(
        xp, mod_p[1], mod_p[0], g_mix[l], w_in_bf, conv_w[l], q_norm_w[l], k_norm_w[l], None, None,
        tm=TM_IN, seq_len=None)
    st = state_conv[l]
    prev2 = jnp.broadcast_to(st[:, None, 0, :], (bs, t_s, D_CONV)).reshape(n_s, D_CONV)
    prev1 = jnp.broadcast_to(st[:, None, 1, :], (bs, t_s, D_CONV)).reshape(n_s, D_CONV)
    yc_s, q_s, k_s, v_s, _, _, u_s = _inproj(
        xs, mod_s[1], mod_s[0], g_mix[l], w_in_bf, conv_w[l], q_norm_w[l], k_norm_w[l], prev2, prev1,
        tm=n_s, seq_len=t_s)

    o_p = _attn_prompt(q_p, kb_p, vb_p, rel_table, lam_vecs, subln_w[l])
    n_phys = cache_k.shape[1]
    o_s = _attn_sample(q_s.astype(F32).reshape(bs, t_s, D_QKH),
                       k_s.reshape(bs, t_s * N_HEADS, 2 * D_QK), v_s.reshape(bs, t_s * N_HEADS, D_V),
                       cache_k[l].reshape(n_phys * PAGE_ROWS, 2 * D_QK), cache_v[l].reshape(n_phys * PAGE_ROWS, D_V),
                       page_table, rel_table, lam_vecs, subln_w[l]).reshape(n_s, D_ATT)

    w_router_pad = jnp.pad(w_router[l], ((0, 0), (0, LANES - N_EXPERTS)))
    b_router_pad = jnp.concatenate([b_router[l], jnp.full((LANES - N_EXPERTS,), NEG_INF, F32)]).reshape(1, LANES)
    cnt0 = jnp.zeros((1, LANES), F32)
    x1_p, h2_p, idx_p, gate_p, rank_p, cnt1 = _outproj(
        xp, yc_p, o_p, mod_p[2], mod_p[4], mod_p[3], g_ffn[l], w_out_bf, w_router_pad, b_router_pad, cnt0, tm=TM_OUT)
    x1_s, h2_s, idx_s, gate_s, rank_s, cnt2 = _outproj(
        xs, yc_s, o_s, mod_s[2], mod_s[4], mod_s[3], g_ffn[l], w_out_bf, w_router_pad, b_router_pad, cnt1, tm=TM_OUT)

    counts = cnt2[0, :N_EXPERTS].astype(I32)
    padded = (counts + MOE_BLOCK - 1) // MOE_BLOCK * MOE_BLOCK
    pend = jnp.cumsum(padded).astype(I32)
    pstart = pend - padded
    n_blocks = (n_all * TOP_K) // MOE_BLOCK + N_EXPERTS
    dest_p = pstart[idx_p[:, :TOP_K]] + rank_p[:, :TOP_K]
    dest_s = pstart[idx_s[:, :TOP_K]] + rank_s[:, :TOP_K]
    block_start = jnp.arange(n_blocks, dtype=I32) * MOE_BLOCK
    blk_e = jnp.minimum(jnp.sum(pend[None, :] <= block_start[:, None], axis=1), N_EXPERTS - 1).astype(I32)
    n_used = (pend[-1] // MOE_BLOCK).astype(I32).reshape(1)
    tiles_p = _dest_tiles(dest_p, TM_ROW)
    tiles_s = _dest_tiles(dest_s, TM_ROW)

    buf = _moe_push(h2_p, h2_s, jnp.concatenate([tiles_p, tiles_s], axis=0), pend, counts, n_blocks=n_blocks)
    out_rows = _moe_expert(buf, blk_e, n_used, w_gate[l], b_gate[l], w_up[l], b_up[l], w_down[l], b_down[l])
    y_p = _moe_combine(out_rows, tiles_p, x1_p, gate_p, mod_p[5])
    y_s = _moe_combine(out_rows, tiles_s, x1_s, gate_s, mod_s[5])

    k_prompt = k_p.reshape(1, bp, t_p, N_HEADS, 2 * D_QK)
    v_prompt = v_p.reshape(1, bp, t_p, N_HEADS, D_V)
    conv_prompt = u_tail[SUBLANES - (CONV_W - 1):, :].reshape(1, bp, CONV_W - 1, D_CONV)
    k_sample = k_s.reshape(1, bs, t_s, N_HEADS, 2 * D_QK)
    v_sample = v_s.reshape(1, bs, t_s, N_HEADS, D_V)
    conv_sample = u_s.reshape(bs, t_s, D_CONV)[:, t_s - (CONV_W - 1):, :].reshape(1, bs, CONV_W - 1, D_CONV)
    return (y_p.reshape(bp, t_p, D_MODEL), y_s.reshape(bs, t_s, D_MODEL),
            k_prompt, v_prompt, conv_prompt, k_sample, v_sample, conv_sample)
```

```python
import functools
import math

import numpy as np
import jax
import jax.numpy as jnp
from jax import lax
from jax.experimental import pallas as pl
from jax.experimental.pallas import tpu as pltpu

F32 = jnp.float32
BF16 = jnp.bfloat16
I32 = jnp.int32

D_MODEL = 1024
D_CONV = 512
CONV_W = 3
N_HEADS = 4
D_QK = 64
D_V = 128
D_ATT = N_HEADS * D_V
D_QKH = N_HEADS * 2 * D_QK
D_IN = 3 * D_CONV + 2 * D_QKH + D_ATT
N_BUCKETS = 32
MAX_EXACT = 16
MAX_DIST = 128
N_EXPERTS = 32
TOP_K = 4
SWIGLU_LIMIT = 7.0
SWIGLU_ALPHA = 1.702
PAGE_SIZE = 128
EPS = 1e-6
NEG_INF = -1e30
LAM_INIT = 0.8 - 0.6 * math.exp(-0.3 * 0)
LOG2E = math.log2(math.e)
Q_SCALE = D_QK ** -0.5 * LOG2E
SCORE_HEADROOM = 60.0

LANES = 128
SUBLANES = 8
VMEM_LIMIT = 56 * 1024 * 1024

TM_IN = 512
TQ = 512
TK = 512
PAGES_PER_STEP = 16
TM_OUT = 256
MOE_BLOCK = 256
TM_ROW = 128
DMA_UNROLL = 4
assert TK == TM_IN


def _bucket_thresholds():
    n = np.arange(0, 4 * MAX_DIST)
    nf = np.maximum(n, 1).astype(np.float32)
    val = np.log(nf / np.float32(MAX_EXACT)) / np.float32(math.log(MAX_DIST / MAX_EXACT)) * np.float32(N_BUCKETS - MAX_EXACT)
    large = np.minimum(MAX_EXACT + val.astype(np.int32), N_BUCKETS - 1)
    bucket = np.where(n < MAX_EXACT, n, large)
    assert np.all(np.diff(bucket) >= 0) and bucket[-1] == N_BUCKETS - 1
    thr = [int(np.argmax(bucket >= b)) for b in range(N_BUCKETS)]
    return thr


BUCKET_THR = _bucket_thresholds()
FAR_DIST = BUCKET_THR[N_BUCKETS - 1]


def _log2(n):
    assert n > 0 and n & (n - 1) == 0
    return n.bit_length() - 1


def _cparams(sem):
    return pltpu.CompilerParams(dimension_semantics=sem, vmem_limit_bytes=VMEM_LIMIT)


def _adaln_kernel(c_ref, w_ref, b_ref, o_ref):
    c = c_ref[...]
    s = c * jax.nn.sigmoid(c)
    o_ref[...] = jnp.dot(s, w_ref[...], preferred_element_type=F32,
                         precision=lax.Precision.HIGHEST) + b_ref[...]


def _adaln(c_all, w_ada, b_ada):
    rows = c_all.shape[0]
    n = w_ada.shape[1]
    bn = D_MODEL
    return pl.pallas_call(
        _adaln_kernel,
        grid=(n // bn,),
        in_specs=[pl.BlockSpec((rows, D_MODEL), lambda j: (0, 0)),
                  pl.BlockSpec((D_MODEL, bn), lambda j: (0, j)),
                  pl.BlockSpec((1, bn), lambda j: (0, j))],
        out_specs=pl.BlockSpec((rows, bn), lambda j: (0, j)),
        out_shape=jax.ShapeDtypeStruct((rows, n), F32),
        compiler_params=_cparams(("arbitrary",)),
        name="adaln",
    )(c_all, w_ada, b_ada.reshape(1, n))


def _group_rms(xb, w_row):
    lane = lax.broadcasted_iota(I32, xb.shape, 1)
    lo_half = lane < D_QK
    sq = xb * xb
    lo = jnp.sum(jnp.where(lo_half, sq, 0.0), axis=1, keepdims=True)
    hi = jnp.sum(jnp.where(lo_half, 0.0, sq), axis=1, keepdims=True)
    ms = jnp.where(lo_half, lo, hi) * (1.0 / D_QK)
    return xb * lax.rsqrt(ms + EPS) * w_row


def _inproj_kernel(x_ref, sc_ref, sh_ref, g_ref, w_ref, cw_ref, qw_ref, kw_ref, p2_ref, p1_ref,
                   yc_ref, q_ref, k_ref, v_ref, kb_ref, vt_ref, u_ref, carry_ref, *, seq_len):
    tm = x_ref.shape[0]
    x = x_ref[...]
    ms = jnp.mean(x * x, axis=1, keepdims=True)
    h = x * lax.rsqrt(ms + EPS) * g_ref[...]
    h = h * (1.0 + sc_ref[...]) + sh_ref[...]
    z = jnp.dot(h.astype(BF16), w_ref[...], preferred_element_type=F32)
    gb = z[:, 0:D_CONV]
    u = z[:, D_CONV:2 * D_CONV] * z[:, 2 * D_CONV:3 * D_CONV]
    row = lax.broadcasted_iota(I32, (tm, D_CONV), 0)
    u1 = pltpu.roll(u, 1, 0)
    u2 = pltpu.roll(u, 2, 0)
    if seq_len is None:
        @pl.when(pl.program_id(0) == 0)
        def _():
            carry_ref[...] = jnp.zeros_like(carry_ref)
        prev2 = carry_ref[SUBLANES - 2:SUBLANES - 1, :]
        prev1 = carry_ref[SUBLANES - 1:SUBLANES, :]
        u1 = jnp.where(row == 0, prev1, u1)
        u2 = jnp.where(row == 0, prev2, jnp.where(row == 1, prev1, u2))
        carry_ref[...] = u[tm - SUBLANES:tm, :]
        u_ref[...] = u[tm - SUBLANES:tm, :]
    else:
        pos = row & (seq_len - 1)
        u1 = jnp.where(pos == 0, p1_ref[...], u1)
        u2 = jnp.where(pos == 0, p2_ref[...], jnp.where(pos == 1, p1_ref[...], u2))
        u_ref[...] = u
    cw = cw_ref[...]
    yc = gb * (cw[0:1, :] * u2 + cw[1:2, :] * u1 + cw[2:3, :] * u)
    yc_ref[...] = yc.astype(yc_ref.dtype)
    o1 = 3 * D_CONV
    for hh in range(N_HEADS):
        c0 = o1 + hh * LANES
        qn = _group_rms(z[:, c0:c0 + LANES], qw_ref[...])
        q_ref[:, hh * LANES:(hh + 1) * LANES] = (qn * Q_SCALE).astype(q_ref.dtype)
        c1 = o1 + D_QKH + hh * LANES
        kn = _group_rms(z[:, c1:c1 + LANES], kw_ref[...])
        k_ref[pl.ds(hh, tm, stride=N_HEADS), :] = kn
        kb_ref[:, hh * LANES:(hh + 1) * LANES] = kn.astype(BF16)
    v = z[:, o1 + 2 * D_QKH:]
    for hh in range(N_HEADS):
        vh = v[:, hh * D_V:(hh + 1) * D_V]
        v_ref[pl.ds(hh, tm, stride=N_HEADS), :] = vh
        vt_ref[0, hh * D_V:(hh + 1) * D_V, :] = vh.T.astype(BF16)


def _inproj(x, sc, sh, g_mix, w_in_bf, conv_w, qw, kw, prev2, prev1, *, tm, seq_len):
    n = x.shape[0]
    grid = (n // tm,)
    per_row = sc.shape[0] != 1
    mod_spec = pl.BlockSpec((tm, D_MODEL), lambda i: (i, 0)) if per_row else pl.BlockSpec((1, D_MODEL), lambda i: (0, 0))
    if seq_len is None:
        prev2 = jnp.zeros((SUBLANES, D_CONV), F32)
        prev1 = prev2
        prev_spec = pl.BlockSpec((SUBLANES, D_CONV), lambda i: (0, 0))
        u_shape = jax.ShapeDtypeStruct((SUBLANES, D_CONV), F32)
        u_spec = pl.BlockSpec((SUBLANES, D_CONV), lambda i: (0, 0))
    else:
        assert seq_len & (seq_len - 1) == 0
        prev_spec = pl.BlockSpec((tm, D_CONV), lambda i: (i, 0))
        u_shape = jax.ShapeDtypeStruct((n, D_CONV), F32)
        u_spec = pl.BlockSpec((tm, D_CONV), lambda i: (i, 0))
    const = lambda shape: pl.BlockSpec(shape, lambda i: (0, 0))
    rows = lambda w: pl.BlockSpec((tm, w), lambda i: (i, 0))
    head_rows = pl.BlockSpec((tm * N_HEADS, LANES), lambda i: (i, 0))
    qw2 = jnp.tile(qw.reshape(1, D_QK), (1, 2))
    kw2 = jnp.tile(kw.reshape(1, D_QK), (1, 2))
    return pl.pallas_call(
        functools.partial(_inproj_kernel, seq_len=seq_len),
        grid=grid,
        in_specs=[rows(D_MODEL), mod_spec, mod_spec, const((1, D_MODEL)), const((D_MODEL, D_IN)),
                  const((CONV_W, D_CONV)), const((1, LANES)), const((1, LANES)), prev_spec, prev_spec],
        out_specs=[rows(D_CONV), rows(D_QKH), head_rows, head_rows, rows(D_QKH),
                   pl.BlockSpec((1, D_ATT, tm), lambda i: (i, 0, 0)), u_spec],
        out_shape=[jax.ShapeDtypeStruct((n, D_CONV), BF16),
                   jax.ShapeDtypeStruct((n, D_QKH), BF16),
                   jax.ShapeDtypeStruct((n * N_HEADS, 2 * D_QK), F32),
                   jax.ShapeDtypeStruct((n * N_HEADS, D_V), F32),
                   jax.ShapeDtypeStruct((n, D_QKH), BF16),
                   jax.ShapeDtypeStruct((n // tm, D_ATT, tm), BF16),
                   u_shape],
        scratch_shapes=[pltpu.VMEM((SUBLANES, D_CONV), F32)],
        compiler_params=_cparams(("arbitrary",)),
        name="inproj_seq" if seq_len is None else "inproj_batch",
    )(x, sc, sh, g_mix.reshape(1, D_MODEL), w_in_bf, conv_w, qw2, kw2, prev2, prev1)


def _bias_from_dist(dist, table_of_bucket):
    b = jnp.zeros(dist.shape, F32) + table_of_bucket(0)
    for bk in range(1, N_BUCKETS):
        b = jnp.where(dist >= BUCKET_THR[bk], table_of_bucket(bk), b)
    return (b - table_of_bucket(N_BUCKETS - 1)) * LOG2E


def _lambda_value(lam_ref):
    lq1, lk1, lq2, lk2 = (lam_ref[i:i + 1, :] for i in range(4))
    return (jnp.exp(jnp.sum(lq1 * lk1, axis=1, keepdims=True))
            - jnp.exp(jnp.sum(lq2 * lk2, axis=1, keepdims=True)) + LAM_INIT)


def _sub_norm(o, sw_row):
    ms = jnp.mean(o * o, axis=1, keepdims=True)
    return o * lax.rsqrt(ms + EPS) * sw_row * (1.0 - LAM_INIT)


def _attn_prompt_kernel(tbl_ref, q_ref, k_ref, vt_ref, lam_ref, sw_ref, o_ref,
                        bdiag_ref, bprev_ref, m_ref, l_ref, acc_ref):
    h = pl.program_id(0)
    i = pl.program_id(1)
    tq = q_ref.shape[0]
    tk = vt_ref.shape[2]
    assert tq == tk and tk >= FAR_DIST

    @pl.when(i == 0)
    def _():
        key = lax.broadcasted_iota(I32, (tk, tq), 0)
        qry = lax.broadcasted_iota(I32, (tk, tq), 1)
        d0 = qry - key
        tb = lambda b: tbl_ref[b, h]
        bdiag_ref[...] = jnp.where(d0 >= 0, _bias_from_dist(jnp.maximum(d0, 0), tb), NEG_INF)
        bprev_ref[...] = _bias_from_dist(d0 + tk, tb)

    m_ref[...] = jnp.full(m_ref.shape, -jnp.inf, F32)
    l_ref[...] = jnp.zeros(l_ref.shape, F32)
    acc_ref[...] = jnp.zeros(acc_ref.shape, F32)

    q = q_ref[...]
    lane = lax.broadcasted_iota(I32, q.shape, 1)
    zero = jnp.zeros_like(q)
    q_maps = (jnp.where(lane < D_QK, q, zero), jnp.where(lane < D_QK, zero, q))

    def scores_t(kb, mi):
        return lax.dot_general(kb, q_maps[mi], (((1,), (1,)), ((), ())), preferred_element_type=F32)

    def key_block(j):
        return k_ref[pl.ds(pl.multiple_of(j * tk, tk), tk), :], vt_ref[j]

    def exact_step(j, bias_ref):
        kb, vtb = key_block(j)
        for mi in range(2):
            st = scores_t(kb, mi)
            if bias_ref is not None:
                st = st + bias_ref[...]
            m_prev = m_ref[mi]
            m_new = jnp.maximum(m_prev, jnp.max(st, axis=0, keepdims=True))
            alpha = jnp.exp2(m_prev - m_new)
            pt = jnp.exp2(st - m_new)
            l_ref[mi] = alpha * l_ref[mi] + jnp.sum(pt, axis=0, keepdims=True)
            acc_ref[mi] = alpha * acc_ref[mi] + jnp.dot(vtb, pt.astype(BF16), preferred_element_type=F32)
            m_ref[mi] = m_new

    def streamed_step(j, n_blk):
        kb = k_ref[pl.ds(pl.multiple_of(j * tk, tk), n_blk * tk), :]
        sts = [scores_t(kb, mi) for mi in range(2)]
        parts = []
        worst = None
        for mi in range(2):
            ref_pt = m_ref[mi]
            pt = jnp.exp2(sts[mi] - ref_pt)
            blk_max = jnp.max(sts[mi], axis=0, keepdims=True)
            l_add = jnp.sum(pt, axis=0, keepdims=True)
            acc_add = functools.reduce(lambda a, b: a + b, [
                jnp.dot(vt_ref[j + u], pt[u * tk:(u + 1) * tk, :].astype(BF16), preferred_element_type=F32)
                for u in range(n_blk)])
            parts.append((ref_pt, blk_max, l_add, acc_add))
            excess = blk_max - ref_pt
            worst = excess if worst is None else jnp.maximum(worst, excess)
        in_range = jnp.max(worst) <= SCORE_HEADROOM

        @pl.when(in_range)
        def _():
            for mi, (ref_pt, blk_max, l_add, acc_add) in enumerate(parts):
                m_new = jnp.maximum(ref_pt, blk_max)
                alpha = jnp.exp2(ref_pt - m_new)
                l_ref[mi] = alpha * (l_ref[mi] + l_add)
                acc_ref[mi] = alpha * (acc_ref[mi] + acc_add)
                m_ref[mi] = m_new

        @pl.when(jnp.logical_not(in_range))
        def _():
            for u in range(n_blk):
                exact_step(j + u, None)

    exact_step(i, bdiag_ref)

    @pl.when(i > 0)
    def _():
        exact_step(i - 1, bprev_ref)

    n_far = jnp.maximum(i - 1, 0)

    def far_body(jj, carry):
        streamed_step(2 * jj, 2)
        return carry

    lax.fori_loop(0, n_far >> 1, far_body, 0)

    @pl.when((n_far & 1) == 1)
    def _():
        streamed_step(n_far - 1, 1)

    lam = _lambda_value(lam_ref)
    ot = acc_ref[0] / l_ref[0] - lam * (acc_ref[1] / l_ref[1])
    o_ref[...] = _sub_norm(ot.T, sw_ref[...]).astype(o_ref.dtype)


def _attn_prompt(q_bf, k_bf, vt_bf, rel_table, lam_vecs, subln_w):
    t = q_bf.shape[0]
    nk, _, tk = vt_bf.shape
    assert tk == TK and nk * tk == t
    nq = t // TQ
    grid_spec = pltpu.PrefetchScalarGridSpec(
        num_scalar_prefetch=0,
        grid=(N_HEADS, nq),
        in_specs=[pl.BlockSpec(memory_space=pltpu.SMEM),
                  pl.BlockSpec((TQ, LANES), lambda h, i: (i, h)),
                  pl.BlockSpec((t, LANES), lambda h, i: (0, h)),
                  pl.BlockSpec((nk, D_V, tk), lambda h, i: (0, h, 0)),
                  pl.BlockSpec((4, D_QK), lambda h, i: (0, 0)),
                  pl.BlockSpec((1, D_V), lambda h, i: (0, 0))],
        out_specs=pl.BlockSpec((TQ, D_V), lambda h, i: (i, h)),
        scratch_shapes=[pltpu.VMEM((TK, TQ), F32), pltpu.VMEM((TK, TQ), F32),
                        pltpu.VMEM((2, 1, TQ), F32), pltpu.VMEM((2, 1, TQ), F32),
                        pltpu.VMEM((2, D_V, TQ), F32)],
    )
    return pl.pallas_call(
        _attn_prompt_kernel,
        grid_spec=grid_spec,
        out_shape=jax.ShapeDtypeStruct((t, D_ATT), BF16),
        compiler_params=_cparams(("arbitrary", "arbitrary")),
        name="attn_prompt",
    )(rel_table, q_bf, k_bf, vt_bf, lam_vecs, subln_w.reshape(1, D_V))


PAGE_ROWS = PAGE_SIZE * N_HEADS


def _attn_sample_kernel(pt_ref, q_ref, kn_ref, vn_ref, tblr_ref, lam_ref, sw_ref, *rest, n_groups, dec_seq):
    pg = PAGES_PER_STEP
    k_refs = rest[:pg]
    v_refs = rest[pg:2 * pg]
    o_ref = rest[2 * pg]
    qf_ref, qb_ref, knp_ref, vnp_ref, bmask_ref, blast_ref, bnew_ref, m_ref, l_ref, acc_ref = rest[2 * pg + 1:]
    b = pl.program_id(0)
    g = pl.program_id(1)
    q_rows = N_HEADS * 2 * dec_seq
    new_rows = dec_seq * N_HEADS
    head_shift = _log2(2 * dec_seq)
    key_shift = _log2(N_HEADS)
    assert PAGE_SIZE + 1 >= FAR_DIST and new_rows <= LANES

    @pl.when(jnp.logical_and(b == 0, g == 0))
    def _():
        tb = lambda bk: tblr_ref[:, bk:bk + 1]
        r = lax.broadcasted_iota(I32, (q_rows, PAGE_ROWS), 0)
        c = lax.broadcasted_iota(I32, (q_rows, PAGE_ROWS), 1)
        same_head = (c & (N_HEADS - 1)) == (r >> head_shift)
        tok = r & (dec_seq - 1)
        key = c >> key_shift
        bmask_ref[...] = jnp.where(same_head, 0.0, NEG_INF)
        blast_ref[...] = jnp.where(same_head, _bias_from_dist(tok + PAGE_SIZE - key, tb), NEG_INF)
        r2 = lax.broadcasted_iota(I32, (q_rows, LANES), 0)
        c2 = lax.broadcasted_iota(I32, (q_rows, LANES), 1)
        d2 = (r2 & (dec_seq - 1)) - (c2 >> key_shift)
        ok2 = jnp.logical_and((c2 & (N_HEADS - 1)) == (r2 >> head_shift),
                              jnp.logical_and(d2 >= 0, c2 < new_rows))
        bnew_ref[...] = jnp.where(ok2, _bias_from_dist(jnp.maximum(d2, 0), tb), NEG_INF)

    @pl.when(g == 0)
    def _():
        q = q_ref[0]
        lane = lax.broadcasted_iota(I32, (dec_seq, LANES), 1)
        for hh in range(N_HEADS):
            qh = q[:, hh * LANES:(hh + 1) * LANES]
            r0 = hh * 2 * dec_seq
            qf_ref[r0:r0 + dec_seq, :] = jnp.where(lane < D_QK, qh, 0.0)
            qf_ref[r0 + dec_seq:r0 + 2 * dec_seq, :] = jnp.where(lane < D_QK, 0.0, qh)
        qb_ref[...] = qf_ref[...].astype(BF16)
        knp_ref[...] = jnp.zeros(knp_ref.shape, F32)
        vnp_ref[...] = jnp.zeros(vnp_ref.shape, F32)
        knp_ref[0:new_rows, :] = kn_ref[0]
        vnp_ref[0:new_rows, :] = vn_ref[0]
        m_ref[...] = jnp.full(m_ref.shape, -jnp.inf, F32)
        l_ref[...] = jnp.zeros(l_ref.shape, F32)
        acc_ref[...] = jnp.zeros(acc_ref.shape, F32)

    qb = qb_ref[...]
    is_last = g == n_groups - 1

    def scores(k_rows):
        return lax.dot_general(qb, k_rows.astype(BF16), (((1,), (1,)), ((), ())), preferred_element_type=F32)

    def update(s_list, v_list):
        m_prev = m_ref[...]
        m_cur = functools.reduce(jnp.maximum, [jnp.max(s, axis=1, keepdims=True) for s in s_list])
        m_new = jnp.maximum(m_prev, m_cur)
        alpha = jnp.exp2(m_prev - m_new)
        l_new = alpha * l_ref[...]
        pv = None
        for s, v_rows in zip(s_list, v_list):
            p = jnp.exp2(s - jnp.concatenate([m_new] * (s.shape[1] // LANES), axis=1))
            l_new = l_new + jnp.sum(p, axis=1, keepdims=True)
            d = jnp.dot(p.astype(BF16), v_rows.astype(BF16), preferred_element_type=F32)
            pv = d if pv is None else pv + d
        acc_ref[...] = alpha * acc_ref[...] + pv
        l_ref[...] = l_new
        m_ref[...] = m_new

    s_list = [scores(k_refs[p][...]) + bmask_ref[...] for p in range(pg - 1)]
    s_list.append(scores(k_refs[pg - 1][...]) + jnp.where(is_last, blast_ref[...], bmask_ref[...]))
    update(s_list, [v_refs[p][...] for p in range(pg)])

    @pl.when(is_last)
    def _():
        update([scores(knp_ref[...]) + bnew_ref[...]], [vnp_ref[...]])
        lam = _lambda_value(lam_ref)
        for hh in range(N_HEADS):
            r0 = hh * 2 * dec_seq
            o1 = acc_ref[r0:r0 + dec_seq, :] / l_ref[r0:r0 + dec_seq, :]
            o2 = acc_ref[r0 + dec_seq:r0 + 2 * dec_seq, :] / l_ref[r0 + dec_seq:r0 + 2 * dec_seq, :]
            o_ref[0, :, hh * D_V:(hh + 1) * D_V] = _sub_norm(o1 - lam * o2, sw_ref[...])


def _attn_sample(q_s, kn_rows, vn_rows, cache_k_rows, cache_v_rows, page_table, rel_table, lam_vecs, subln_w):
    bsz, dec_seq, _ = q_s.shape
    n_pages = page_table.shape[1]
    pg = PAGES_PER_STEP
    assert n_pages % pg == 0
    n_groups = n_pages // pg
    q_rows = N_HEADS * 2 * dec_seq
    new_rows = dec_seq * N_HEADS
    tbl_rows = jnp.repeat(rel_table.T, 2 * dec_seq, axis=0)
    tbl_rows = jnp.pad(tbl_rows, ((0, 0), (0, LANES - N_BUCKETS)))
    per_b = lambda shape: pl.BlockSpec(shape, lambda b, g, pt: (b, 0, 0))
    const2 = lambda shape: pl.BlockSpec(shape, lambda b, g, pt: (0, 0))

    def page_spec(p):
        return pl.BlockSpec((PAGE_ROWS, LANES), lambda b, g, pt: (pt[b, g * pg + p], 0))

    grid_spec = pltpu.PrefetchScalarGridSpec(
        num_scalar_prefetch=1,
        grid=(bsz, n_groups),
        in_specs=[per_b((1, dec_seq, D_QKH)), per_b((1, new_rows, LANES)), per_b((1, new_rows, LANES)),
                  const2((q_rows, LANES)), const2((4, D_QK)), const2((1, D_V))]
                 + [page_spec(p) for p in range(pg)] + [page_spec(p) for p in range(pg)],
        out_specs=per_b((1, dec_seq, D_ATT)),
        scratch_shapes=[pltpu.VMEM((q_rows, LANES), F32), pltpu.VMEM((q_rows, LANES), BF16),
                        pltpu.VMEM((LANES, LANES), F32), pltpu.VMEM((LANES, D_V), F32),
                        pltpu.VMEM((q_rows, PAGE_ROWS), F32), pltpu.VMEM((q_rows, PAGE_ROWS), F32),
                        pltpu.VMEM((q_rows, LANES), F32),
                        pltpu.VMEM((q_rows, LANES), F32), pltpu.VMEM((q_rows, LANES), F32),
                        pltpu.VMEM((q_rows, D_V), F32)],
    )
    return pl.pallas_call(
        functools.partial(_attn_sample_kernel, n_groups=n_groups, dec_seq=dec_seq),
        grid_spec=grid_spec,
        out_shape=jax.ShapeDtypeStruct((bsz, dec_seq, D_ATT), F32),
        compiler_params=_cparams(("arbitrary", "arbitrary")),
        name="attn_sample",
    )(page_table, q_s, kn_rows, vn_rows, tbl_rows, lam_vecs, subln_w.reshape(1, D_V),
      *([cache_k_rows] * pg), *([cache_v_rows] * pg))


def _split_bf16(a):
    hi = a.astype(BF16)
    lo = (a - hi.astype(F32)).astype(BF16)
    return hi, lo


def _outproj_kernel(x_ref, yc_ref, o_ref, gt_ref, sc_ref, sh_ref, g_ref, wo_ref, wr_ref, br_ref, cin_ref,
                    x1_ref, h2_ref, idx_ref, gate_ref, rank_ref, cnt_ref):
    tm = x_ref.shape[0]

    @pl.when(pl.program_id(0) == 0)
    def _():
        cnt_ref[...] = cin_ref[...]

    mix = (jnp.dot(yc_ref[...].astype(BF16), wo_ref[0:D_CONV, :], preferred_element_type=F32)
           + jnp.dot(o_ref[...].astype(BF16), wo_ref[D_CONV:, :], preferred_element_type=F32))
    x1 = x_ref[...] + gt_ref[...] * mix
    x1_ref[...] = x1
    ms = jnp.mean(x1 * x1, axis=1, keepdims=True)
    h2 = x1 * lax.rsqrt(ms + EPS) * g_ref[...]
    h2 = h2 * (1.0 + sc_ref[...]) + sh_ref[...]
    h2_ref[...] = h2
    h_hi, h_lo = _split_bf16(h2)
    w_hi, w_lo = _split_bf16(wr_ref[...])
    logits = (jnp.dot(h_hi, w_hi, preferred_element_type=F32)
              + jnp.dot(h_hi, w_lo, preferred_element_type=F32)
              + jnp.dot(h_lo, w_hi, preferred_element_type=F32)) + br_ref[...]
    lane = lax.broadcasted_iota(I32, (tm, LANES), 1)
    lane_f = lane.astype(F32)
    vals, ids = [], []
    cur = logits
    for _ in range(TOP_K):
        mx = jnp.max(cur, axis=1, keepdims=True)
        ik = jnp.min(jnp.where(cur == mx, lane_f, float(LANES)), axis=1, keepdims=True)
        vals.append(mx)
        ids.append(ik)
        cur = jnp.where(lane_f == ik, -jnp.inf, cur)
    es = [jnp.exp(v - vals[0]) for v in vals]
    denom = functools.reduce(lambda a, c: a + c, es)
    sel = jnp.zeros((tm, LANES), F32)
    idx_out = jnp.zeros((tm, LANES), F32)
    gate_out = jnp.zeros((tm, LANES), F32)
    for k in range(TOP_K):
        sel = sel + jnp.where(lane_f == ids[k], 1.0, 0.0)
        idx_out = jnp.where(lane == k, ids[k], idx_out)
        gate_out = jnp.where(lane == k, es[k] / denom, gate_out)
    r = lax.broadcasted_iota(I32, (tm, tm), 0)
    c = lax.broadcasted_iota(I32, (tm, tm), 1)
    lower = jnp.where(r > c, 1.0, 0.0).astype(BF16)
    before = jnp.dot(lower, sel.astype(BF16), preferred_element_type=F32) + cnt_ref[...]
    rank_out = jnp.zeros((tm, LANES), F32)
    for k in range(TOP_K):
        rk = jnp.sum(jnp.where(lane_f == ids[k], before, 0.0), axis=1, keepdims=True)
        rank_out = jnp.where(lane == k, rk, rank_out)
    cnt_ref[...] = cnt_ref[...] + jnp.sum(sel, axis=0, keepdims=True)
    idx_ref[...] = idx_out.astype(I32)
    gate_ref[...] = gate_out
    rank_ref[...] = rank_out.astype(I32)


def _outproj(x, yc, o, gt, sc, sh, g_ffn, w_out_bf, w_router_pad, b_router_pad, cnt_in, *, tm):
    n = x.shape[0]
    per_row = gt.shape[0] != 1
    mod_spec = pl.BlockSpec((tm, D_MODEL), lambda i: (i, 0)) if per_row else pl.BlockSpec((1, D_MODEL), lambda i: (0, 0))
    const = lambda shape: pl.BlockSpec(shape, lambda i: (0, 0))
    rows = lambda w: pl.BlockSpec((tm, w), lambda i: (i, 0))
    return pl.pallas_call(
        _outproj_kernel,
        grid=(n // tm,),
        in_specs=[rows(D_MODEL), rows(D_CONV), rows(D_ATT), mod_spec, mod_spec, mod_spec, const((1, D_MODEL)),
                  const((D_MODEL, D_MODEL)), const((D_MODEL, LANES)), const((1, LANES)), const((1, LANES))],
        out_specs=[rows(D_MODEL), rows(D_MODEL), rows(LANES), rows(LANES), rows(LANES), const((1, LANES))],
        out_shape=[jax.ShapeDtypeStruct((n, D_MODEL), F32),
                   jax.ShapeDtypeStruct((n, D_MODEL), F32),
                   jax.ShapeDtypeStruct((n, LANES), I32),
                   jax.ShapeDtypeStruct((n, LANES), F32),
                   jax.ShapeDtypeStruct((n, LANES), I32),
                   jax.ShapeDtypeStruct((1, LANES), F32)],
        compiler_params=_cparams(("arbitrary",)),
        name="outproj",
    )(x, yc, o, gt, sc, sh, g_ffn.reshape(1, D_MODEL), w_out_bf, w_router_pad, b_router_pad, cnt_in)


def _push_kernel(pend_ref, cnt_ref, dest_hbm, hp_ref, hs_ref, buf_out, idx0, idx1, zero_ref, isem, rsem, zsem,
                 *, n_steps, n_p_steps, n_blocks):
    i = pl.program_id(0)
    tm = hp_ref.shape[0] * SUBLANES
    idx_refs = (idx0, idx1)

    @pl.when(i == 0)
    def _():
        zero_ref[...] = jnp.zeros(zero_ref.shape, F32)
        blk_shift = _log2(MOE_BLOCK)
        n_used = pend_ref[N_EXPERTS - 1] >> blk_shift

        def zero_copy(block):
            start = pl.multiple_of(block * MOE_BLOCK, MOE_BLOCK)
            return pltpu.make_async_copy(zero_ref, buf_out.at[pl.ds(start, MOE_BLOCK), :], zsem)

        def targets(e):
            return ((cnt_ref[e] > 0, (pend_ref[e] >> blk_shift) - 1), (n_used + e < n_blocks, n_used + e))

        for wait in (False, True):
            for e in range(N_EXPERTS):
                for cond, block in targets(e):
                    @pl.when(cond)
                    def _():
                        if wait:
                            zero_copy(block).wait()
                        else:
                            zero_copy(block).start()

    def idx_copy(step, sl):
        return pltpu.make_async_copy(dest_hbm.at[step], idx_refs[sl], isem.at[sl])

    @pl.when(i == 0)
    def _():
        idx_copy(0, 0).start()

    def push_tile(h_ref, sl):
        idx_copy(i, sl).wait()

        @pl.when(i + 1 < n_steps)
        def _():
            idx_copy(i + 1, 1 - sl).start()

        def row_copy(g, u, dst_row):
            return pltpu.make_async_copy(h_ref.at[g, pl.ds(u, 1), :], buf_out.at[pl.ds(dst_row, 1), :], rsem)

        def start_body(g, carry):
            for u in range(SUBLANES):
                for k in range(TOP_K):
                    row_copy(g, u, idx_refs[sl][k * tm + g * SUBLANES + u]).start(priority=k % 2)
            return carry

        lax.fori_loop(0, tm // SUBLANES, start_body, 0)

        def wait_body(g, carry):
            for u in range(SUBLANES):
                for k in range(TOP_K):
                    row_copy(g, u, 0).wait()
            return carry

        lax.fori_loop(0, tm // SUBLANES, wait_body, 0)

    for sl in range(2):
        @pl.when(jnp.logical_and(i < n_p_steps, (i & 1) == sl))
        def _():
            push_tile(hp_ref, sl)

        @pl.when(jnp.logical_and(i >= n_p_steps, (i & 1) == sl))
        def _():
            push_tile(hs_ref, sl)


def _moe_push(h2_p, h2_s, dest_tiles, pend, counts, *, n_blocks):
    tm = TM_ROW
    n_p_steps = h2_p.shape[0] // tm
    n_steps = n_p_steps + h2_s.shape[0] // tm
    assert dest_tiles.shape[0] == n_steps
    groups = tm // SUBLANES
    any_spec = pl.BlockSpec(memory_space=pl.ANY)
    grid_spec = pltpu.PrefetchScalarGridSpec(
        num_scalar_prefetch=2,
        grid=(n_steps,),
        in_specs=[any_spec,
                  pl.BlockSpec((groups, SUBLANES, D_MODEL), lambda i, pe, cn: (jnp.minimum(i, n_p_steps - 1), 0, 0)),
                  pl.BlockSpec((groups, SUBLANES, D_MODEL), lambda i, pe, cn: (jnp.maximum(i - n_p_steps, 0), 0, 0))],
        out_specs=any_spec,
        scratch_shapes=[pltpu.SMEM((TOP_K * tm,), I32), pltpu.SMEM((TOP_K * tm,), I32),
                        pltpu.VMEM((MOE_BLOCK, D_MODEL), F32),
                        pltpu.SemaphoreType.DMA((2,)), pltpu.SemaphoreType.DMA(()), pltpu.SemaphoreType.DMA(())],
    )
    as_groups = lambda h: h.reshape(h.shape[0] // SUBLANES, SUBLANES, D_MODEL)
    return pl.pallas_call(
        functools.partial(_push_kernel, n_steps=n_steps, n_p_steps=n_p_steps, n_blocks=n_blocks),
        grid_spec=grid_spec,
        out_shape=jax.ShapeDtypeStruct((n_blocks * MOE_BLOCK, D_MODEL), F32),
        compiler_params=_cparams(("arbitrary",)),
        name="moe_push",
    )(pend, counts, dest_tiles, as_groups(h2_p), as_groups(h2_s))


def _expert_kernel(be_ref, nu_ref, x_ref, wg_ref, bg_ref, wu_ref, bu_ref, wd_ref, bd_ref, o_ref,
                   wg_bf, wu_bf, wd_bf):
    i = pl.program_id(0)
    n_used = nu_ref[0]
    prev = be_ref[jnp.maximum(i - 1, 0)]
    new_expert = jnp.logical_or(i == 0, be_ref[i] != prev)

    @pl.when(jnp.logical_and(i < n_used, new_expert))
    def _():
        wg_bf[...] = wg_ref[0].astype(BF16)
        wu_bf[...] = wu_ref[0].astype(BF16)
        wd_bf[...] = wd_ref[0].astype(BF16)

    @pl.when(i < n_used)
    def _():
        x = x_ref[...].astype(BF16)
        g = jnp.dot(x, wg_bf[...], preferred_element_type=F32) + bg_ref[0]
        u = jnp.dot(x, wu_bf[...], preferred_element_type=F32) + bu_ref[0]
        g = jnp.minimum(g, SWIGLU_LIMIT)
        u = jnp.clip(u, -SWIGLU_LIMIT, SWIGLU_LIMIT)
        a = g * jax.nn.sigmoid(SWIGLU_ALPHA * g) * (u + 1.0)
        o_ref[...] = jnp.dot(a.astype(BF16), wd_bf[...], preferred_element_type=F32) + bd_ref[0]

    @pl.when(i >= n_used)
    def _():
        o_ref[...] = jnp.zeros(o_ref.shape, F32)


def _moe_expert(buf, blk_e, n_used, w_gate, b_gate, w_up, b_up, w_down, b_down):
    rows = buf.shape[0]
    n_blocks = rows // MOE_BLOCK
    d_ff = w_gate.shape[2]

    def blk(i, be, nu):
        return jnp.minimum(i, nu[0] - 1)

    xspec = pl.BlockSpec((MOE_BLOCK, D_MODEL), lambda i, be, nu: (blk(i, be, nu), 0))
    wspec = lambda a, b: pl.BlockSpec((1, a, b), lambda i, be, nu: (be[blk(i, be, nu)], 0, 0))
    grid_spec = pltpu.PrefetchScalarGridSpec(
        num_scalar_prefetch=2,
        grid=(n_blocks,),
        in_specs=[xspec, wspec(D_MODEL, d_ff), wspec(1, d_ff), wspec(D_MODEL, d_ff), wspec(1, d_ff),
                  wspec(d_ff, D_MODEL), wspec(1, D_MODEL)],
        out_specs=pl.BlockSpec((MOE_BLOCK, D_MODEL), lambda i, be, nu: (i, 0)),
        scratch_shapes=[pltpu.VMEM((D_MODEL, d_ff), BF16), pltpu.VMEM((D_MODEL, d_ff), BF16),
                        pltpu.VMEM((d_ff, D_MODEL), BF16)],
    )
    return pl.pallas_call(
        _expert_kernel,
        grid_spec=grid_spec,
        out_shape=jax.ShapeDtypeStruct((rows, D_MODEL), F32),
        compiler_params=_cparams(("arbitrary",)),
        name="moe_expert",
    )(blk_e, n_used, buf, w_gate, b_gate.reshape(N_EXPERTS, 1, d_ff), w_up, b_up.reshape(N_EXPERTS, 1, d_ff),
      w_down, b_down.reshape(N_EXPERTS, 1, D_MODEL))


def _combine_kernel(dest_hbm, out_hbm, x1_ref, gate_ref, gt_ref, y_ref,
                    idx0, idx1, rows0, rows1, isem, rsem, *, n_steps):
    i = pl.program_id(0)
    tm = x1_ref.shape[0]
    n_rows = TOP_K * tm
    idx_refs = (idx0, idx1)
    rows_refs = (rows0, rows1)

    def idx_copy(step, sl):
        return pltpu.make_async_copy(dest_hbm.at[step], idx_refs[sl], isem.at[sl])

    def row_copy(src_row, sl, g, u):
        return pltpu.make_async_copy(out_hbm.at[pl.ds(src_row, 1), :], rows_refs[sl].at[g, pl.ds(u, 1), :],
                                     rsem.at[sl])

    def issue_rows(sl):
        def body(g, carry):
            for u in range(SUBLANES):
                row_copy(idx_refs[sl][g * SUBLANES + u], sl, g, u).start(priority=u % 2)
            return carry
        lax.fori_loop(0, n_rows // SUBLANES, body, 0)

    def wait_rows(sl):
        def body(g, carry):
            for u in range(SUBLANES):
                row_copy(0, sl, g, u).wait()
            return carry
        lax.fori_loop(0, n_rows // SUBLANES, body, 0)

    @pl.when(i == 0)
    def _():
        idx_copy(0, 0).start()
        idx_copy(0, 0).wait()
        issue_rows(0)
        if n_steps > 1:
            idx_copy(1, 1).start()

    for sl in range(2):
        @pl.when((i & 1) == sl)
        def _():
            @pl.when(i + 1 < n_steps)
            def _():
                idx_copy(i + 1, 1 - sl).wait()
                issue_rows(1 - sl)

            @pl.when(i + 2 < n_steps)
            def _():
                idx_copy(i + 2, sl).start()

            wait_rows(sl)
            groups = tm // SUBLANES
            for gi in range(groups):
                tok = slice(gi * SUBLANES, (gi + 1) * SUBLANES)
                gates = gate_ref[tok, :]
                y = jnp.zeros((SUBLANES, D_MODEL), F32)
                for k in range(TOP_K):
                    y = y + gates[:, k:k + 1] * rows_refs[sl][k * groups + gi]
                gt = gt_ref[tok, :] if gt_ref.shape[0] == tm else gt_ref[...]
                y_ref[tok, :] = x1_ref[tok, :] + gt * y


def _moe_combine(out_rows, dest_tiles, x1, gates, gt):
    n = x1.shape[0]
    tm = TM_ROW
    n_steps = n // tm
    per_row = gt.shape[0] != 1
    mod_spec = pl.BlockSpec((tm, D_MODEL), lambda i: (i, 0)) if per_row else pl.BlockSpec((1, D_MODEL), lambda i: (0, 0))
    return pl.pallas_call(
        functools.partial(_combine_kernel, n_steps=n_steps),
        grid=(n_steps,),
        in_specs=[pl.BlockSpec(memory_space=pl.ANY), pl.BlockSpec(memory_space=pl.ANY),
                  pl.BlockSpec((tm, D_MODEL), lambda i: (i, 0)),
                  pl.BlockSpec((tm, LANES), lambda i: (i, 0)),
                  mod_spec],
        out_specs=pl.BlockSpec((tm, D_MODEL), lambda i: (i, 0)),
        out_shape=jax.ShapeDtypeStruct((n, D_MODEL), F32),
        scratch_shapes=[pltpu.SMEM((TOP_K * tm,), I32), pltpu.SMEM((TOP_K * tm,), I32),
                        pltpu.VMEM((TOP_K * tm // SUBLANES, SUBLANES, D_MODEL), F32),
                        pltpu.VMEM((TOP_K * tm // SUBLANES, SUBLANES, D_MODEL), F32),
                        pltpu.SemaphoreType.DMA((2,)), pltpu.SemaphoreType.DMA((2,))],
        compiler_params=_cparams(("arbitrary",)),
        name="moe_combine",
    )(dest_tiles, out_rows, x1, gates, gt)


def _dest_tiles(dest, tm):
    n = dest.shape[0]
    return dest.reshape(n // tm, tm, TOP_K).transpose(0, 2, 1).reshape(n // tm, TOP_K * tm)


def kernel(x_prompt, x_sample, cache_k, cache_v, state_conv, page_table, c_prompt, c_sample, rel_table, w_ada, b_ada, g_mix, w_in, conv_w, q_norm_w, k_norm_w, lam_q1, lam_k1, lam_q2, lam_k2, subln_w, w_out, g_ffn, w_router, b_router, w_gate, b_gate, w_up, b_up, w_down, b_down):
    assert w_ada.shape[0] == 1, "single-layer trunk"
    bp, t_p, _ = x_prompt.shape
    bs, t_s, _ = x_sample.shape
    assert bp == 1
    n_s = bs * t_s
    n_all = t_p + n_s
    l = 0

    n_c = bp + bs
    c_rows = -(-n_c // SUBLANES) * SUBLANES
    c_all = jnp.concatenate([c_prompt, c_sample, jnp.zeros((c_rows - n_c, D_MODEL), F32)], axis=0)
    mod = _adaln(c_all, w_ada[l], b_ada[l]).reshape(c_rows, 6, D_MODEL)
    mod_p = [mod[0:1, j, :] for j in range(6)]
    per_token = jnp.broadcast_to(mod[1:1 + bs, None, :, :], (bs, t_s, 6, D_MODEL)).reshape(n_s, 6, D_MODEL)
    mod_s = [per_token[:, j, :] for j in range(6)]

    w_in_bf = w_in[l].astype(BF16)
    w_out_bf = w_out[l].astype(BF16)
    lam_vecs = jnp.stack([lam_q1[l], lam_k1[l], lam_q2[l], lam_k2[l]])

    xp = x_prompt.reshape(t_p, D_MODEL)
    xs = x_sample.reshape(n_s, D_MODEL)
    yc_p, q_p, k_p, v_p, kb_p, vt_p, u_tail = _inproj(
        xp, mod_p[1], mod_p[0], g_mix[l], w_in_bf, conv_w[l], q_norm_w[l], k_norm_w[l], None, None,
        tm=TM_IN, seq_len=None)
    st = state_conv[l]
    prev2 = jnp.broadcast_to(st[:, None, 0, :], (bs, t_s, D_CONV)).reshape(n_s, D_CONV)
    prev1 = jnp.broadcast_to(st[:, None, 1, :], (bs, t_s, D_CONV)).reshape(n_s, D_CONV)
    yc_s, q_s, k_s, v_s, _, _, u_s = _inproj(
        xs, mod_s[1], mod_s[0], g_mix[l], w_in_bf, conv_w[l], q_norm_w[l], k_norm_w[l], prev2, prev1,
        tm=n_s, seq_len=t_s)

    o_p = _attn_prompt(q_p, kb_p, vt_p, rel_table, lam_vecs, subln_w[l])
    n_phys = cache_k.shape[1]
    o_s = _attn_sample(q_s.astype(F32).reshape(bs, t_s, D_QKH),
                       k_s.reshape(bs, t_s * N_HEADS, 2 * D_QK), v_s.reshape(bs, t_s * N_HEADS, D_V),
                       cache_k[l].reshape(n_phys * PAGE_ROWS, 2 * D_QK), cache_v[l].reshape(n_phys * PAGE_ROWS, D_V),
                       page_table, rel_table, lam_vecs, subln_w[l]).reshape(n_s, D_ATT)

    w_router_pad = jnp.pad(w_router[l], ((0, 0), (0, LANES - N_EXPERTS)))
    b_router_pad = jnp.concatenate([b_router[l], jnp.full((LANES - N_EXPERTS,), NEG_INF, F32)]).reshape(1, LANES)
    cnt0 = jnp.zeros((1, LANES), F32)
    x1_p, h2_p, idx_p, gate_p, rank_p, cnt1 = _outproj(
        xp, yc_p, o_p, mod_p[2], mod_p[4], mod_p[3], g_ffn[l], w_out_bf, w_router_pad, b_router_pad, cnt0, tm=TM_OUT)
    x1_s, h2_s, idx_s, gate_s, rank_s, cnt2 = _outproj(
        xs, yc_s, o_s, mod_s[2], mod_s[4], mod_s[3], g_ffn[l], w_out_bf, w_router_pad, b_router_pad, cnt1, tm=TM_OUT)

    counts = cnt2[0, :N_EXPERTS].astype(I32)
    padded = (counts + MOE_BLOCK - 1) // MOE_BLOCK * MOE_BLOCK
    pend = jnp.cumsum(padded).astype(I32)
    pstart = pend - padded
    n_blocks = (n_all * TOP_K) // MOE_BLOCK + N_EXPERTS
    dest_p = pstart[idx_p[:, :TOP_K]] + rank_p[:, :TOP_K]
    dest_s = pstart[idx_s[:, :TOP_K]] + rank_s[:, :TOP_K]
    block_start = jnp.arange(n_blocks, dtype=I32) * MOE_BLOCK
    blk_e = jnp.minimum(jnp.sum(pend[None, :] <= block_start[:, None], axis=1), N_EXPERTS - 1).astype(I32)
    n_used = (pend[-1] // MOE_BLOCK).astype(I32).reshape(1)
    tiles_p = _dest_tiles(dest_p, TM_ROW)
    tiles_s = _dest_tiles(dest_s, TM_ROW)

    buf = _moe_push(h2_p, h2_s, jnp.concatenate([tiles_p, tiles_s], axis=0), pend, counts, n_blocks=n_blocks)
    out_rows = _moe_expert(buf, blk_e, n_used, w_gate[l], b_gate[l], w_up[l], b_up[l], w_down[l], b_down[l])
    y_p = _moe_combine(out_rows, tiles_p, x1_p, gate_p, mod_p[5])
    y_s = _moe_combine(out_rows, tiles_s, x1_s, gate_s, mod_s[5])

    k_prompt = k_p.reshape(1, bp, t_p, N_HEADS, 2 * D_QK)
    v_prompt = v_p.reshape(1, bp, t_p, N_HEADS, D_V)
    conv_prompt = u_tail[SUBLANES - (CONV_W - 1):, :].reshape(1, bp, CONV_W - 1, D_CONV)
    k_sample = k_s.reshape(1, bs, t_s, N_HEADS, 2 * D_QK)
    v_sample = v_s.reshape(1, bs, t_s, N_HEADS, D_V)
    conv_sample = u_s.reshape(bs, t_s, D_CONV)[:, t_s - (CONV_W - 1):, :].reshape(1, bs, CONV_W - 1, D_CONV)
    return (y_p.reshape(bp, t_p, D_MODEL), y_s.reshape(bs, t_s, D_MODEL),
            k_prompt, v_prompt, conv_prompt, k_sample, v_sample, conv_sample)
```

```python
import functools
import math

import numpy as np
import jax
import jax.numpy as jnp
from jax import lax
from jax.experimental import pallas as pl
from jax.experimental.pallas import tpu as pltpu

F32 = jnp.float32
BF16 = jnp.bfloat16
I32 = jnp.int32

D_MODEL = 1024
D_CONV = 512
CONV_W = 3
N_HEADS = 4
D_QK = 64
D_V = 128
D_ATT = N_HEADS * D_V
D_QKH = N_HEADS * 2 * D_QK
D_IN = 3 * D_CONV + 2 * D_QKH + D_ATT
N_BUCKETS = 32
MAX_EXACT = 16
MAX_DIST = 128
N_EXPERTS = 32
TOP_K = 4
SWIGLU_LIMIT = 7.0
SWIGLU_ALPHA = 1.702
PAGE_SIZE = 128
EPS = 1e-6
NEG_INF = -1e30
LAM_INIT = 0.8 - 0.6 * math.exp(-0.3 * 0)
LOG2E = math.log2(math.e)
Q_SCALE = D_QK ** -0.5 * LOG2E
SCORE_HEADROOM = 60.0

LANES = 128
SUBLANES = 8
VMEM_LIMIT = 56 * 1024 * 1024

TM_IN = 512
TQ = 512
TK = 512
PAGES_PER_STEP = 16
TM_OUT = 256
MOE_BLOCK = 512
TM_ROW = 128
DMA_UNROLL = 4
assert TK == TM_IN


def _bucket_thresholds():
    n = np.arange(0, 4 * MAX_DIST)
    nf = np.maximum(n, 1).astype(np.float32)
    val = np.log(nf / np.float32(MAX_EXACT)) / np.float32(math.log(MAX_DIST / MAX_EXACT)) * np.float32(N_BUCKETS - MAX_EXACT)
    large = np.minimum(MAX_EXACT + val.astype(np.int32), N_BUCKETS - 1)
    bucket = np.where(n < MAX_EXACT, n, large)
    assert np.all(np.diff(bucket) >= 0) and bucket[-1] == N_BUCKETS - 1
    thr = [int(np.argmax(bucket >= b)) for b in range(N_BUCKETS)]
    return thr


BUCKET_THR = _bucket_thresholds()
FAR_DIST = BUCKET_THR[N_BUCKETS - 1]


def _log2(n):
    assert n > 0 and n & (n - 1) == 0
    return n.bit_length() - 1


def _cparams(sem):
    return pltpu.CompilerParams(dimension_semantics=sem, vmem_limit_bytes=VMEM_LIMIT)


def _adaln_kernel(c_ref, w_ref, b_ref, o_ref):
    c = c_ref[...]
    s = c * jax.nn.sigmoid(c)
    o_ref[...] = jnp.dot(s, w_ref[...], preferred_element_type=F32,
                         precision=lax.Precision.HIGHEST) + b_ref[...]


def _adaln(c_all, w_ada, b_ada):
    rows = c_all.shape[0]
    n = w_ada.shape[1]
    bn = D_MODEL
    return pl.pallas_call(
        _adaln_kernel,
        grid=(n // bn,),
        in_specs=[pl.BlockSpec((rows, D_MODEL), lambda j: (0, 0)),
                  pl.BlockSpec((D_MODEL, bn), lambda j: (0, j)),
                  pl.BlockSpec((1, bn), lambda j: (0, j))],
        out_specs=pl.BlockSpec((rows, bn), lambda j: (0, j)),
        out_shape=jax.ShapeDtypeStruct((rows, n), F32),
        compiler_params=_cparams(("arbitrary",)),
        name="adaln",
    )(c_all, w_ada, b_ada.reshape(1, n))


def _group_rms(xb, w_row):
    lane = lax.broadcasted_iota(I32, xb.shape, 1)
    lo_half = lane < D_QK
    sq = xb * xb
    lo = jnp.sum(jnp.where(lo_half, sq, 0.0), axis=1, keepdims=True)
    hi = jnp.sum(jnp.where(lo_half, 0.0, sq), axis=1, keepdims=True)
    ms = jnp.where(lo_half, lo, hi) * (1.0 / D_QK)
    return xb * lax.rsqrt(ms + EPS) * w_row


def _inproj_kernel(x_ref, sc_ref, sh_ref, g_ref, w_ref, cw_ref, qw_ref, kw_ref, p2_ref, p1_ref,
                   yc_ref, q_ref, k_ref, v_ref, kb_ref, vt_ref, u_ref, carry_ref, *, seq_len):
    tm = x_ref.shape[0]
    x = x_ref[...]
    ms = jnp.mean(x * x, axis=1, keepdims=True)
    h = x * lax.rsqrt(ms + EPS) * g_ref[...]
    h = h * (1.0 + sc_ref[...]) + sh_ref[...]
    z = jnp.dot(h.astype(BF16), w_ref[...], preferred_element_type=F32)
    gb = z[:, 0:D_CONV]
    u = z[:, D_CONV:2 * D_CONV] * z[:, 2 * D_CONV:3 * D_CONV]
    row = lax.broadcasted_iota(I32, (tm, D_CONV), 0)
    u1 = pltpu.roll(u, 1, 0)
    u2 = pltpu.roll(u, 2, 0)
    if seq_len is None:
        @pl.when(pl.program_id(0) == 0)
        def _():
            carry_ref[...] = jnp.zeros_like(carry_ref)
        prev2 = carry_ref[SUBLANES - 2:SUBLANES - 1, :]
        prev1 = carry_ref[SUBLANES - 1:SUBLANES, :]
        u1 = jnp.where(row == 0, prev1, u1)
        u2 = jnp.where(row == 0, prev2, jnp.where(row == 1, prev1, u2))
        carry_ref[...] = u[tm - SUBLANES:tm, :]
        u_ref[...] = u[tm - SUBLANES:tm, :]
    else:
        pos = row & (seq_len - 1)
        u1 = jnp.where(pos == 0, p1_ref[...], u1)
        u2 = jnp.where(pos == 0, p2_ref[...], jnp.where(pos == 1, p1_ref[...], u2))
        u_ref[...] = u
    cw = cw_ref[...]
    yc = gb * (cw[0:1, :] * u2 + cw[1:2, :] * u1 + cw[2:3, :] * u)
    yc_ref[...] = yc.astype(yc_ref.dtype)
    o1 = 3 * D_CONV
    for hh in range(N_HEADS):
        c0 = o1 + hh * LANES
        qn = _group_rms(z[:, c0:c0 + LANES], qw_ref[...])
        q_ref[:, hh * LANES:(hh + 1) * LANES] = (qn * Q_SCALE).astype(q_ref.dtype)
        c1 = o1 + D_QKH + hh * LANES
        kn = _group_rms(z[:, c1:c1 + LANES], kw_ref[...])
        k_ref[pl.ds(hh, tm, stride=N_HEADS), :] = kn
        kb_ref[:, hh * LANES:(hh + 1) * LANES] = kn.astype(BF16)
    v = z[:, o1 + 2 * D_QKH:]
    for hh in range(N_HEADS):
        vh = v[:, hh * D_V:(hh + 1) * D_V]
        v_ref[pl.ds(hh, tm, stride=N_HEADS), :] = vh
        vt_ref[0, hh * D_V:(hh + 1) * D_V, :] = vh.T.astype(BF16)


def _inproj(x, sc, sh, g_mix, w_in_bf, conv_w, qw, kw, prev2, prev1, *, tm, seq_len):
    n = x.shape[0]
    grid = (n // tm,)
    per_row = sc.shape[0] != 1
    mod_spec = pl.BlockSpec((tm, D_MODEL), lambda i: (i, 0)) if per_row else pl.BlockSpec((1, D_MODEL), lambda i: (0, 0))
    if seq_len is None:
        prev2 = jnp.zeros((SUBLANES, D_CONV), F32)
        prev1 = prev2
        prev_spec = pl.BlockSpec((SUBLANES, D_CONV), lambda i: (0, 0))
        u_shape = jax.ShapeDtypeStruct((SUBLANES, D_CONV), F32)
        u_spec = pl.BlockSpec((SUBLANES, D_CONV), lambda i: (0, 0))
    else:
        assert seq_len & (seq_len - 1) == 0
        prev_spec = pl.BlockSpec((tm, D_CONV), lambda i: (i, 0))
        u_shape = jax.ShapeDtypeStruct((n, D_CONV), F32)
        u_spec = pl.BlockSpec((tm, D_CONV), lambda i: (i, 0))
    const = lambda shape: pl.BlockSpec(shape, lambda i: (0, 0))
    rows = lambda w: pl.BlockSpec((tm, w), lambda i: (i, 0))
    head_rows = pl.BlockSpec((tm * N_HEADS, LANES), lambda i: (i, 0))
    qw2 = jnp.tile(qw.reshape(1, D_QK), (1, 2))
    kw2 = jnp.tile(kw.reshape(1, D_QK), (1, 2))
    return pl.pallas_call(
        functools.partial(_inproj_kernel, seq_len=seq_len),
        grid=grid,
        in_specs=[rows(D_MODEL), mod_spec, mod_spec, const((1, D_MODEL)), const((D_MODEL, D_IN)),
                  const((CONV_W, D_CONV)), const((1, LANES)), const((1, LANES)), prev_spec, prev_spec],
        out_specs=[rows(D_CONV), rows(D_QKH), head_rows, head_rows, rows(D_QKH),
                   pl.BlockSpec((1, D_ATT, tm), lambda i: (i, 0, 0)), u_spec],
        out_shape=[jax.ShapeDtypeStruct((n, D_CONV), BF16),
                   jax.ShapeDtypeStruct((n, D_QKH), BF16),
                   jax.ShapeDtypeStruct((n * N_HEADS, 2 * D_QK), F32),
                   jax.ShapeDtypeStruct((n * N_HEADS, D_V), F32),
                   jax.ShapeDtypeStruct((n, D_QKH), BF16),
                   jax.ShapeDtypeStruct((n // tm, D_ATT, tm), BF16),
                   u_shape],
        scratch_shapes=[pltpu.VMEM((SUBLANES, D_CONV), F32)],
        compiler_params=_cparams(("arbitrary",)),
        name="inproj_seq" if seq_len is None else "inproj_batch",
    )(x, sc, sh, g_mix.reshape(1, D_MODEL), w_in_bf, conv_w, qw2, kw2, prev2, prev1)


def _bias_from_dist(dist, table_of_bucket):
    b = jnp.zeros(dist.shape, F32) + table_of_bucket(0)
    for bk in range(1, N_BUCKETS):
        b = jnp.where(dist >= BUCKET_THR[bk], table_of_bucket(bk), b)
    return (b - table_of_bucket(N_BUCKETS - 1)) * LOG2E


def _lambda_value(lam_ref):
    lq1, lk1, lq2, lk2 = (lam_ref[i:i + 1, :] for i in range(4))
    return (jnp.exp(jnp.sum(lq1 * lk1, axis=1, keepdims=True))
            - jnp.exp(jnp.sum(lq2 * lk2, axis=1, keepdims=True)) + LAM_INIT)


def _sub_norm(o, sw_row):
    ms = jnp.mean(o * o, axis=1, keepdims=True)
    return o * lax.rsqrt(ms + EPS) * sw_row * (1.0 - LAM_INIT)


def _attn_prompt_kernel(tbl_ref, q_ref, k_ref, vt_ref, lam_ref, sw_ref, o_ref,
                        bdiag_ref, bprev_ref, m_ref, l_ref, acc_ref):
    h = pl.program_id(0)
    i = pl.program_id(1)
    tq = q_ref.shape[0]
    tk = vt_ref.shape[2]
    assert tq == tk and tk >= FAR_DIST

    @pl.when(i == 0)
    def _():
        key = lax.broadcasted_iota(I32, (tk, tq), 0)
        qry = lax.broadcasted_iota(I32, (tk, tq), 1)
        d0 = qry - key
        tb = lambda b: tbl_ref[b, h]
        bdiag_ref[...] = jnp.where(d0 >= 0, _bias_from_dist(jnp.maximum(d0, 0), tb), NEG_INF)
        bprev_ref[...] = _bias_from_dist(d0 + tk, tb)

    m_ref[...] = jnp.full(m_ref.shape, -jnp.inf, F32)
    l_ref[...] = jnp.zeros(l_ref.shape, F32)
    acc_ref[...] = jnp.zeros(acc_ref.shape, F32)

    q = q_ref[...]
    lane = lax.broadcasted_iota(I32, q.shape, 1)
    zero = jnp.zeros_like(q)
    q_maps = (jnp.where(lane < D_QK, q, zero), jnp.where(lane < D_QK, zero, q))

    def scores_t(kb, mi):
        return lax.dot_general(kb, q_maps[mi], (((1,), (1,)), ((), ())), preferred_element_type=F32)

    def key_block(j):
        return k_ref[pl.ds(pl.multiple_of(j * tk, tk), tk), :], vt_ref[j]

    def exact_step(j, bias_ref):
        kb, vtb = key_block(j)
        for mi in range(2):
            st = scores_t(kb, mi)
            if bias_ref is not None:
                st = st + bias_ref[...]
            m_prev = m_ref[mi]
            m_new = jnp.maximum(m_prev, jnp.max(st, axis=0, keepdims=True))
            alpha = jnp.exp2(m_prev - m_new)
            pt = jnp.exp2(st - m_new)
            l_ref[mi] = alpha * l_ref[mi] + jnp.sum(pt, axis=0, keepdims=True)
            acc_ref[mi] = alpha * acc_ref[mi] + jnp.dot(vtb, pt.astype(BF16), preferred_element_type=F32)
            m_ref[mi] = m_new

    def streamed_step(j, n_blk, bias_ref=None):
        kb = k_ref[pl.ds(pl.multiple_of(j * tk, tk), n_blk * tk), :]
        sts = [scores_t(kb, mi) for mi in range(2)]
        if bias_ref is not None:
            assert n_blk == 1
            sts = [st + bias_ref[...] for st in sts]
        parts = []
        worst = None
        for mi in range(2):
            ref_pt = m_ref[mi]
            pt = jnp.exp2(sts[mi] - ref_pt)
            blk_max = jnp.max(sts[mi], axis=0, keepdims=True)
            l_add = jnp.sum(pt, axis=0, keepdims=True)
            acc_add = functools.reduce(lambda a, b: a + b, [
                jnp.dot(vt_ref[j + u], pt[u * tk:(u + 1) * tk, :].astype(BF16), preferred_element_type=F32)
                for u in range(n_blk)])
            parts.append((ref_pt, blk_max, l_add, acc_add))
            excess = blk_max - ref_pt
            worst = excess if worst is None else jnp.maximum(worst, excess)
        in_range = jnp.max(worst) <= SCORE_HEADROOM

        @pl.when(in_range)
        def _():
            for mi, (ref_pt, blk_max, l_add, acc_add) in enumerate(parts):
                m_new = jnp.maximum(ref_pt, blk_max)
                alpha = jnp.exp2(ref_pt - m_new)
                l_ref[mi] = alpha * (l_ref[mi] + l_add)
                acc_ref[mi] = alpha * (acc_ref[mi] + acc_add)
                m_ref[mi] = m_new

        @pl.when(jnp.logical_not(in_range))
        def _():
            for u in range(n_blk):
                exact_step(j + u, bias_ref)

    exact_step(i, bdiag_ref)

    @pl.when(i > 0)
    def _():
        streamed_step(i - 1, 1, bprev_ref)

    n_far = jnp.maximum(i - 1, 0)

    def far_body(jj, carry):
        streamed_step(2 * jj, 2)
        return carry

    lax.fori_loop(0, n_far >> 1, far_body, 0)

    @pl.when((n_far & 1) == 1)
    def _():
        streamed_step(n_far - 1, 1)

    lam = _lambda_value(lam_ref)
    ot = acc_ref[0] / l_ref[0] - lam * (acc_ref[1] / l_ref[1])
    o_ref[...] = _sub_norm(ot.T, sw_ref[...]).astype(o_ref.dtype)


def _attn_prompt(q_bf, k_bf, vt_bf, rel_table, lam_vecs, subln_w):
    t = q_bf.shape[0]
    nk, _, tk = vt_bf.shape
    assert tk == TK and nk * tk == t
    nq = t // TQ
    grid_spec = pltpu.PrefetchScalarGridSpec(
        num_scalar_prefetch=0,
        grid=(N_HEADS, nq),
        in_specs=[pl.BlockSpec(memory_space=pltpu.SMEM),
                  pl.BlockSpec((TQ, LANES), lambda h, i: (i, h)),
                  pl.BlockSpec((t, LANES), lambda h, i: (0, h)),
                  pl.BlockSpec((nk, D_V, tk), lambda h, i: (0, h, 0)),
                  pl.BlockSpec((4, D_QK), lambda h, i: (0, 0)),
                  pl.BlockSpec((1, D_V), lambda h, i: (0, 0))],
        out_specs=pl.BlockSpec((TQ, D_V), lambda h, i: (i, h)),
        scratch_shapes=[pltpu.VMEM((TK, TQ), F32), pltpu.VMEM((TK, TQ), F32),
                        pltpu.VMEM((2, 1, TQ), F32), pltpu.VMEM((2, 1, TQ), F32),
                        pltpu.VMEM((2, D_V, TQ), F32)],
    )
    return pl.pallas_call(
        _attn_prompt_kernel,
        grid_spec=grid_spec,
        out_shape=jax.ShapeDtypeStruct((t, D_ATT), BF16),
        compiler_params=_cparams(("arbitrary", "arbitrary")),
        name="attn_prompt",
    )(rel_table, q_bf, k_bf, vt_bf, lam_vecs, subln_w.reshape(1, D_V))


PAGE_ROWS = PAGE_SIZE * N_HEADS


def _attn_sample_kernel(pt_ref, q_ref, kn_ref, vn_ref, tblr_ref, lam_ref, sw_ref, *rest, n_groups, dec_seq):
    pg = PAGES_PER_STEP
    k_refs = rest[:pg]
    v_refs = rest[pg:2 * pg]
    o_ref = rest[2 * pg]
    qf_ref, qb_ref, knp_ref, vnp_ref, bmask_ref, blast_ref, bnew_ref, m_ref, l_ref, acc_ref = rest[2 * pg + 1:]
    b = pl.program_id(0)
    g = pl.program_id(1)
    q_rows = N_HEADS * 2 * dec_seq
    new_rows = dec_seq * N_HEADS
    head_shift = _log2(2 * dec_seq)
    key_shift = _log2(N_HEADS)
    assert PAGE_SIZE + 1 >= FAR_DIST and new_rows <= LANES

    @pl.when(jnp.logical_and(b == 0, g == 0))
    def _():
        tb = lambda bk: tblr_ref[:, bk:bk + 1]
        r = lax.broadcasted_iota(I32, (q_rows, PAGE_ROWS), 0)
        c = lax.broadcasted_iota(I32, (q_rows, PAGE_ROWS), 1)
        same_head = (c & (N_HEADS - 1)) == (r >> head_shift)
        tok = r & (dec_seq - 1)
        key = c >> key_shift
        bmask_ref[...] = jnp.where(same_head, 0.0, NEG_INF)
        blast_ref[...] = jnp.where(same_head, _bias_from_dist(tok + PAGE_SIZE - key, tb), NEG_INF)
        r2 = lax.broadcasted_iota(I32, (q_rows, LANES), 0)
        c2 = lax.broadcasted_iota(I32, (q_rows, LANES), 1)
        d2 = (r2 & (dec_seq - 1)) - (c2 >> key_shift)
        ok2 = jnp.logical_and((c2 & (N_HEADS - 1)) == (r2 >> head_shift),
                              jnp.logical_and(d2 >= 0, c2 < new_rows))
        bnew_ref[...] = jnp.where(ok2, _bias_from_dist(jnp.maximum(d2, 0), tb), NEG_INF)

    @pl.when(g == 0)
    def _():
        q = q_ref[0]
        lane = lax.broadcasted_iota(I32, (dec_seq, LANES), 1)
        for hh in range(N_HEADS):
            qh = q[:, hh * LANES:(hh + 1) * LANES]
            r0 = hh * 2 * dec_seq
            qf_ref[r0:r0 + dec_seq, :] = jnp.where(lane < D_QK, qh, 0.0)
            qf_ref[r0 + dec_seq:r0 + 2 * dec_seq, :] = jnp.where(lane < D_QK, 0.0, qh)
        qb_ref[...] = qf_ref[...].astype(BF16)
        knp_ref[...] = jnp.zeros(knp_ref.shape, F32)
        vnp_ref[...] = jnp.zeros(vnp_ref.shape, F32)
        knp_ref[0:new_rows, :] = kn_ref[0]
        vnp_ref[0:new_rows, :] = vn_ref[0]
        m_ref[...] = jnp.full(m_ref.shape, -jnp.inf, F32)
        l_ref[...] = jnp.zeros(l_ref.shape, F32)
        acc_ref[...] = jnp.zeros(acc_ref.shape, F32)

    qb = qb_ref[...]
    is_last = g == n_groups - 1

    def scores(k_rows):
        return lax.dot_general(qb, k_rows.astype(BF16), (((1,), (1,)), ((), ())), preferred_element_type=F32)

    def update(s_list, v_list):
        m_prev = m_ref[...]
        m_cur = functools.reduce(jnp.maximum, [jnp.max(s, axis=1, keepdims=True) for s in s_list])
        m_new = jnp.maximum(m_prev, m_cur)
        alpha = jnp.exp2(m_prev - m_new)
        l_new = alpha * l_ref[...]
        pv = None
        for s, v_rows in zip(s_list, v_list):
            p = jnp.exp2(s - jnp.concatenate([m_new] * (s.shape[1] // LANES), axis=1))
            l_new = l_new + jnp.sum(p, axis=1, keepdims=True)
            d = jnp.dot(p.astype(BF16), v_rows.astype(BF16), preferred_element_type=F32)
            pv = d if pv is None else pv + d
        acc_ref[...] = alpha * acc_ref[...] + pv
        l_ref[...] = l_new
        m_ref[...] = m_new

    s_list = [scores(k_refs[p][...]) + bmask_ref[...] for p in range(pg - 1)]
    s_list.append(scores(k_refs[pg - 1][...]) + jnp.where(is_last, blast_ref[...], bmask_ref[...]))
    update(s_list, [v_refs[p][...] for p in range(pg)])

    @pl.when(is_last)
    def _():
        update([scores(knp_ref[...]) + bnew_ref[...]], [vnp_ref[...]])
        lam = _lambda_value(lam_ref)
        for hh in range(N_HEADS):
            r0 = hh * 2 * dec_seq
            o1 = acc_ref[r0:r0 + dec_seq, :] / l_ref[r0:r0 + dec_seq, :]
            o2 = acc_ref[r0 + dec_seq:r0 + 2 * dec_seq, :] / l_ref[r0 + dec_seq:r0 + 2 * dec_seq, :]
            o_ref[0, :, hh * D_V:(hh + 1) * D_V] = _sub_norm(o1 - lam * o2, sw_ref[...])


def _attn_sample(q_s, kn_rows, vn_rows, cache_k_rows, cache_v_rows, page_table, rel_table, lam_vecs, subln_w):
    bsz, dec_seq, _ = q_s.shape
    n_pages = page_table.shape[1]
    pg = PAGES_PER_STEP
    assert n_pages % pg == 0
    n_groups = n_pages // pg
    q_rows = N_HEADS * 2 * dec_seq
    new_rows = dec_seq * N_HEADS
    tbl_rows = jnp.repeat(rel_table.T, 2 * dec_seq, axis=0)
    tbl_rows = jnp.pad(tbl_rows, ((0, 0), (0, LANES - N_BUCKETS)))
    per_b = lambda shape: pl.BlockSpec(shape, lambda b, g, pt: (b, 0, 0))
    const2 = lambda shape: pl.BlockSpec(shape, lambda b, g, pt: (0, 0))

    def page_spec(p):
        return pl.BlockSpec((PAGE_ROWS, LANES), lambda b, g, pt: (pt[b, g * pg + p], 0))

    grid_spec = pltpu.PrefetchScalarGridSpec(
        num_scalar_prefetch=1,
        grid=(bsz, n_groups),
        in_specs=[per_b((1, dec_seq, D_QKH)), per_b((1, new_rows, LANES)), per_b((1, new_rows, LANES)),
                  const2((q_rows, LANES)), const2((4, D_QK)), const2((1, D_V))]
                 + [page_spec(p) for p in range(pg)] + [page_spec(p) for p in range(pg)],
        out_specs=per_b((1, dec_seq, D_ATT)),
        scratch_shapes=[pltpu.VMEM((q_rows, LANES), F32), pltpu.VMEM((q_rows, LANES), BF16),
                        pltpu.VMEM((LANES, LANES), F32), pltpu.VMEM((LANES, D_V), F32),
                        pltpu.VMEM((q_rows, PAGE_ROWS), F32), pltpu.VMEM((q_rows, PAGE_ROWS), F32),
                        pltpu.VMEM((q_rows, LANES), F32),
                        pltpu.VMEM((q_rows, LANES), F32), pltpu.VMEM((q_rows, LANES), F32),
                        pltpu.VMEM((q_rows, D_V), F32)],
    )
    return pl.pallas_call(
        functools.partial(_attn_sample_kernel, n_groups=n_groups, dec_seq=dec_seq),
        grid_spec=grid_spec,
        out_shape=jax.ShapeDtypeStruct((bsz, dec_seq, D_ATT), F32),
        compiler_params=_cparams(("arbitrary", "arbitrary")),
        name="attn_sample",
    )(page_table, q_s, kn_rows, vn_rows, tbl_rows, lam_vecs, subln_w.reshape(1, D_V),
      *([cache_k_rows] * pg), *([cache_v_rows] * pg))


def _split_bf16(a):
    hi = a.astype(BF16)
    lo = (a - hi.astype(F32)).astype(BF16)
    return hi, lo


def _outproj_kernel(x_ref, yc_ref, o_ref, gt_ref, sc_ref, sh_ref, g_ref, wo_ref, wr_ref, br_ref, cin_ref,
                    x1_ref, h2_ref, idx_ref, gate_ref, rank_ref, cnt_ref):
    tm = x_ref.shape[0]

    @pl.when(pl.program_id(0) == 0)
    def _():
        cnt_ref[...] = cin_ref[...]

    mix = (jnp.dot(yc_ref[...].astype(BF16), wo_ref[0:D_CONV, :], preferred_element_type=F32)
           + jnp.dot(o_ref[...].astype(BF16), wo_ref[D_CONV:, :], preferred_element_type=F32))
    x1 = x_ref[...] + gt_ref[...] * mix
    x1_ref[...] = x1
    ms = jnp.mean(x1 * x1, axis=1, keepdims=True)
    h2 = x1 * lax.rsqrt(ms + EPS) * g_ref[...]
    h2 = h2 * (1.0 + sc_ref[...]) + sh_ref[...]
    h2_ref[...] = h2
    h_hi, h_lo = _split_bf16(h2)
    w_hi, w_lo = _split_bf16(wr_ref[...])
    logits = (jnp.dot(h_hi, w_hi, preferred_element_type=F32)
              + jnp.dot(h_hi, w_lo, preferred_element_type=F32)
              + jnp.dot(h_lo, w_hi, preferred_element_type=F32)) + br_ref[...]
    lane = lax.broadcasted_iota(I32, (tm, LANES), 1)
    lane_f = lane.astype(F32)
    vals, ids = [], []
    cur = logits
    for _ in range(TOP_K):
        mx = jnp.max(cur, axis=1, keepdims=True)
        ik = jnp.min(jnp.where(cur == mx, lane_f, float(LANES)), axis=1, keepdims=True)
        vals.append(mx)
        ids.append(ik)
        cur = jnp.where(lane_f == ik, -jnp.inf, cur)
    es = [jnp.exp(v - vals[0]) for v in vals]
    denom = functools.reduce(lambda a, c: a + c, es)
    sel = jnp.zeros((tm, LANES), F32)
    idx_out = jnp.zeros((tm, LANES), F32)
    gate_out = jnp.zeros((tm, LANES), F32)
    for k in range(TOP_K):
        sel = sel + jnp.where(lane_f == ids[k], 1.0, 0.0)
        idx_out = jnp.where(lane == k, ids[k], idx_out)
        gate_out = jnp.where(lane == k, es[k] / denom, gate_out)
    r = lax.broadcasted_iota(I32, (tm, tm), 0)
    c = lax.broadcasted_iota(I32, (tm, tm), 1)
    lower = jnp.where(r > c, 1.0, 0.0).astype(BF16)
    before = jnp.dot(lower, sel.astype(BF16), preferred_element_type=F32) + cnt_ref[...]
    rank_out = jnp.zeros((tm, LANES), F32)
    for k in range(TOP_K):
        rk = jnp.sum(jnp.where(lane_f == ids[k], before, 0.0), axis=1, keepdims=True)
        rank_out = jnp.where(lane == k, rk, rank_out)
    cnt_ref[...] = cnt_ref[...] + jnp.sum(sel, axis=0, keepdims=True)
    idx_ref[...] = idx_out.astype(I32)
    gate_ref[...] = gate_out
    rank_ref[...] = rank_out.astype(I32)


def _outproj(x, yc, o, gt, sc, sh, g_ffn, w_out_bf, w_router_pad, b_router_pad, cnt_in, *, tm):
    n = x.shape[0]
    per_row = gt.shape[0] != 1
    mod_spec = pl.BlockSpec((tm, D_MODEL), lambda i: (i, 0)) if per_row else pl.BlockSpec((1, D_MODEL), lambda i: (0, 0))
    const = lambda shape: pl.BlockSpec(shape, lambda i: (0, 0))
    rows = lambda w: pl.BlockSpec((tm, w), lambda i: (i, 0))
    return pl.pallas_call(
        _outproj_kernel,
        grid=(n // tm,),
        in_specs=[rows(D_MODEL), rows(D_CONV), rows(D_ATT), mod_spec, mod_spec, mod_spec, const((1, D_MODEL)),
                  const((D_MODEL, D_MODEL)), const((D_MODEL, LANES)), const((1, LANES)), const((1, LANES))],
        out_specs=[rows(D_MODEL), rows(D_MODEL), rows(LANES), rows(LANES), rows(LANES), const((1, LANES))],
        out_shape=[jax.ShapeDtypeStruct((n, D_MODEL), F32),
                   jax.ShapeDtypeStruct((n, D_MODEL), F32),
                   jax.ShapeDtypeStruct((n, LANES), I32),
                   jax.ShapeDtypeStruct((n, LANES), F32),
                   jax.ShapeDtypeStruct((n, LANES), I32),
                   jax.ShapeDtypeStruct((1, LANES), F32)],
        compiler_params=_cparams(("arbitrary",)),
        name="outproj",
    )(x, yc, o, gt, sc, sh, g_ffn.reshape(1, D_MODEL), w_out_bf, w_router_pad, b_router_pad, cnt_in)


def _push_kernel(pend_ref, cnt_ref, dest_hbm, hp_ref, hs_ref, buf_out, idx0, idx1, zero_ref, isem, rsem, zsem,
                 *, n_steps, n_p_steps, n_blocks):
    i = pl.program_id(0)
    tm = hp_ref.shape[0] * SUBLANES
    idx_refs = (idx0, idx1)

    @pl.when(i == 0)
    def _():
        zero_ref[...] = jnp.zeros(zero_ref.shape, F32)
        blk_shift = _log2(MOE_BLOCK)
        n_used = pend_ref[N_EXPERTS - 1] >> blk_shift

        def zero_copy(block):
            start = pl.multiple_of(block * MOE_BLOCK, MOE_BLOCK)
            return pltpu.make_async_copy(zero_ref, buf_out.at[pl.ds(start, MOE_BLOCK), :], zsem)

        def targets(e):
            return ((cnt_ref[e] > 0, (pend_ref[e] >> blk_shift) - 1), (n_used + e < n_blocks, n_used + e))

        for wait in (False, True):
            for e in range(N_EXPERTS):
                for cond, block in targets(e):
                    @pl.when(cond)
                    def _():
                        if wait:
                            zero_copy(block).wait()
                        else:
                            zero_copy(block).start()

    def idx_copy(step, sl):
        return pltpu.make_async_copy(dest_hbm.at[step], idx_refs[sl], isem.at[sl])

    @pl.when(i == 0)
    def _():
        idx_copy(0, 0).start()

    def push_tile(h_ref, sl):
        idx_copy(i, sl).wait()

        @pl.when(i + 1 < n_steps)
        def _():
            idx_copy(i + 1, 1 - sl).start()

        def row_copy(g, u, dst_row):
            return pltpu.make_async_copy(h_ref.at[g, pl.ds(u, 1), :], buf_out.at[pl.ds(dst_row, 1), :], rsem)

        def start_body(g, carry):
            for u in range(SUBLANES):
                for k in range(TOP_K):
                    row_copy(g, u, idx_refs[sl][k * tm + g * SUBLANES + u]).start(priority=k % 2)
            return carry

        lax.fori_loop(0, tm // SUBLANES, start_body, 0)

        def wait_body(g, carry):
            for u in range(SUBLANES):
                for k in range(TOP_K):
                    row_copy(g, u, 0).wait()
            return carry

        lax.fori_loop(0, tm // SUBLANES, wait_body, 0)

    for sl in range(2):
        @pl.when(jnp.logical_and(i < n_p_steps, (i & 1) == sl))
        def _():
            push_tile(hp_ref, sl)

        @pl.when(jnp.logical_and(i >= n_p_steps, (i & 1) == sl))
        def _():
            push_tile(hs_ref, sl)


def _moe_push(h2_p, h2_s, dest_tiles, pend, counts, *, n_blocks):
    tm = TM_ROW
    n_p_steps = h2_p.shape[0] // tm
    n_steps = n_p_steps + h2_s.shape[0] // tm
    assert dest_tiles.shape[0] == n_steps
    groups = tm // SUBLANES
    any_spec = pl.BlockSpec(memory_space=pl.ANY)
    grid_spec = pltpu.PrefetchScalarGridSpec(
        num_scalar_prefetch=2,
        grid=(n_steps,),
        in_specs=[any_spec,
                  pl.BlockSpec((groups, SUBLANES, D_MODEL), lambda i, pe, cn: (jnp.minimum(i, n_p_steps - 1), 0, 0)),
                  pl.BlockSpec((groups, SUBLANES, D_MODEL), lambda i, pe, cn: (jnp.maximum(i - n_p_steps, 0), 0, 0))],
        out_specs=any_spec,
        scratch_shapes=[pltpu.SMEM((TOP_K * tm,), I32), pltpu.SMEM((TOP_K * tm,), I32),
                        pltpu.VMEM((MOE_BLOCK, D_MODEL), F32),
                        pltpu.SemaphoreType.DMA((2,)), pltpu.SemaphoreType.DMA(()), pltpu.SemaphoreType.DMA(())],
    )
    as_groups = lambda h: h.reshape(h.shape[0] // SUBLANES, SUBLANES, D_MODEL)
    return pl.pallas_call(
        functools.partial(_push_kernel, n_steps=n_steps, n_p_steps=n_p_steps, n_blocks=n_blocks),
        grid_spec=grid_spec,
        out_shape=jax.ShapeDtypeStruct((n_blocks * MOE_BLOCK, D_MODEL), F32),
        compiler_params=_cparams(("arbitrary",)),
        name="moe_push",
    )(pend, counts, dest_tiles, as_groups(h2_p), as_groups(h2_s))


def _expert_kernel(be_ref, nu_ref, x_ref, wg_ref, bg_ref, wu_ref, bu_ref, wd_ref, bd_ref, o_ref,
                   wg_bf, wu_bf, wd_bf):
    i = pl.program_id(0)
    n_used = nu_ref[0]
    prev = be_ref[jnp.maximum(i - 1, 0)]
    new_expert = jnp.logical_or(i == 0, be_ref[i] != prev)

    @pl.when(jnp.logical_and(i < n_used, new_expert))
    def _():
        wg_bf[...] = wg_ref[0].astype(BF16)
        wu_bf[...] = wu_ref[0].astype(BF16)
        wd_bf[...] = wd_ref[0].astype(BF16)

    @pl.when(i < n_used)
    def _():
        x = x_ref[...].astype(BF16)
        g = jnp.dot(x, wg_bf[...], preferred_element_type=F32) + bg_ref[0]
        u = jnp.dot(x, wu_bf[...], preferred_element_type=F32) + bu_ref[0]
        g = jnp.minimum(g, SWIGLU_LIMIT)
        u = jnp.clip(u, -SWIGLU_LIMIT, SWIGLU_LIMIT)
        a = g * jax.nn.sigmoid(SWIGLU_ALPHA * g) * (u + 1.0)
        o_ref[...] = jnp.dot(a.astype(BF16), wd_bf[...], preferred_element_type=F32) + bd_ref[0]

    @pl.when(i >= n_used)
    def _():
        o_ref[...] = jnp.zeros(o_ref.shape, F32)


def _moe_expert(buf, blk_e, n_used, w_gate, b_gate, w_up, b_up, w_down, b_down):
    rows = buf.shape[0]
    n_blocks = rows // MOE_BLOCK
    d_ff = w_gate.shape[2]

    def blk(i, be, nu):
        return jnp.minimum(i, nu[0] - 1)

    xspec = pl.BlockSpec((MOE_BLOCK, D_MODEL), lambda i, be, nu: (blk(i, be, nu), 0))
    wspec = lambda a, b: pl.BlockSpec((1, a, b), lambda i, be, nu: (be[blk(i, be, nu)], 0, 0))
    grid_spec = pltpu.PrefetchScalarGridSpec(
        num_scalar_prefetch=2,
        grid=(n_blocks,),
        in_specs=[xspec, wspec(D_MODEL, d_ff), wspec(1, d_ff), wspec(D_MODEL, d_ff), wspec(1, d_ff),
                  wspec(d_ff, D_MODEL), wspec(1, D_MODEL)],
        out_specs=pl.BlockSpec((MOE_BLOCK, D_MODEL), lambda i, be, nu: (i, 0)),
        scratch_shapes=[pltpu.VMEM((D_MODEL, d_ff), BF16), pltpu.VMEM((D_MODEL, d_ff), BF16),
                        pltpu.VMEM((d_ff, D_MODEL), BF16)],
    )
    return pl.pallas_call(
        _expert_kernel,
        grid_spec=grid_spec,
        out_shape=jax.ShapeDtypeStruct((rows, D_MODEL), F32),
        compiler_params=_cparams(("arbitrary",)),
        name="moe_expert",
    )(blk_e, n_used, buf, w_gate, b_gate.reshape(N_EXPERTS, 1, d_ff), w_up, b_up.reshape(N_EXPERTS, 1, d_ff),
      w_down, b_down.reshape(N_EXPERTS, 1, D_MODEL))


def _combine_kernel(dest_hbm, out_hbm, x1_ref, gate_ref, gt_ref, y_ref,
                    idx0, idx1, rows0, rows1, isem, rsem, *, n_steps):
    i = pl.program_id(0)
    tm = x1_ref.shape[0]
    n_rows = TOP_K * tm
    idx_refs = (idx0, idx1)
    rows_refs = (rows0, rows1)

    def idx_copy(step, sl):
        return pltpu.make_async_copy(dest_hbm.at[step], idx_refs[sl], isem.at[sl])

    def row_copy(src_row, sl, g, u):
        return pltpu.make_async_copy(out_hbm.at[pl.ds(src_row, 1), :], rows_refs[sl].at[g, pl.ds(u, 1), :],
                                     rsem.at[sl])

    def issue_rows(sl):
        def body(g, carry):
            for u in range(SUBLANES):
                row_copy(idx_refs[sl][g * SUBLANES + u], sl, g, u).start(priority=u % 2)
            return carry
        lax.fori_loop(0, n_rows // SUBLANES, body, 0)

    def wait_rows(sl):
        def body(g, carry):
            for u in range(SUBLANES):
                row_copy(0, sl, g, u).wait()
            return carry
        lax.fori_loop(0, n_rows // SUBLANES, body, 0)

    @pl.when(i == 0)
    def _():
        idx_copy(0, 0).start()
        idx_copy(0, 0).wait()
        issue_rows(0)
        if n_steps > 1:
            idx_copy(1, 1).start()

    for sl in range(2):
        @pl.when((i & 1) == sl)
        def _():
            @pl.when(i + 1 < n_steps)
            def _():
                idx_copy(i + 1, 1 - sl).wait()
                issue_rows(1 - sl)

            @pl.when(i + 2 < n_steps)
            def _():
                idx_copy(i + 2, sl).start()

            wait_rows(sl)
            groups = tm // SUBLANES
            for gi in range(groups):
                tok = slice(gi * SUBLANES, (gi + 1) * SUBLANES)
                gates = gate_ref[tok, :]
                y = jnp.zeros((SUBLANES, D_MODEL), F32)
                for k in range(TOP_K):
                    y = y + gates[:, k:k + 1] * rows_refs[sl][k * groups + gi]
                gt = gt_ref[tok, :] if gt_ref.shape[0] == tm else gt_ref[...]
                y_ref[tok, :] = x1_ref[tok, :] + gt * y


def _moe_combine(out_rows, dest_tiles, x1, gates, gt):
    n = x1.shape[0]
    tm = TM_ROW
    n_steps = n // tm
    per_row = gt.shape[0] != 1
    mod_spec = pl.BlockSpec((tm, D_MODEL), lambda i: (i, 0)) if per_row else pl.BlockSpec((1, D_MODEL), lambda i: (0, 0))
    return pl.pallas_call(
        functools.partial(_combine_kernel, n_steps=n_steps),
        grid=(n_steps,),
        in_specs=[pl.BlockSpec(memory_space=pl.ANY), pl.BlockSpec(memory_space=pl.ANY),
                  pl.BlockSpec((tm, D_MODEL), lambda i: (i, 0)),
                  pl.BlockSpec((tm, LANES), lambda i: (i, 0)),
                  mod_spec],
        out_specs=pl.BlockSpec((tm, D_MODEL), lambda i: (i, 0)),
        out_shape=jax.ShapeDtypeStruct((n, D_MODEL), F32),
        scratch_shapes=[pltpu.SMEM((TOP_K * tm,), I32), pltpu.SMEM((TOP_K * tm,), I32),
                        pltpu.VMEM((TOP_K * tm // SUBLANES, SUBLANES, D_MODEL), F32),
                        pltpu.VMEM((TOP_K * tm // SUBLANES, SUBLANES, D_MODEL), F32),
                        pltpu.SemaphoreType.DMA((2,)), pltpu.SemaphoreType.DMA((2,))],
        compiler_params=_cparams(("arbitrary",)),
        name="moe_combine",
    )(dest_tiles, out_rows, x1, gates, gt)


def _dest_tiles(dest, tm):
    n = dest.shape[0]
    return dest.reshape(n // tm, tm, TOP_K).transpose(0, 2, 1).reshape(n // tm, TOP_K * tm)


def kernel(x_prompt, x_sample, cache_k, cache_v, state_conv, page_table, c_prompt, c_sample, rel_table, w_ada, b_ada, g_mix, w_in, conv_w, q_norm_w, k_norm_w, lam_q1, lam_k1, lam_q2, lam_k2, subln_w, w_out, g_ffn, w_router, b_router, w_gate, b_gate, w_up, b_up, w_down, b_down):
    assert w_ada.shape[0] == 1, "single-layer trunk"
    bp, t_p, _ = x_prompt.shape
    bs, t_s, _ = x_sample.shape
    assert bp == 1
    n_s = bs * t_s
    n_all = t_p + n_s
    l = 0

    n_c = bp + bs
    c_rows = -(-n_c // SUBLANES) * SUBLANES
    c_all = jnp.concatenate([c_prompt, c_sample, jnp.zeros((c_rows - n_c, D_MODEL), F32)], axis=0)
    mod = _adaln(c_all, w_ada[l], b_ada[l]).reshape(c_rows, 6, D_MODEL)
    mod_p = [mod[0:1, j, :] for j in range(6)]
    per_token = jnp.broadcast_to(mod[1:1 + bs, None, :, :], (bs, t_s, 6, D_MODEL)).reshape(n_s, 6, D_MODEL)
    mod_s = [per_token[:, j, :] for j in range(6)]

    w_in_bf = w_in[l].astype(BF16)
    w_out_bf = w_out[l].astype(BF16)
    lam_vecs = jnp.stack([lam_q1[l], lam_k1[l], lam_q2[l], lam_k2[l]])

    xp = x_prompt.reshape(t_p, D_MODEL)
    xs = x_sample.reshape(n_s, D_MODEL)
    yc_p, q_p, k_p, v_p, kb_p, vt_p, u_tail = _inproj(
        xp, mod_p[1], mod_p[0], g_mix[l], w_in_bf, conv_w[l], q_norm_w[l], k_norm_w[l], None, None,
        tm=TM_IN, seq_len=None)
    st = state_conv[l]
    prev2 = jnp.broadcast_to(st[:, None, 0, :], (bs, t_s, D_CONV)).reshape(n_s, D_CONV)
    prev1 = jnp.broadcast_to(st[:, None, 1, :], (bs, t_s, D_CONV)).reshape(n_s, D_CONV)
    yc_s, q_s, k_s, v_s, _, _, u_s = _inproj(
        xs, mod_s[1], mod_s[0], g_mix[l], w_in_bf, conv_w[l], q_norm_w[l], k_norm_w[l], prev2, prev1,
        tm=n_s, seq_len=t_s)

    o_p = _attn_prompt(q_p, kb_p, vt_p, rel_table, lam_vecs, subln_w[l])
    n_phys = cache_k.shape[1]
    o_s = _attn_sample(q_s.astype(F32).reshape(bs, t_s, D_QKH),
                       k_s.reshape(bs, t_s * N_HEADS, 2 * D_QK), v_s.reshape(bs, t_s * N_HEADS, D_V),
                       cache_k[l].reshape(n_phys * PAGE_ROWS, 2 * D_QK), cache_v[l].reshape(n_phys * PAGE_ROWS, D_V),
                       page_table, rel_table, lam_vecs, subln_w[l]).reshape(n_s, D_ATT)

    w_router_pad = jnp.pad(w_router[l], ((0, 0), (0, LANES - N_EXPERTS)))
    b_router_pad = jnp.concatenate([b_router[l], jnp.full((LANES - N_EXPERTS,), NEG_INF, F32)]).reshape(1, LANES)
    cnt0 = jnp.zeros((1, LANES), F32)
    x1_p, h2_p, idx_p, gate_p, rank_p, cnt1 = _outproj(
        xp, yc_p, o_p, mod_p[2], mod_p[4], mod_p[3], g_ffn[l], w_out_bf, w_router_pad, b_router_pad, cnt0, tm=TM_OUT)
    x1_s, h2_s, idx_s, gate_s, rank_s, cnt2 = _outproj(
        xs, yc_s, o_s, mod_s[2], mod_s[4], mod_s[3], g_ffn[l], w_out_bf, w_router_pad, b_router_pad, cnt1, tm=TM_OUT)

    counts = cnt2[0, :N_EXPERTS].astype(I32)
    padded = (counts + MOE_BLOCK - 1) // MOE_BLOCK * MOE_BLOCK
    pend = jnp.cumsum(padded).astype(I32)
    pstart = pend - padded
    n_blocks = (n_all * TOP_K) // MOE_BLOCK + N_EXPERTS
    def slots(idx, rank):
        is_e = idx[:, :TOP_K, None] == jnp.arange(N_EXPERTS, dtype=I32)
        return jnp.sum(jnp.where(is_e, pstart, 0), axis=-1) + rank[:, :TOP_K]

    dest_p = slots(idx_p, rank_p)
    dest_s = slots(idx_s, rank_s)
    block_start = jnp.arange(n_blocks, dtype=I32) * MOE_BLOCK
    blk_e = jnp.minimum(jnp.sum(pend[None, :] <= block_start[:, None], axis=1), N_EXPERTS - 1).astype(I32)
    n_used = (pend[-1] // MOE_BLOCK).astype(I32).reshape(1)
    tiles_p = _dest_tiles(dest_p, TM_ROW)
    tiles_s = _dest_tiles(dest_s, TM_ROW)

    buf = _moe_push(h2_p, h2_s, jnp.concatenate([tiles_p, tiles_s], axis=0), pend, counts, n_blocks=n_blocks)
    out_rows = _moe_expert(buf, blk_e, n_used, w_gate[l], b_gate[l], w_up[l], b_up[l], w_down[l], b_down[l])
    y_p = _moe_combine(out_rows, tiles_p, x1_p, gate_p, mod_p[5])
    y_s = _moe_combine(out_rows, tiles_s, x1_s, gate_s, mod_s[5])

    k_prompt = k_p.reshape(1, bp, t_p, N_HEADS, 2 * D_QK)
    v_prompt = v_p.reshape(1, bp, t_p, N_HEADS, D_V)
    conv_prompt = u_tail[SUBLANES - (CONV_W - 1):, :].reshape(1, bp, CONV_W - 1, D_CONV)
    k_sample = k_s.reshape(1, bs, t_s, N_HEADS, 2 * D_QK)
    v_sample = v_s.reshape(1, bs, t_s, N_HEADS, D_V)
    conv_sample = u_s.reshape(bs, t_s, D_CONV)[:, t_s - (CONV_W - 1):, :].reshape(1, bs, CONV_W - 1, D_CONV)
    return (y_p.reshape(bp, t_p, D_MODEL), y_s.reshape(bs, t_s, D_MODEL),
            k_prompt, v_prompt, conv_prompt, k_sample, v_sample, conv_sample)
```

```python
import functools
import math

import numpy as np
import jax
import jax.numpy as jnp
from jax import lax
from jax.experimental import pallas as pl
from jax.experimental.pallas import tpu as pltpu

F32 = jnp.float32
BF16 = jnp.bfloat16
I32 = jnp.int32

D_MODEL = 1024
D_CONV = 512
CONV_W = 3
N_HEADS = 4
D_QK = 64
D_V = 128
D_ATT = N_HEADS * D_V
D_QKH = N_HEADS * 2 * D_QK
D_IN = 3 * D_CONV + 2 * D_QKH + D_ATT
N_BUCKETS = 32
MAX_EXACT = 16
MAX_DIST = 128
N_EXPERTS = 32
TOP_K = 4
SWIGLU_LIMIT = 7.0
SWIGLU_ALPHA = 1.702
PAGE_SIZE = 128
EPS = 1e-6
NEG_INF = -1e30
LAM_INIT = 0.8 - 0.6 * math.exp(-0.3 * 0)
LOG2E = math.log2(math.e)
Q_SCALE = D_QK ** -0.5 * LOG2E
SCORE_HEADROOM = 60.0

LANES = 128
SUBLANES = 8
VMEM_LIMIT = 56 * 1024 * 1024

TM_IN = 512
TQ = 512
TK = 512
PAGES_PER_STEP = 16
TM_OUT = 256
MOE_BLOCK = 512
TM_ROW = 128
DMA_UNROLL = 4
assert TK == TM_IN


def _bucket_thresholds():
    n = np.arange(0, 4 * MAX_DIST)
    nf = np.maximum(n, 1).astype(np.float32)
    val = np.log(nf / np.float32(MAX_EXACT)) / np.float32(math.log(MAX_DIST / MAX_EXACT)) * np.float32(N_BUCKETS - MAX_EXACT)
    large = np.minimum(MAX_EXACT + val.astype(np.int32), N_BUCKETS - 1)
    bucket = np.where(n < MAX_EXACT, n, large)
    assert np.all(np.diff(bucket) >= 0) and bucket[-1] == N_BUCKETS - 1
    thr = [int(np.argmax(bucket >= b)) for b in range(N_BUCKETS)]
    return thr


BUCKET_THR = _bucket_thresholds()
FAR_DIST = BUCKET_THR[N_BUCKETS - 1]


def _log2(n):
    assert n > 0 and n & (n - 1) == 0
    return n.bit_length() - 1


def _cparams(sem):
    return pltpu.CompilerParams(dimension_semantics=sem, vmem_limit_bytes=VMEM_LIMIT)


def _adaln_kernel(c_ref, w_ref, b_ref, o_ref):
    c = c_ref[...]
    s = c * jax.nn.sigmoid(c)
    o_ref[...] = jnp.dot(s, w_ref[...], preferred_element_type=F32,
                         precision=lax.Precision.HIGHEST) + b_ref[...]


def _adaln(c_all, w_ada, b_ada):
    rows = c_all.shape[0]
    n = w_ada.shape[1]
    bn = D_MODEL
    return pl.pallas_call(
        _adaln_kernel,
        grid=(n // bn,),
        in_specs=[pl.BlockSpec((rows, D_MODEL), lambda j: (0, 0)),
                  pl.BlockSpec((D_MODEL, bn), lambda j: (0, j)),
                  pl.BlockSpec((1, bn), lambda j: (0, j))],
        out_specs=pl.BlockSpec((rows, bn), lambda j: (0, j)),
        out_shape=jax.ShapeDtypeStruct((rows, n), F32),
        compiler_params=_cparams(("arbitrary",)),
        name="adaln",
    )(c_all, w_ada, b_ada.reshape(1, n))


def _group_rms(xb, w_row):
    lane = lax.broadcasted_iota(I32, xb.shape, 1)
    lo_half = lane < D_QK
    sq = xb * xb
    lo = jnp.sum(jnp.where(lo_half, sq, 0.0), axis=1, keepdims=True)
    hi = jnp.sum(jnp.where(lo_half, 0.0, sq), axis=1, keepdims=True)
    ms = jnp.where(lo_half, lo, hi) * (1.0 / D_QK)
    return xb * lax.rsqrt(ms + EPS) * w_row


def _inproj_kernel(x_ref, sc_ref, sh_ref, g_ref, w_ref, cw_ref, qw_ref, kw_ref, p2_ref, p1_ref,
                   yc_ref, q_ref, k_ref, v_ref, kb_ref, vt_ref, u_ref, carry_ref, *, seq_len):
    tm = x_ref.shape[0]
    x = x_ref[...]
    ms = jnp.mean(x * x, axis=1, keepdims=True)
    h = x * lax.rsqrt(ms + EPS) * g_ref[...]
    h = h * (1.0 + sc_ref[...]) + sh_ref[...]
    z = jnp.dot(h.astype(BF16), w_ref[...], preferred_element_type=F32)
    gb = z[:, 0:D_CONV]
    u = z[:, D_CONV:2 * D_CONV] * z[:, 2 * D_CONV:3 * D_CONV]
    row = lax.broadcasted_iota(I32, (tm, D_CONV), 0)
    u1 = pltpu.roll(u, 1, 0)
    u2 = pltpu.roll(u, 2, 0)
    if seq_len is None:
        @pl.when(pl.program_id(0) == 0)
        def _():
            carry_ref[...] = jnp.zeros_like(carry_ref)
        prev2 = carry_ref[SUBLANES - 2:SUBLANES - 1, :]
        prev1 = carry_ref[SUBLANES - 1:SUBLANES, :]
        u1 = jnp.where(row == 0, prev1, u1)
        u2 = jnp.where(row == 0, prev2, jnp.where(row == 1, prev1, u2))
        carry_ref[...] = u[tm - SUBLANES:tm, :]
        u_ref[...] = u[tm - SUBLANES:tm, :]
    else:
        pos = row & (seq_len - 1)
        u1 = jnp.where(pos == 0, p1_ref[...], u1)
        u2 = jnp.where(pos == 0, p2_ref[...], jnp.where(pos == 1, p1_ref[...], u2))
        u_ref[...] = u
    cw = cw_ref[...]
    yc = gb * (cw[0:1, :] * u2 + cw[1:2, :] * u1 + cw[2:3, :] * u)
    yc_ref[...] = yc.astype(yc_ref.dtype)
    o1 = 3 * D_CONV
    for hh in range(N_HEADS):
        c0 = o1 + hh * LANES
        qn = _group_rms(z[:, c0:c0 + LANES], qw_ref[...])
        q_ref[:, hh * LANES:(hh + 1) * LANES] = (qn * Q_SCALE).astype(q_ref.dtype)
        c1 = o1 + D_QKH + hh * LANES
        kn = _group_rms(z[:, c1:c1 + LANES], kw_ref[...])
        k_ref[pl.ds(hh, tm, stride=N_HEADS), :] = kn
        kb_ref[:, hh * LANES:(hh + 1) * LANES] = kn.astype(BF16)
    v = z[:, o1 + 2 * D_QKH:]
    for hh in range(N_HEADS):
        vh = v[:, hh * D_V:(hh + 1) * D_V]
        v_ref[pl.ds(hh, tm, stride=N_HEADS), :] = vh
        vt_ref[0, hh * D_V:(hh + 1) * D_V, :] = vh.T.astype(BF16)


def _inproj(x, sc, sh, g_mix, w_in_bf, conv_w, qw, kw, prev2, prev1, *, tm, seq_len):
    n = x.shape[0]
    grid = (n // tm,)
    per_row = sc.shape[0] != 1
    mod_spec = pl.BlockSpec((tm, D_MODEL), lambda i: (i, 0)) if per_row else pl.BlockSpec((1, D_MODEL), lambda i: (0, 0))
    if seq_len is None:
        prev2 = jnp.zeros((SUBLANES, D_CONV), F32)
        prev1 = prev2
        prev_spec = pl.BlockSpec((SUBLANES, D_CONV), lambda i: (0, 0))
        u_shape = jax.ShapeDtypeStruct((SUBLANES, D_CONV), F32)
        u_spec = pl.BlockSpec((SUBLANES, D_CONV), lambda i: (0, 0))
    else:
        assert seq_len & (seq_len - 1) == 0
        prev_spec = pl.BlockSpec((tm, D_CONV), lambda i: (i, 0))
        u_shape = jax.ShapeDtypeStruct((n, D_CONV), F32)
        u_spec = pl.BlockSpec((tm, D_CONV), lambda i: (i, 0))
    const = lambda shape: pl.BlockSpec(shape, lambda i: (0, 0))
    rows = lambda w: pl.BlockSpec((tm, w), lambda i: (i, 0))
    head_rows = pl.BlockSpec((tm * N_HEADS, LANES), lambda i: (i, 0))
    qw2 = jnp.tile(qw.reshape(1, D_QK), (1, 2))
    kw2 = jnp.tile(kw.reshape(1, D_QK), (1, 2))
    return pl.pallas_call(
        functools.partial(_inproj_kernel, seq_len=seq_len),
        grid=grid,
        in_specs=[rows(D_MODEL), mod_spec, mod_spec, const((1, D_MODEL)), const((D_MODEL, D_IN)),
                  const((CONV_W, D_CONV)), const((1, LANES)), const((1, LANES)), prev_spec, prev_spec],
        out_specs=[rows(D_CONV), rows(D_QKH), head_rows, head_rows, rows(D_QKH),
                   pl.BlockSpec((1, D_ATT, tm), lambda i: (i, 0, 0)), u_spec],
        out_shape=[jax.ShapeDtypeStruct((n, D_CONV), BF16),
                   jax.ShapeDtypeStruct((n, D_QKH), BF16),
                   jax.ShapeDtypeStruct((n * N_HEADS, 2 * D_QK), F32),
                   jax.ShapeDtypeStruct((n * N_HEADS, D_V), F32),
                   jax.ShapeDtypeStruct((n, D_QKH), BF16),
                   jax.ShapeDtypeStruct((n // tm, D_ATT, tm), BF16),
                   u_shape],
        scratch_shapes=[pltpu.VMEM((SUBLANES, D_CONV), F32)],
        compiler_params=_cparams(("arbitrary",)),
        name="inproj_seq" if seq_len is None else "inproj_batch",
    )(x, sc, sh, g_mix.reshape(1, D_MODEL), w_in_bf, conv_w, qw2, kw2, prev2, prev1)


def _bias_from_dist(dist, table_of_bucket):
    b = jnp.zeros(dist.shape, F32) + table_of_bucket(0)
    for bk in range(1, N_BUCKETS):
        b = jnp.where(dist >= BUCKET_THR[bk], table_of_bucket(bk), b)
    return (b - table_of_bucket(N_BUCKETS - 1)) * LOG2E


def _lambda_value(lam_ref):
    lq1, lk1, lq2, lk2 = (lam_ref[i:i + 1, :] for i in range(4))
    return (jnp.exp(jnp.sum(lq1 * lk1, axis=1, keepdims=True))
            - jnp.exp(jnp.sum(lq2 * lk2, axis=1, keepdims=True)) + LAM_INIT)


def _sub_norm(o, sw_row):
    ms = jnp.mean(o * o, axis=1, keepdims=True)
    return o * lax.rsqrt(ms + EPS) * sw_row * (1.0 - LAM_INIT)


def _attn_prompt_kernel(tbl_ref, q_ref, k_ref, vt_ref, lam_ref, sw_ref, o_ref,
                        bdiag_ref, bprev_ref, m_ref, l_ref, acc_ref):
    h = pl.program_id(0)
    i = pl.program_id(1)
    tq = q_ref.shape[0]
    tk = vt_ref.shape[2]
    assert tq == tk and tk >= FAR_DIST

    @pl.when(i == 0)
    def _():
        key = lax.broadcasted_iota(I32, (tk, tq), 0)
        qry = lax.broadcasted_iota(I32, (tk, tq), 1)
        d0 = qry - key
        tb = lambda b: tbl_ref[b, h]
        bdiag_ref[...] = jnp.where(d0 >= 0, _bias_from_dist(jnp.maximum(d0, 0), tb), NEG_INF)
        bprev_ref[...] = _bias_from_dist(d0 + tk, tb)

    m_ref[...] = jnp.full(m_ref.shape, -jnp.inf, F32)
    l_ref[...] = jnp.zeros(l_ref.shape, F32)
    acc_ref[...] = jnp.zeros(acc_ref.shape, F32)

    q = q_ref[...]
    lane = lax.broadcasted_iota(I32, q.shape, 1)
    zero = jnp.zeros_like(q)
    q_maps = (jnp.where(lane < D_QK, q, zero), jnp.where(lane < D_QK, zero, q))

    def scores_t(kb, mi):
        return lax.dot_general(kb, q_maps[mi], (((1,), (1,)), ((), ())), preferred_element_type=F32)

    def key_block(j):
        return k_ref[pl.ds(pl.multiple_of(j * tk, tk), tk), :], vt_ref[j]

    def exact_step(j, bias_ref):
        kb, vtb = key_block(j)
        for mi in range(2):
            st = scores_t(kb, mi)
            if bias_ref is not None:
                st = st + bias_ref[...]
            m_prev = m_ref[mi]
            m_new = jnp.maximum(m_prev, jnp.max(st, axis=0, keepdims=True))
            alpha = jnp.exp2(m_prev - m_new)
            pt = jnp.exp2(st - m_new)
            l_ref[mi] = alpha * l_ref[mi] + jnp.sum(pt, axis=0, keepdims=True)
            acc_ref[mi] = alpha * acc_ref[mi] + jnp.dot(vtb, pt.astype(BF16), preferred_element_type=F32)
            m_ref[mi] = m_new

    def streamed_step(j, n_blk, bias_ref=None):
        kb = k_ref[pl.ds(pl.multiple_of(j * tk, tk), n_blk * tk), :]
        sts = [scores_t(kb, mi) for mi in range(2)]
        if bias_ref is not None:
            assert n_blk == 1
            sts = [st + bias_ref[...] for st in sts]
        parts = []
        worst = None
        for mi in range(2):
            ref_pt = m_ref[mi]
            pt = jnp.exp2(sts[mi] - ref_pt)
            blk_max = jnp.max(sts[mi], axis=0, keepdims=True)
            l_add = jnp.sum(pt, axis=0, keepdims=True)
            acc_add = functools.reduce(lambda a, b: a + b, [
                jnp.dot(vt_ref[j + u], pt[u * tk:(u + 1) * tk, :].astype(BF16), preferred_element_type=F32)
                for u in range(n_blk)])
            parts.append((ref_pt, blk_max, l_add, acc_add))
            excess = blk_max - ref_pt
            worst = excess if worst is None else jnp.maximum(worst, excess)
        in_range = jnp.max(worst) <= SCORE_HEADROOM

        @pl.when(in_range)
        def _():
            for mi, (ref_pt, blk_max, l_add, acc_add) in enumerate(parts):
                m_new = jnp.maximum(ref_pt, blk_max)
                alpha = jnp.exp2(ref_pt - m_new)
                l_ref[mi] = alpha * (l_ref[mi] + l_add)
                acc_ref[mi] = alpha * (acc_ref[mi] + acc_add)
                m_ref[mi] = m_new

        @pl.when(jnp.logical_not(in_range))
        def _():
            for u in range(n_blk):
                exact_step(j + u, bias_ref)

    exact_step(i, bdiag_ref)

    @pl.when(i > 0)
    def _():
        streamed_step(i - 1, 1, bprev_ref)

    n_far = jnp.maximum(i - 1, 0)

    def far_body(jj, carry):
        streamed_step(2 * jj, 2)
        return carry

    lax.fori_loop(0, n_far >> 1, far_body, 0)

    @pl.when((n_far & 1) == 1)
    def _():
        streamed_step(n_far - 1, 1)

    lam = _lambda_value(lam_ref)
    ot = acc_ref[0] / l_ref[0] - lam * (acc_ref[1] / l_ref[1])
    o_ref[...] = _sub_norm(ot.T, sw_ref[...]).astype(o_ref.dtype)


def _attn_prompt(q_bf, k_bf, vt_bf, rel_table, lam_vecs, subln_w):
    t = q_bf.shape[0]
    nk, _, tk = vt_bf.shape
    assert tk == TK and nk * tk == t
    nq = t // TQ
    grid_spec = pltpu.PrefetchScalarGridSpec(
        num_scalar_prefetch=0,
        grid=(N_HEADS, nq),
        in_specs=[pl.BlockSpec(memory_space=pltpu.SMEM),
                  pl.BlockSpec((TQ, LANES), lambda h, i: (i, h)),
                  pl.BlockSpec((t, LANES), lambda h, i: (0, h)),
                  pl.BlockSpec((nk, D_V, tk), lambda h, i: (0, h, 0)),
                  pl.BlockSpec((4, D_QK), lambda h, i: (0, 0)),
                  pl.BlockSpec((1, D_V), lambda h, i: (0, 0))],
        out_specs=pl.BlockSpec((TQ, D_V), lambda h, i: (i, h)),
        scratch_shapes=[pltpu.VMEM((TK, TQ), F32), pltpu.VMEM((TK, TQ), F32),
                        pltpu.VMEM((2, 1, TQ), F32), pltpu.VMEM((2, 1, TQ), F32),
                        pltpu.VMEM((2, D_V, TQ), F32)],
    )
    return pl.pallas_call(
        _attn_prompt_kernel,
        grid_spec=grid_spec,
        out_shape=jax.ShapeDtypeStruct((t, D_ATT), BF16),
        compiler_params=_cparams(("arbitrary", "arbitrary")),
        name="attn_prompt",
    )(rel_table, q_bf, k_bf, vt_bf, lam_vecs, subln_w.reshape(1, D_V))


PAGE_ROWS = PAGE_SIZE * N_HEADS


def _attn_sample_kernel(pt_ref, q_ref, kn_ref, vn_ref, tblr_ref, lam_ref, sw_ref, *rest, n_groups, dec_seq):
    pg = PAGES_PER_STEP
    k_refs = rest[:pg]
    v_refs = rest[pg:2 * pg]
    o_ref = rest[2 * pg]
    qf_ref, qb_ref, knp_ref, vnp_ref, bmask_ref, blast_ref, bnew_ref, m_ref, l_ref, acc_ref = rest[2 * pg + 1:]
    b = pl.program_id(0)
    g = pl.program_id(1)
    q_rows = N_HEADS * 2 * dec_seq
    new_rows = dec_seq * N_HEADS
    head_shift = _log2(2 * dec_seq)
    key_shift = _log2(N_HEADS)
    assert PAGE_SIZE + 1 >= FAR_DIST and new_rows <= LANES

    @pl.when(jnp.logical_and(b == 0, g == 0))
    def _():
        tb = lambda bk: tblr_ref[:, bk:bk + 1]
        r = lax.broadcasted_iota(I32, (q_rows, PAGE_ROWS), 0)
        c = lax.broadcasted_iota(I32, (q_rows, PAGE_ROWS), 1)
        same_head = (c & (N_HEADS - 1)) == (r >> head_shift)
        tok = r & (dec_seq - 1)
        key = c >> key_shift
        bmask_ref[...] = jnp.where(same_head, 0.0, NEG_INF)
        blast_ref[...] = jnp.where(same_head, _bias_from_dist(tok + PAGE_SIZE - key, tb), NEG_INF)
        r2 = lax.broadcasted_iota(I32, (q_rows, LANES), 0)
        c2 = lax.broadcasted_iota(I32, (q_rows, LANES), 1)
        d2 = (r2 & (dec_seq - 1)) - (c2 >> key_shift)
        ok2 = jnp.logical_and((c2 & (N_HEADS - 1)) == (r2 >> head_shift),
                              jnp.logical_and(d2 >= 0, c2 < new_rows))
        bnew_ref[...] = jnp.where(ok2, _bias_from_dist(jnp.maximum(d2, 0), tb), NEG_INF)

    @pl.when(g == 0)
    def _():
        q = q_ref[0]
        lane = lax.broadcasted_iota(I32, (dec_seq, LANES), 1)
        for hh in range(N_HEADS):
            qh = q[:, hh * LANES:(hh + 1) * LANES]
            r0 = hh * 2 * dec_seq
            qf_ref[r0:r0 + dec_seq, :] = jnp.where(lane < D_QK, qh, 0.0)
            qf_ref[r0 + dec_seq:r0 + 2 * dec_seq, :] = jnp.where(lane < D_QK, 0.0, qh)
        qb_ref[...] = qf_ref[...].astype(BF16)
        knp_ref[...] = jnp.zeros(knp_ref.shape, F32)
        vnp_ref[...] = jnp.zeros(vnp_ref.shape, F32)
        knp_ref[0:new_rows, :] = kn_ref[0]
        vnp_ref[0:new_rows, :] = vn_ref[0]
        m_ref[...] = jnp.full(m_ref.shape, -jnp.inf, F32)
        l_ref[...] = jnp.zeros(l_ref.shape, F32)
        acc_ref[...] = jnp.zeros(acc_ref.shape, F32)

    qb = qb_ref[...]
    is_last = g == n_groups - 1

    def scores(k_rows):
        return lax.dot_general(qb, k_rows.astype(BF16), (((1,), (1,)), ((), ())), preferred_element_type=F32)

    def update(s_list, v_list):
        m_prev = m_ref[...]
        m_cur = functools.reduce(jnp.maximum, [jnp.max(s, axis=1, keepdims=True) for s in s_list])
        m_new = jnp.maximum(m_prev, m_cur)
        alpha = jnp.exp2(m_prev - m_new)
        l_new = alpha * l_ref[...]
        pv = None
        for s, v_rows in zip(s_list, v_list):
            p = jnp.exp2(s - jnp.concatenate([m_new] * (s.shape[1] // LANES), axis=1))
            l_new = l_new + jnp.sum(p, axis=1, keepdims=True)
            d = jnp.dot(p.astype(BF16), v_rows.astype(BF16), preferred_element_type=F32)
            pv = d if pv is None else pv + d
        acc_ref[...] = alpha * acc_ref[...] + pv
        l_ref[...] = l_new
        m_ref[...] = m_new

    s_list = [scores(k_refs[p][...]) + bmask_ref[...] for p in range(pg - 1)]
    s_list.append(scores(k_refs[pg - 1][...]) + jnp.where(is_last, blast_ref[...], bmask_ref[...]))
    update(s_list, [v_refs[p][...] for p in range(pg)])

    @pl.when(is_last)
    def _():
        update([scores(knp_ref[...]) + bnew_ref[...]], [vnp_ref[...]])
        lam = _lambda_value(lam_ref)
        for hh in range(N_HEADS):
            r0 = hh * 2 * dec_seq
            o1 = acc_ref[r0:r0 + dec_seq, :] / l_ref[r0:r0 + dec_seq, :]
            o2 = acc_ref[r0 + dec_seq:r0 + 2 * dec_seq, :] / l_ref[r0 + dec_seq:r0 + 2 * dec_seq, :]
            o_ref[0, :, hh * D_V:(hh + 1) * D_V] = _sub_norm(o1 - lam * o2, sw_ref[...])


def _attn_sample(q_s, kn_rows, vn_rows, cache_k_rows, cache_v_rows, page_table, rel_table, lam_vecs, subln_w):
    bsz, dec_seq, _ = q_s.shape
    n_pages = page_table.shape[1]
    pg = PAGES_PER_STEP
    assert n_pages % pg == 0
    n_groups = n_pages // pg
    q_rows = N_HEADS * 2 * dec_seq
    new_rows = dec_seq * N_HEADS
    tbl_rows = jnp.repeat(rel_table.T, 2 * dec_seq, axis=0)
    tbl_rows = jnp.pad(tbl_rows, ((0, 0), (0, LANES - N_BUCKETS)))
    per_b = lambda shape: pl.BlockSpec(shape, lambda b, g, pt: (b, 0, 0))
    const2 = lambda shape: pl.BlockSpec(shape, lambda b, g, pt: (0, 0))

    def page_spec(p):
        return pl.BlockSpec((PAGE_ROWS, LANES), lambda b, g, pt: (pt[b, g * pg + p], 0))

    grid_spec = pltpu.PrefetchScalarGridSpec(
        num_scalar_prefetch=1,
        grid=(bsz, n_groups),
        in_specs=[per_b((1, dec_seq, D_QKH)), per_b((1, new_rows, LANES)), per_b((1, new_rows, LANES)),
                  const2((q_rows, LANES)), const2((4, D_QK)), const2((1, D_V))]
                 + [page_spec(p) for p in range(pg)] + [page_spec(p) for p in range(pg)],
        out_specs=per_b((1, dec_seq, D_ATT)),
        scratch_shapes=[pltpu.VMEM((q_rows, LANES), F32), pltpu.VMEM((q_rows, LANES), BF16),
                        pltpu.VMEM((LANES, LANES), F32), pltpu.VMEM((LANES, D_V), F32),
                        pltpu.VMEM((q_rows, PAGE_ROWS), F32), pltpu.VMEM((q_rows, PAGE_ROWS), F32),
                        pltpu.VMEM((q_rows, LANES), F32),
                        pltpu.VMEM((q_rows, LANES), F32), pltpu.VMEM((q_rows, LANES), F32),
                        pltpu.VMEM((q_rows, D_V), F32)],
    )
    return pl.pallas_call(
        functools.partial(_attn_sample_kernel, n_groups=n_groups, dec_seq=dec_seq),
        grid_spec=grid_spec,
        out_shape=jax.ShapeDtypeStruct((bsz, dec_seq, D_ATT), F32),
        compiler_params=_cparams(("arbitrary", "arbitrary")),
        name="attn_sample",
    )(page_table, q_s, kn_rows, vn_rows, tbl_rows, lam_vecs, subln_w.reshape(1, D_V),
      *([cache_k_rows] * pg), *([cache_v_rows] * pg))


ROW_CHUNKS = D_MODEL // LANES
assert ROW_CHUNKS == SUBLANES


def _store_row_tiles(ref, mat):
    rows = mat.shape[0]
    for c in range(ROW_CHUNKS):
        ref[pl.ds(c, rows, stride=ROW_CHUNKS), :] = mat[:, c * LANES:(c + 1) * LANES]


def _load_row_tiles(ref, first_row, rows, dtype=F32):
    return jnp.concatenate(
        [ref[pl.ds(first_row * ROW_CHUNKS + c, rows, stride=ROW_CHUNKS), :].astype(dtype) for c in range(ROW_CHUNKS)],
        axis=1)


def _split_bf16(a):
    hi = a.astype(BF16)
    lo = (a - hi.astype(F32)).astype(BF16)
    return hi, lo


def _outproj_kernel(x_ref, yc_ref, o_ref, gt_ref, sc_ref, sh_ref, g_ref, wo_ref, wr_ref, br_ref, cin_ref,
                    x1_ref, h2_ref, idx_ref, gate_ref, rank_ref, cnt_ref):
    tm = x_ref.shape[0]

    @pl.when(pl.program_id(0) == 0)
    def _():
        cnt_ref[...] = cin_ref[...]

    mix = (jnp.dot(yc_ref[...].astype(BF16), wo_ref[0:D_CONV, :], preferred_element_type=F32)
           + jnp.dot(o_ref[...].astype(BF16), wo_ref[D_CONV:, :], preferred_element_type=F32))
    x1 = x_ref[...] + gt_ref[...] * mix
    x1_ref[...] = x1
    ms = jnp.mean(x1 * x1, axis=1, keepdims=True)
    h2 = x1 * lax.rsqrt(ms + EPS) * g_ref[...]
    h2 = h2 * (1.0 + sc_ref[...]) + sh_ref[...]
    _store_row_tiles(h2_ref, h2)
    h_hi, h_lo = _split_bf16(h2)
    w_hi, w_lo = _split_bf16(wr_ref[...])
    logits = (jnp.dot(h_hi, w_hi, preferred_element_type=F32)
              + jnp.dot(h_hi, w_lo, preferred_element_type=F32)
              + jnp.dot(h_lo, w_hi, preferred_element_type=F32)) + br_ref[...]
    lane = lax.broadcasted_iota(I32, (tm, LANES), 1)
    lane_f = lane.astype(F32)
    vals, ids = [], []
    cur = logits
    for _ in range(TOP_K):
        mx = jnp.max(cur, axis=1, keepdims=True)
        ik = jnp.min(jnp.where(cur == mx, lane_f, float(LANES)), axis=1, keepdims=True)
        vals.append(mx)
        ids.append(ik)
        cur = jnp.where(lane_f == ik, -jnp.inf, cur)
    es = [jnp.exp(v - vals[0]) for v in vals]
    denom = functools.reduce(lambda a, c: a + c, es)
    sel = jnp.zeros((tm, LANES), F32)
    idx_out = jnp.zeros((tm, LANES), F32)
    gate_out = jnp.zeros((tm, LANES), F32)
    for k in range(TOP_K):
        sel = sel + jnp.where(lane_f == ids[k], 1.0, 0.0)
        idx_out = jnp.where(lane == k, ids[k], idx_out)
        gate_out = jnp.where(lane == k, es[k] / denom, gate_out)
    r = lax.broadcasted_iota(I32, (tm, tm), 0)
    c = lax.broadcasted_iota(I32, (tm, tm), 1)
    lower = jnp.where(r > c, 1.0, 0.0).astype(BF16)
    before = jnp.dot(lower, sel.astype(BF16), preferred_element_type=F32) + cnt_ref[...]
    rank_out = jnp.zeros((tm, LANES), F32)
    for k in range(TOP_K):
        rk = jnp.sum(jnp.where(lane_f == ids[k], before, 0.0), axis=1, keepdims=True)
        rank_out = jnp.where(lane == k, rk, rank_out)
    cnt_ref[...] = cnt_ref[...] + jnp.sum(sel, axis=0, keepdims=True)
    idx_ref[...] = idx_out.astype(I32)
    gate_ref[...] = gate_out
    rank_ref[...] = rank_out.astype(I32)


def _outproj(x, yc, o, gt, sc, sh, g_ffn, w_out_bf, w_router_pad, b_router_pad, cnt_in, *, tm):
    n = x.shape[0]
    per_row = gt.shape[0] != 1
    mod_spec = pl.BlockSpec((tm, D_MODEL), lambda i: (i, 0)) if per_row else pl.BlockSpec((1, D_MODEL), lambda i: (0, 0))
    const = lambda shape: pl.BlockSpec(shape, lambda i: (0, 0))
    rows = lambda w: pl.BlockSpec((tm, w), lambda i: (i, 0))
    return pl.pallas_call(
        _outproj_kernel,
        grid=(n // tm,),
        in_specs=[rows(D_MODEL), rows(D_CONV), rows(D_ATT), mod_spec, mod_spec, mod_spec, const((1, D_MODEL)),
                  const((D_MODEL, D_MODEL)), const((D_MODEL, LANES)), const((1, LANES)), const((1, LANES))],
        out_specs=[rows(D_MODEL), pl.BlockSpec((tm * ROW_CHUNKS, LANES), lambda i: (i, 0)),
                   rows(LANES), rows(LANES), rows(LANES), const((1, LANES))],
        out_shape=[jax.ShapeDtypeStruct((n, D_MODEL), F32),
                   jax.ShapeDtypeStruct((n * ROW_CHUNKS, LANES), F32),
                   jax.ShapeDtypeStruct((n, LANES), I32),
                   jax.ShapeDtypeStruct((n, LANES), F32),
                   jax.ShapeDtypeStruct((n, LANES), I32),
                   jax.ShapeDtypeStruct((1, LANES), F32)],
        compiler_params=_cparams(("arbitrary",)),
        name="outproj",
    )(x, yc, o, gt, sc, sh, g_ffn.reshape(1, D_MODEL), w_out_bf, w_router_pad, b_router_pad, cnt_in)


def _push_kernel(pend_ref, cnt_ref, dest_hbm, hp_ref, hs_ref, buf_out, idx0, idx1, zero_ref, isem, rsem, zsem,
                 *, n_steps, n_p_steps, n_blocks):
    i = pl.program_id(0)
    tm = hp_ref.shape[0]
    idx_refs = (idx0, idx1)

    @pl.when(i == 0)
    def _():
        zero_ref[...] = jnp.zeros(zero_ref.shape, F32)
        blk_shift = _log2(MOE_BLOCK)
        n_used = pend_ref[N_EXPERTS - 1] >> blk_shift

        def zero_copy(block):
            start = pl.multiple_of(block * MOE_BLOCK, MOE_BLOCK)
            return pltpu.make_async_copy(zero_ref, buf_out.at[pl.ds(start, MOE_BLOCK)], zsem)

        def targets(e):
            return ((cnt_ref[e] > 0, (pend_ref[e] >> blk_shift) - 1), (n_used + e < n_blocks, n_used + e))

        for wait in (False, True):
            for e in range(N_EXPERTS):
                for cond, block in targets(e):
                    @pl.when(cond)
                    def _():
                        if wait:
                            zero_copy(block).wait()
                        else:
                            zero_copy(block).start()

    def idx_copy(step, sl):
        return pltpu.make_async_copy(dest_hbm.at[step], idx_refs[sl], isem.at[sl])

    @pl.when(i == 0)
    def _():
        idx_copy(0, 0).start()

    def push_tile(h_ref, sl):
        idx_copy(i, sl).wait()

        @pl.when(i + 1 < n_steps)
        def _():
            idx_copy(i + 1, 1 - sl).start()

        def row_copy(t, dst_row):
            return pltpu.make_async_copy(h_ref.at[t], buf_out.at[dst_row], rsem)

        def start_body(t, carry):
            for k in range(TOP_K):
                row_copy(t, idx_refs[sl][k * tm + t]).start(priority=k % 2)
            return carry

        lax.fori_loop(0, tm, start_body, 0, unroll=DMA_UNROLL)

        def wait_body(t, carry):
            for k in range(TOP_K):
                row_copy(t, 0).wait()
            return carry

        lax.fori_loop(0, tm, wait_body, 0, unroll=DMA_UNROLL)

    for sl in range(2):
        @pl.when(jnp.logical_and(i < n_p_steps, (i & 1) == sl))
        def _():
            push_tile(hp_ref, sl)

        @pl.when(jnp.logical_and(i >= n_p_steps, (i & 1) == sl))
        def _():
            push_tile(hs_ref, sl)


def _moe_push(h2_p, h2_s, dest_tiles, pend, counts, *, n_blocks):
    tm = TM_ROW
    n_p_steps = h2_p.shape[0] // tm
    n_steps = n_p_steps + h2_s.shape[0] // tm
    assert dest_tiles.shape[0] == n_steps
    any_spec = pl.BlockSpec(memory_space=pl.ANY)
    tile = (tm, ROW_CHUNKS, LANES)
    grid_spec = pltpu.PrefetchScalarGridSpec(
        num_scalar_prefetch=2,
        grid=(n_steps,),
        in_specs=[any_spec,
                  pl.BlockSpec(tile, lambda i, pe, cn: (jnp.minimum(i, n_p_steps - 1), 0, 0)),
                  pl.BlockSpec(tile, lambda i, pe, cn: (jnp.maximum(i - n_p_steps, 0), 0, 0))],
        out_specs=any_spec,
        scratch_shapes=[pltpu.SMEM((TOP_K * tm,), I32), pltpu.SMEM((TOP_K * tm,), I32),
                        pltpu.VMEM((MOE_BLOCK, ROW_CHUNKS, LANES), F32),
                        pltpu.SemaphoreType.DMA((2,)), pltpu.SemaphoreType.DMA(()), pltpu.SemaphoreType.DMA(())],
    )
    return pl.pallas_call(
        functools.partial(_push_kernel, n_steps=n_steps, n_p_steps=n_p_steps, n_blocks=n_blocks),
        grid_spec=grid_spec,
        out_shape=jax.ShapeDtypeStruct((n_blocks * MOE_BLOCK, ROW_CHUNKS, LANES), F32),
        compiler_params=_cparams(("arbitrary",)),
        name="moe_push",
    )(pend, counts, dest_tiles, h2_p, h2_s)


def _expert_kernel(be_ref, nu_ref, x_ref, wg_ref, bg_ref, wu_ref, bu_ref, wd_ref, bd_ref, o_ref,
                   wg_bf, wu_bf, wd_bf):
    i = pl.program_id(0)
    n_used = nu_ref[0]
    prev = be_ref[jnp.maximum(i - 1, 0)]
    new_expert = jnp.logical_or(i == 0, be_ref[i] != prev)

    @pl.when(jnp.logical_and(i < n_used, new_expert))
    def _():
        wg_bf[...] = wg_ref[0].astype(BF16)
        wu_bf[...] = wu_ref[0].astype(BF16)
        wd_bf[...] = wd_ref[0].astype(BF16)

    @pl.when(i < n_used)
    def _():
        x = _load_row_tiles(x_ref, 0, MOE_BLOCK, BF16)
        g = jnp.dot(x, wg_bf[...], preferred_element_type=F32) + bg_ref[0]
        u = jnp.dot(x, wu_bf[...], preferred_element_type=F32) + bu_ref[0]
        g = jnp.minimum(g, SWIGLU_LIMIT)
        u = jnp.clip(u, -SWIGLU_LIMIT, SWIGLU_LIMIT)
        a = g * jax.nn.sigmoid(SWIGLU_ALPHA * g) * (u + 1.0)
        _store_row_tiles(o_ref, jnp.dot(a.astype(BF16), wd_bf[...], preferred_element_type=F32) + bd_ref[0])

    @pl.when(i >= n_used)
    def _():
        o_ref[...] = jnp.zeros(o_ref.shape, F32)


def _moe_expert(buf, blk_e, n_used, w_gate, b_gate, w_up, b_up, w_down, b_down):
    rows = buf.shape[0] // ROW_CHUNKS
    n_blocks = rows // MOE_BLOCK
    d_ff = w_gate.shape[2]

    def blk(i, be, nu):
        return jnp.minimum(i, nu[0] - 1)

    xspec = pl.BlockSpec((MOE_BLOCK * ROW_CHUNKS, LANES), lambda i, be, nu: (blk(i, be, nu), 0))
    wspec = lambda a, b: pl.BlockSpec((1, a, b), lambda i, be, nu: (be[blk(i, be, nu)], 0, 0))
    grid_spec = pltpu.PrefetchScalarGridSpec(
        num_scalar_prefetch=2,
        grid=(n_blocks,),
        in_specs=[xspec, wspec(D_MODEL, d_ff), wspec(1, d_ff), wspec(D_MODEL, d_ff), wspec(1, d_ff),
                  wspec(d_ff, D_MODEL), wspec(1, D_MODEL)],
        out_specs=pl.BlockSpec((MOE_BLOCK * ROW_CHUNKS, LANES), lambda i, be, nu: (i, 0)),
        scratch_shapes=[pltpu.VMEM((D_MODEL, d_ff), BF16), pltpu.VMEM((D_MODEL, d_ff), BF16),
                        pltpu.VMEM((d_ff, D_MODEL), BF16)],
    )
    return pl.pallas_call(
        _expert_kernel,
        grid_spec=grid_spec,
        out_shape=jax.ShapeDtypeStruct((rows * ROW_CHUNKS, LANES), F32),
        compiler_params=_cparams(("arbitrary",)),
        name="moe_expert",
    )(blk_e, n_used, buf, w_gate, b_gate.reshape(N_EXPERTS, 1, d_ff), w_up, b_up.reshape(N_EXPERTS, 1, d_ff),
      w_down, b_down.reshape(N_EXPERTS, 1, D_MODEL))


def _combine_kernel(dest_hbm, out_hbm, x1_ref, gate_ref, gt_ref, y_ref,
                    idx0, idx1, rows0, rows1, isem, rsem, *, n_steps):
    i = pl.program_id(0)
    tm = x1_ref.shape[0]
    n_rows = TOP_K * tm
    idx_refs = (idx0, idx1)
    rows_refs = (rows0, rows1)

    def idx_copy(step, sl):
        return pltpu.make_async_copy(dest_hbm.at[step], idx_refs[sl], isem.at[sl])

    def row_copy(src_row, sl, r):
        dst = rows_refs[sl].at[pl.ds(pl.multiple_of(r * ROW_CHUNKS, ROW_CHUNKS), ROW_CHUNKS), :]
        return pltpu.make_async_copy(out_hbm.at[src_row], dst, rsem.at[sl])

    def issue_rows(sl):
        def body(r2, carry):
            for par in range(2):
                r = 2 * r2 + par
                row_copy(idx_refs[sl][r], sl, r).start(priority=par)
            return carry
        lax.fori_loop(0, n_rows // 2, body, 0, unroll=DMA_UNROLL)

    def wait_rows(sl):
        def body(r, carry):
            row_copy(0, sl, r).wait()
            return carry
        lax.fori_loop(0, n_rows, body, 0, unroll=2 * DMA_UNROLL)

    @pl.when(i == 0)
    def _():
        idx_copy(0, 0).start()
        idx_copy(0, 0).wait()
        issue_rows(0)
        if n_steps > 1:
            idx_copy(1, 1).start()

    for sl in range(2):
        @pl.when((i & 1) == sl)
        def _():
            @pl.when(i + 1 < n_steps)
            def _():
                idx_copy(i + 1, 1 - sl).wait()
                issue_rows(1 - sl)

            @pl.when(i + 2 < n_steps)
            def _():
                idx_copy(i + 2, sl).start()

            wait_rows(sl)
            gates = gate_ref[...]
            y = jnp.zeros((tm, D_MODEL), F32)
            for k in range(TOP_K):
                y = y + gates[:, k:k + 1] * _load_row_tiles(rows_refs[sl], k * tm, tm)
            y_ref[...] = x1_ref[...] + gt_ref[...] * y


def _moe_combine(out_rows, dest_tiles, x1, gates, gt):
    n = x1.shape[0]
    tm = TM_ROW
    n_steps = n // tm
    per_row = gt.shape[0] != 1
    mod_spec = pl.BlockSpec((tm, D_MODEL), lambda i: (i, 0)) if per_row else pl.BlockSpec((1, D_MODEL), lambda i: (0, 0))
    return pl.pallas_call(
        functools.partial(_combine_kernel, n_steps=n_steps),
        grid=(n_steps,),
        in_specs=[pl.BlockSpec(memory_space=pl.ANY), pl.BlockSpec(memory_space=pl.ANY),
                  pl.BlockSpec((tm, D_MODEL), lambda i: (i, 0)),
                  pl.BlockSpec((tm, LANES), lambda i: (i, 0)),
                  mod_spec],
        out_specs=pl.BlockSpec((tm, D_MODEL), lambda i: (i, 0)),
        out_shape=jax.ShapeDtypeStruct((n, D_MODEL), F32),
        scratch_shapes=[pltpu.SMEM((TOP_K * tm,), I32), pltpu.SMEM((TOP_K * tm,), I32),
                        pltpu.VMEM((TOP_K * tm * ROW_CHUNKS, LANES), F32),
                        pltpu.VMEM((TOP_K * tm * ROW_CHUNKS, LANES), F32),
                        pltpu.SemaphoreType.DMA((2,)), pltpu.SemaphoreType.DMA((2,))],
        compiler_params=_cparams(("arbitrary",)),
        name="moe_combine",
    )(dest_tiles, out_rows, x1, gates, gt)


def _dest_tiles(dest, tm):
    n = dest.shape[0]
    return dest.reshape(n // tm, tm, TOP_K).transpose(0, 2, 1).reshape(n // tm, TOP_K * tm)


def kernel(x_prompt, x_sample, cache_k, cache_v, state_conv, page_table, c_prompt, c_sample, rel_table, w_ada, b_ada, g_mix, w_in, conv_w, q_norm_w, k_norm_w, lam_q1, lam_k1, lam_q2, lam_k2, subln_w, w_out, g_ffn, w_router, b_router, w_gate, b_gate, w_up, b_up, w_down, b_down):
    assert w_ada.shape[0] == 1, "single-layer trunk"
    bp, t_p, _ = x_prompt.shape
    bs, t_s, _ = x_sample.shape
    assert bp == 1
    n_s = bs * t_s
    n_all = t_p + n_s
    l = 0

    n_c = bp + bs
    c_rows = -(-n_c // SUBLANES) * SUBLANES
    c_all = jnp.concatenate([c_prompt, c_sample, jnp.zeros((c_rows - n_c, D_MODEL), F32)], axis=0)
    mod = _adaln(c_all, w_ada[l], b_ada[l]).reshape(c_rows, 6, D_MODEL)
    mod_p = [mod[0:1, j, :] for j in range(6)]
    per_token = jnp.broadcast_to(mod[1:1 + bs, None, :, :], (bs, t_s, 6, D_MODEL)).reshape(n_s, 6, D_MODEL)
    mod_s = [per_token[:, j, :] for j in range(6)]

    w_in_bf = w_in[l].astype(BF16)
    w_out_bf = w_out[l].astype(BF16)
    lam_vecs = jnp.stack([lam_q1[l], lam_k1[l], lam_q2[l], lam_k2[l]])

    xp = x_prompt.reshape(t_p, D_MODEL)
    xs = x_sample.reshape(n_s, D_MODEL)
    yc_p, q_p, k_p, v_p, kb_p, vt_p, u_tail = _inproj(
        xp, mod_p[1], mod_p[0], g_mix[l], w_in_bf, conv_w[l], q_norm_w[l], k_norm_w[l], None, None,
        tm=TM_IN, seq_len=None)
    st = state_conv[l]
    prev2 = jnp.broadcast_to(st[:, None, 0, :], (bs, t_s, D_CONV)).reshape(n_s, D_CONV)
    prev1 = jnp.broadcast_to(st[:, None, 1, :], (bs, t_s, D_CONV)).reshape(n_s, D_CONV)
    yc_s, q_s, k_s, v_s, _, _, u_s = _inproj(
        xs, mod_s[1], mod_s[0], g_mix[l], w_in_bf, conv_w[l], q_norm_w[l], k_norm_w[l], prev2, prev1,
        tm=n_s, seq_len=t_s)

    o_p = _attn_prompt(q_p, kb_p, vt_p, rel_table, lam_vecs, subln_w[l])
    n_phys = cache_k.shape[1]
    o_s = _attn_sample(q_s.astype(F32).reshape(bs, t_s, D_QKH),
                       k_s.reshape(bs, t_s * N_HEADS, 2 * D_QK), v_s.reshape(bs, t_s * N_HEADS, D_V),
                       cache_k[l].reshape(n_phys * PAGE_ROWS, 2 * D_QK), cache_v[l].reshape(n_phys * PAGE_ROWS, D_V),
                       page_table, rel_table, lam_vecs, subln_w[l]).reshape(n_s, D_ATT)

    w_router_pad = jnp.pad(w_router[l], ((0, 0), (0, LANES - N_EXPERTS)))
    b_router_pad = jnp.concatenate([b_router[l], jnp.full((LANES - N_EXPERTS,), NEG_INF, F32)]).reshape(1, LANES)
    cnt0 = jnp.zeros((1, LANES), F32)
    x1_p, h2_p, idx_p, gate_p, rank_p, cnt1 = _outproj(
        xp, yc_p, o_p, mod_p[2], mod_p[4], mod_p[3], g_ffn[l], w_out_bf, w_router_pad, b_router_pad, cnt0, tm=TM_OUT)
    x1_s, h2_s, idx_s, gate_s, rank_s, cnt2 = _outproj(
        xs, yc_s, o_s, mod_s[2], mod_s[4], mod_s[3], g_ffn[l], w_out_bf, w_router_pad, b_router_pad, cnt1, tm=TM_OUT)

    counts = cnt2[0, :N_EXPERTS].astype(I32)
    padded = (counts + MOE_BLOCK - 1) // MOE_BLOCK * MOE_BLOCK
    pend = jnp.cumsum(padded).astype(I32)
    pstart = pend - padded
    n_blocks = (n_all * TOP_K) // MOE_BLOCK + N_EXPERTS
    def slots(idx, rank):
        is_e = idx[:, :TOP_K, None] == jnp.arange(N_EXPERTS, dtype=I32)
        return jnp.sum(jnp.where(is_e, pstart, 0), axis=-1) + rank[:, :TOP_K]

    dest_p = slots(idx_p, rank_p)
    dest_s = slots(idx_s, rank_s)
    block_start = jnp.arange(n_blocks, dtype=I32) * MOE_BLOCK
    blk_e = jnp.minimum(jnp.sum(pend[None, :] <= block_start[:, None], axis=1), N_EXPERTS - 1).astype(I32)
    n_used = (pend[-1] // MOE_BLOCK).astype(I32).reshape(1)
    tiles_p = _dest_tiles(dest_p, TM_ROW)
    tiles_s = _dest_tiles(dest_s, TM_ROW)

    as_tiles = lambda a: a.reshape(a.shape[0] // ROW_CHUNKS, ROW_CHUNKS, LANES)
    buf = _moe_push(as_tiles(h2_p), as_tiles(h2_s), jnp.concatenate([tiles_p, tiles_s], axis=0), pend, counts,
                    n_blocks=n_blocks)
    out_rows = as_tiles(_moe_expert(buf.reshape(n_blocks * MOE_BLOCK * ROW_CHUNKS, LANES), blk_e, n_used,
                                    w_gate[l], b_gate[l], w_up[l], b_up[l], w_down[l], b_down[l]))
    y_p = _moe_combine(out_rows, tiles_p, x1_p, gate_p, mod_p[5])
    y_s = _moe_combine(out_rows, tiles_s, x1_s, gate_s, mod_s[5])

    k_prompt = k_p.reshape(1, bp, t_p, N_HEADS, 2 * D_QK)
    v_prompt = v_p.reshape(1, bp, t_p, N_HEADS, D_V)
    conv_prompt = u_tail[SUBLANES - (CONV_W - 1):, :].reshape(1, bp, CONV_W - 1, D_CONV)
    k_sample = k_s.reshape(1, bs, t_s, N_HEADS, 2 * D_QK)
    v_sample = v_s.reshape(1, bs, t_s, N_HEADS, D_V)
    conv_sample = u_s.reshape(bs, t_s, D_CONV)[:, t_s - (CONV_W - 1):, :].reshape(1, bs, CONV_W - 1, D_CONV)
    return (y_p.reshape(bp, t_p, D_MODEL), y_s.reshape(bs, t_s, D_MODEL),
            k_prompt, v_prompt, conv_prompt, k_sample, v_sample, conv_sample)
```

```python
import functools
import math

import numpy as np
import jax
import jax.numpy as jnp
from jax import lax
from jax.experimental import pallas as pl
from jax.experimental.pallas import tpu as pltpu

F32 = jnp.float32
BF16 = jnp.bfloat16
I32 = jnp.int32

D_MODEL = 1024
D_CONV = 512
CONV_W = 3
N_HEADS = 4
D_QK = 64
D_V = 128
D_ATT = N_HEADS * D_V
D_QKH = N_HEADS * 2 * D_QK
D_IN = 3 * D_CONV + 2 * D_QKH + D_ATT
N_BUCKETS = 32
MAX_EXACT = 16
MAX_DIST = 128
N_EXPERTS = 32
TOP_K = 4
SWIGLU_LIMIT = 7.0
SWIGLU_ALPHA = 1.702
PAGE_SIZE = 128
EPS = 1e-6
NEG_INF = -1e30
LAM_INIT = 0.8 - 0.6 * math.exp(-0.3 * 0)
LOG2E = math.log2(math.e)
Q_SCALE = D_QK ** -0.5 * LOG2E
SCORE_HEADROOM = 60.0

LANES = 128
SUBLANES = 8
VMEM_LIMIT = 56 * 1024 * 1024

TM_IN = 512
TQ = 512
TK = 512
FAR_GROUP = 4
PAGES_PER_STEP = 16
TM_OUT = 256
MOE_BLOCK = 512
TM_ROW = 128
DMA_UNROLL = 4
assert TK == TM_IN


def _bucket_thresholds():
    n = np.arange(0, 4 * MAX_DIST)
    nf = np.maximum(n, 1).astype(np.float32)
    val = np.log(nf / np.float32(MAX_EXACT)) / np.float32(math.log(MAX_DIST / MAX_EXACT)) * np.float32(N_BUCKETS - MAX_EXACT)
    large = np.minimum(MAX_EXACT + val.astype(np.int32), N_BUCKETS - 1)
    bucket = np.where(n < MAX_EXACT, n, large)
    assert np.all(np.diff(bucket) >= 0) and bucket[-1] == N_BUCKETS - 1
    thr = [int(np.argmax(bucket >= b)) for b in range(N_BUCKETS)]
    return thr


BUCKET_THR = _bucket_thresholds()
FAR_DIST = BUCKET_THR[N_BUCKETS - 1]


def _log2(n):
    assert n > 0 and n & (n - 1) == 0
    return n.bit_length() - 1


def _cparams(sem):
    return pltpu.CompilerParams(dimension_semantics=sem, vmem_limit_bytes=VMEM_LIMIT)


def _adaln_kernel(c_ref, w_ref, b_ref, o_ref):
    c = c_ref[...]
    s = c * jax.nn.sigmoid(c)
    o_ref[...] = jnp.dot(s, w_ref[...], preferred_element_type=F32,
                         precision=lax.Precision.HIGHEST) + b_ref[...]


def _adaln(c_all, w_ada, b_ada):
    rows = c_all.shape[0]
    n = w_ada.shape[1]
    bn = D_MODEL
    return pl.pallas_call(
        _adaln_kernel,
        grid=(n // bn,),
        in_specs=[pl.BlockSpec((rows, D_MODEL), lambda j: (0, 0)),
                  pl.BlockSpec((D_MODEL, bn), lambda j: (0, j)),
                  pl.BlockSpec((1, bn), lambda j: (0, j))],
        out_specs=pl.BlockSpec((rows, bn), lambda j: (0, j)),
        out_shape=jax.ShapeDtypeStruct((rows, n), F32),
        compiler_params=_cparams(("arbitrary",)),
        name="adaln",
    )(c_all, w_ada, b_ada.reshape(1, n))


def _group_rms(xb, w_row):
    lane = lax.broadcasted_iota(I32, xb.shape, 1)
    lo_half = lane < D_QK
    sq = xb * xb
    lo = jnp.sum(jnp.where(lo_half, sq, 0.0), axis=1, keepdims=True)
    hi = jnp.sum(jnp.where(lo_half, 0.0, sq), axis=1, keepdims=True)
    ms = jnp.where(lo_half, lo, hi) * (1.0 / D_QK)
    return xb * lax.rsqrt(ms + EPS) * w_row


def _inproj_kernel(x_ref, sc_ref, sh_ref, g_ref, w_ref, cw_ref, qw_ref, kw_ref, p2_ref, p1_ref,
                   yc_ref, q_ref, k_ref, v_ref, kb_ref, vt_ref, u_ref, carry_ref, *, seq_len):
    tm = x_ref.shape[0]
    x = x_ref[...]
    ms = jnp.mean(x * x, axis=1, keepdims=True)
    h = x * lax.rsqrt(ms + EPS) * g_ref[...]
    h = h * (1.0 + sc_ref[...]) + sh_ref[...]
    z = jnp.dot(h.astype(BF16), w_ref[...], preferred_element_type=F32)
    gb = z[:, 0:D_CONV]
    u = z[:, D_CONV:2 * D_CONV] * z[:, 2 * D_CONV:3 * D_CONV]
    row = lax.broadcasted_iota(I32, (tm, D_CONV), 0)
    u1 = pltpu.roll(u, 1, 0)
    u2 = pltpu.roll(u, 2, 0)
    if seq_len is None:
        @pl.when(pl.program_id(0) == 0)
        def _():
            carry_ref[...] = jnp.zeros_like(carry_ref)
        prev2 = carry_ref[SUBLANES - 2:SUBLANES - 1, :]
        prev1 = carry_ref[SUBLANES - 1:SUBLANES, :]
        u1 = jnp.where(row == 0, prev1, u1)
        u2 = jnp.where(row == 0, prev2, jnp.where(row == 1, prev1, u2))
        carry_ref[...] = u[tm - SUBLANES:tm, :]
        u_ref[...] = u[tm - SUBLANES:tm, :]
    else:
        pos = row & (seq_len - 1)
        u1 = jnp.where(pos == 0, p1_ref[...], u1)
        u2 = jnp.where(pos == 0, p2_ref[...], jnp.where(pos == 1, p1_ref[...], u2))
        u_ref[...] = u
    cw = cw_ref[...]
    yc = gb * (cw[0:1, :] * u2 + cw[1:2, :] * u1 + cw[2:3, :] * u)
    yc_ref[...] = yc.astype(yc_ref.dtype)
    o1 = 3 * D_CONV
    for hh in range(N_HEADS):
        c0 = o1 + hh * LANES
        qn = _group_rms(z[:, c0:c0 + LANES], qw_ref[...])
        q_ref[:, hh * LANES:(hh + 1) * LANES] = (qn * Q_SCALE).astype(q_ref.dtype)
        c1 = o1 + D_QKH + hh * LANES
        kn = _group_rms(z[:, c1:c1 + LANES], kw_ref[...])
        k_ref[pl.ds(hh, tm, stride=N_HEADS), :] = kn
        kb_ref[:, hh * LANES:(hh + 1) * LANES] = kn.astype(BF16)
    v = z[:, o1 + 2 * D_QKH:]
    for hh in range(N_HEADS):
        vh = v[:, hh * D_V:(hh + 1) * D_V]
        v_ref[pl.ds(hh, tm, stride=N_HEADS), :] = vh
        vt_ref[0, hh * D_V:(hh + 1) * D_V, :] = vh.T.astype(BF16)


def _inproj(x, sc, sh, g_mix, w_in_bf, conv_w, qw, kw, prev2, prev1, *, tm, seq_len):
    n = x.shape[0]
    grid = (n // tm,)
    per_row = sc.shape[0] != 1
    mod_spec = pl.BlockSpec((tm, D_MODEL), lambda i: (i, 0)) if per_row else pl.BlockSpec((1, D_MODEL), lambda i: (0, 0))
    if seq_len is None:
        prev2 = jnp.zeros((SUBLANES, D_CONV), F32)
        prev1 = prev2
        prev_spec = pl.BlockSpec((SUBLANES, D_CONV), lambda i: (0, 0))
        u_shape = jax.ShapeDtypeStruct((SUBLANES, D_CONV), F32)
        u_spec = pl.BlockSpec((SUBLANES, D_CONV), lambda i: (0, 0))
    else:
        assert seq_len & (seq_len - 1) == 0
        prev_spec = pl.BlockSpec((tm, D_CONV), lambda i: (i, 0))
        u_shape = jax.ShapeDtypeStruct((n, D_CONV), F32)
        u_spec = pl.BlockSpec((tm, D_CONV), lambda i: (i, 0))
    const = lambda shape: pl.BlockSpec(shape, lambda i: (0, 0))
    rows = lambda w: pl.BlockSpec((tm, w), lambda i: (i, 0))
    head_rows = pl.BlockSpec((tm * N_HEADS, LANES), lambda i: (i, 0))
    qw2 = jnp.tile(qw.reshape(1, D_QK), (1, 2))
    kw2 = jnp.tile(kw.reshape(1, D_QK), (1, 2))
    return pl.pallas_call(
        functools.partial(_inproj_kernel, seq_len=seq_len),
        grid=grid,
        in_specs=[rows(D_MODEL), mod_spec, mod_spec, const((1, D_MODEL)), const((D_MODEL, D_IN)),
                  const((CONV_W, D_CONV)), const((1, LANES)), const((1, LANES)), prev_spec, prev_spec],
        out_specs=[rows(D_CONV), rows(D_QKH), head_rows, head_rows, rows(D_QKH),
                   pl.BlockSpec((1, D_ATT, tm), lambda i: (i, 0, 0)), u_spec],
        out_shape=[jax.ShapeDtypeStruct((n, D_CONV), BF16),
                   jax.ShapeDtypeStruct((n, D_QKH), BF16),
                   jax.ShapeDtypeStruct((n * N_HEADS, 2 * D_QK), F32),
                   jax.ShapeDtypeStruct((n * N_HEADS, D_V), F32),
                   jax.ShapeDtypeStruct((n, D_QKH), BF16),
                   jax.ShapeDtypeStruct((n // tm, D_ATT, tm), BF16),
                   u_shape],
        scratch_shapes=[pltpu.VMEM((SUBLANES, D_CONV), F32)],
        compiler_params=_cparams(("arbitrary",)),
        name="inproj_seq" if seq_len is None else "inproj_batch",
    )(x, sc, sh, g_mix.reshape(1, D_MODEL), w_in_bf, conv_w, qw2, kw2, prev2, prev1)


def _bias_from_dist(dist, table_of_bucket):
    b = jnp.zeros(dist.shape, F32) + table_of_bucket(0)
    for bk in range(1, N_BUCKETS):
        b = jnp.where(dist >= BUCKET_THR[bk], table_of_bucket(bk), b)
    return (b - table_of_bucket(N_BUCKETS - 1)) * LOG2E


def _lambda_value(lam_ref):
    lq1, lk1, lq2, lk2 = (lam_ref[i:i + 1, :] for i in range(4))
    return (jnp.exp(jnp.sum(lq1 * lk1, axis=1, keepdims=True))
            - jnp.exp(jnp.sum(lq2 * lk2, axis=1, keepdims=True)) + LAM_INIT)


def _sub_norm(o, sw_row):
    ms = jnp.mean(o * o, axis=1, keepdims=True)
    return o * lax.rsqrt(ms + EPS) * sw_row * (1.0 - LAM_INIT)


def _attn_prompt_kernel(tbl_ref, q_ref, k_ref, vt_ref, lam_ref, sw_ref, o_ref,
                        bdiag_ref, bprev_ref, m_ref, l_ref, acc_ref):
    h = pl.program_id(0)
    i = pl.program_id(1)
    tq = q_ref.shape[0]
    tk = vt_ref.shape[2]
    assert tq == tk and tk >= FAR_DIST

    @pl.when(i == 0)
    def _():
        key = lax.broadcasted_iota(I32, (tk, tq), 0)
        qry = lax.broadcasted_iota(I32, (tk, tq), 1)
        d0 = qry - key
        tb = lambda b: tbl_ref[b, h]
        bdiag_ref[...] = jnp.where(d0 >= 0, _bias_from_dist(jnp.maximum(d0, 0), tb), NEG_INF)
        bprev_ref[...] = _bias_from_dist(d0 + tk, tb)

    m_ref[...] = jnp.full(m_ref.shape, -jnp.inf, F32)
    l_ref[...] = jnp.zeros(l_ref.shape, F32)
    acc_ref[...] = jnp.zeros(acc_ref.shape, F32)

    q = q_ref[...]
    lane = lax.broadcasted_iota(I32, q.shape, 1)
    zero = jnp.zeros_like(q)
    q_maps = (jnp.where(lane < D_QK, q, zero), jnp.where(lane < D_QK, zero, q))

    def scores_t(kb, mi):
        return lax.dot_general(kb, q_maps[mi], (((1,), (1,)), ((), ())), preferred_element_type=F32)

    def key_block(j):
        return k_ref[pl.ds(pl.multiple_of(j * tk, tk), tk), :], vt_ref[j]

    def exact_step(j, bias_ref):
        kb, vtb = key_block(j)
        for mi in range(2):
            st = scores_t(kb, mi)
            if bias_ref is not None:
                st = st + bias_ref[...]
            m_prev = m_ref[mi]
            m_new = jnp.maximum(m_prev, jnp.max(st, axis=0, keepdims=True))
            alpha = jnp.exp2(m_prev - m_new)
            pt = jnp.exp2(st - m_new)
            l_ref[mi] = alpha * l_ref[mi] + jnp.sum(pt, axis=0, keepdims=True)
            acc_ref[mi] = alpha * acc_ref[mi] + jnp.dot(vtb, pt.astype(BF16), preferred_element_type=F32)
            m_ref[mi] = m_new

    def streamed_step(j, n_blk, bias_ref=None):
        kb = k_ref[pl.ds(pl.multiple_of(j * tk, tk), n_blk * tk), :]
        sts = [scores_t(kb, mi) for mi in range(2)]
        if bias_ref is not None:
            assert n_blk == 1
            sts = [st + bias_ref[...] for st in sts]
        parts = []
        worst = None
        for mi in range(2):
            ref_pt = m_ref[mi]
            pt = jnp.exp2(sts[mi] - ref_pt)
            blk_max = jnp.max(sts[mi], axis=0, keepdims=True)
            l_add = jnp.sum(pt, axis=0, keepdims=True)
            acc_add = functools.reduce(lambda a, b: a + b, [
                jnp.dot(vt_ref[j + u], pt[u * tk:(u + 1) * tk, :].astype(BF16), preferred_element_type=F32)
                for u in range(n_blk)])
            parts.append((ref_pt, blk_max, l_add, acc_add))
            excess = blk_max - ref_pt
            worst = excess if worst is None else jnp.maximum(worst, excess)
        in_range = jnp.max(worst) <= SCORE_HEADROOM

        @pl.when(in_range)
        def _():
            for mi, (ref_pt, blk_max, l_add, acc_add) in enumerate(parts):
                m_new = jnp.maximum(ref_pt, blk_max)
                alpha = jnp.exp2(ref_pt - m_new)
                l_ref[mi] = alpha * (l_ref[mi] + l_add)
                acc_ref[mi] = alpha * (acc_ref[mi] + acc_add)
                m_ref[mi] = m_new

        @pl.when(jnp.logical_not(in_range))
        def _():
            for u in range(n_blk):
                exact_step(j + u, bias_ref)

    exact_step(i, bdiag_ref)

    @pl.when(i > 0)
    def _():
        streamed_step(i - 1, 1, bprev_ref)

    n_far = jnp.maximum(i - 1, 0)

    group_shift = _log2(FAR_GROUP)

    def far_group(jj, carry):
        streamed_step(FAR_GROUP * jj, FAR_GROUP)
        return carry

    lax.fori_loop(0, n_far >> group_shift, far_group, 0)

    def far_single(j, carry):
        streamed_step(j, 1)
        return carry

    lax.fori_loop((n_far >> group_shift) << group_shift, n_far, far_single, 0)

    lam = _lambda_value(lam_ref)
    ot = acc_ref[0] / l_ref[0] - lam * (acc_ref[1] / l_ref[1])
    o_ref[...] = _sub_norm(ot.T, sw_ref[...]).astype(o_ref.dtype)


def _attn_prompt(q_bf, k_bf, vt_bf, rel_table, lam_vecs, subln_w):
    t = q_bf.shape[0]
    nk, _, tk = vt_bf.shape
    assert tk == TK and nk * tk == t
    nq = t // TQ
    grid_spec = pltpu.PrefetchScalarGridSpec(
        num_scalar_prefetch=0,
        grid=(N_HEADS, nq),
        in_specs=[pl.BlockSpec(memory_space=pltpu.SMEM),
                  pl.BlockSpec((TQ, LANES), lambda h, i: (i, h)),
                  pl.BlockSpec((t, LANES), lambda h, i: (0, h)),
                  pl.BlockSpec((nk, D_V, tk), lambda h, i: (0, h, 0)),
                  pl.BlockSpec((4, D_QK), lambda h, i: (0, 0)),
                  pl.BlockSpec((1, D_V), lambda h, i: (0, 0))],
        out_specs=pl.BlockSpec((TQ, D_V), lambda h, i: (i, h)),
        scratch_shapes=[pltpu.VMEM((TK, TQ), F32), pltpu.VMEM((TK, TQ), F32),
                        pltpu.VMEM((2, 1, TQ), F32), pltpu.VMEM((2, 1, TQ), F32),
                        pltpu.VMEM((2, D_V, TQ), F32)],
    )
    return pl.pallas_call(
        _attn_prompt_kernel,
        grid_spec=grid_spec,
        out_shape=jax.ShapeDtypeStruct((t, D_ATT), BF16),
        compiler_params=_cparams(("arbitrary", "arbitrary")),
        name="attn_prompt",
    )(rel_table, q_bf, k_bf, vt_bf, lam_vecs, subln_w.reshape(1, D_V))


PAGE_ROWS = PAGE_SIZE * N_HEADS


def _attn_sample_kernel(pt_ref, q_ref, kn_ref, vn_ref, tblr_ref, lam_ref, sw_ref, *rest, n_groups, dec_seq):
    pg = PAGES_PER_STEP
    k_refs = rest[:pg]
    v_refs = rest[pg:2 * pg]
    o_ref = rest[2 * pg]
    qf_ref, qb_ref, knp_ref, vnp_ref, bmask_ref, blast_ref, bnew_ref, m_ref, l_ref, acc_ref = rest[2 * pg + 1:]
    b = pl.program_id(0)
    g = pl.program_id(1)
    q_rows = N_HEADS * 2 * dec_seq
    new_rows = dec_seq * N_HEADS
    head_shift = _log2(2 * dec_seq)
    key_shift = _log2(N_HEADS)
    assert PAGE_SIZE + 1 >= FAR_DIST and new_rows <= LANES

    @pl.when(jnp.logical_and(b == 0, g == 0))
    def _():
        tb = lambda bk: tblr_ref[:, bk:bk + 1]
        r = lax.broadcasted_iota(I32, (q_rows, PAGE_ROWS), 0)
        c = lax.broadcasted_iota(I32, (q_rows, PAGE_ROWS), 1)
        same_head = (c & (N_HEADS - 1)) == (r >> head_shift)
        tok = r & (dec_seq - 1)
        key = c >> key_shift
        bmask_ref[...] = jnp.where(same_head, 0.0, NEG_INF)
        blast_ref[...] = jnp.where(same_head, _bias_from_dist(tok + PAGE_SIZE - key, tb), NEG_INF)
        r2 = lax.broadcasted_iota(I32, (q_rows, LANES), 0)
        c2 = lax.broadcasted_iota(I32, (q_rows, LANES), 1)
        d2 = (r2 & (dec_seq - 1)) - (c2 >> key_shift)
        ok2 = jnp.logical_and((c2 & (N_HEADS - 1)) == (r2 >> head_shift),
                              jnp.logical_and(d2 >= 0, c2 < new_rows))
        bnew_ref[...] = jnp.where(ok2, _bias_from_dist(jnp.maximum(d2, 0), tb), NEG_INF)

    @pl.when(g == 0)
    def _():
        q = q_ref[0]
        lane = lax.broadcasted_iota(I32, (dec_seq, LANES), 1)
        for hh in range(N_HEADS):
            qh = q[:, hh * LANES:(hh + 1) * LANES]
            r0 = hh * 2 * dec_seq
            qf_ref[r0:r0 + dec_seq, :] = jnp.where(lane < D_QK, qh, 0.0)
            qf_ref[r0 + dec_seq:r0 + 2 * dec_seq, :] = jnp.where(lane < D_QK, 0.0, qh)
        qb_ref[...] = qf_ref[...].astype(BF16)
        knp_ref[...] = jnp.zeros(knp_ref.shape, F32)
        vnp_ref[...] = jnp.zeros(vnp_ref.shape, F32)
        knp_ref[0:new_rows, :] = kn_ref[0]
        vnp_ref[0:new_rows, :] = vn_ref[0]
        m_ref[...] = jnp.full(m_ref.shape, -jnp.inf, F32)
        l_ref[...] = jnp.zeros(l_ref.shape, F32)
        acc_ref[...] = jnp.zeros(acc_ref.shape, F32)

    qb = qb_ref[...]
    is_last = g == n_groups - 1

    def scores(k_rows):
        return lax.dot_general(qb, k_rows.astype(BF16), (((1,), (1,)), ((), ())), preferred_element_type=F32)

    def update(s_list, v_list):
        m_prev = m_ref[...]
        m_cur = functools.reduce(jnp.maximum, [jnp.max(s, axis=1, keepdims=True) for s in s_list])
        m_new = jnp.maximum(m_prev, m_cur)
        alpha = jnp.exp2(m_prev - m_new)
        l_new = alpha * l_ref[...]
        pv = None
        for s, v_rows in zip(s_list, v_list):
            p = jnp.exp2(s - jnp.concatenate([m_new] * (s.shape[1] // LANES), axis=1))
            l_new = l_new + jnp.sum(p, axis=1, keepdims=True)
            d = jnp.dot(p.astype(BF16), v_rows.astype(BF16), preferred_element_type=F32)
            pv = d if pv is None else pv + d
        acc_ref[...] = alpha * acc_ref[...] + pv
        l_ref[...] = l_new
        m_ref[...] = m_new

    s_list = [scores(k_refs[p][...]) + bmask_ref[...] for p in range(pg - 1)]
    s_list.append(scores(k_refs[pg - 1][...]) + jnp.where(is_last, blast_ref[...], bmask_ref[...]))
    update(s_list, [v_refs[p][...] for p in range(pg)])

    @pl.when(is_last)
    def _():
        update([scores(knp_ref[...]) + bnew_ref[...]], [vnp_ref[...]])
        lam = _lambda_value(lam_ref)
        for hh in range(N_HEADS):
            r0 = hh * 2 * dec_seq
            o1 = acc_ref[r0:r0 + dec_seq, :] / l_ref[r0:r0 + dec_seq, :]
            o2 = acc_ref[r0 + dec_seq:r0 + 2 * dec_seq, :] / l_ref[r0 + dec_seq:r0 + 2 * dec_seq, :]
            o_ref[0, :, hh * D_V:(hh + 1) * D_V] = _sub_norm(o1 - lam * o2, sw_ref[...])


def _attn_sample(q_s, kn_rows, vn_rows, cache_k_rows, cache_v_rows, page_table, rel_table, lam_vecs, subln_w):
    bsz, dec_seq, _ = q_s.shape
    n_pages = page_table.shape[1]
    pg = PAGES_PER_STEP
    assert n_pages % pg == 0
    n_groups = n_pages // pg
    q_rows = N_HEADS * 2 * dec_seq
    new_rows = dec_seq * N_HEADS
    tbl_rows = jnp.repeat(rel_table.T, 2 * dec_seq, axis=0)
    tbl_rows = jnp.pad(tbl_rows, ((0, 0), (0, LANES - N_BUCKETS)))
    per_b = lambda shape: pl.BlockSpec(shape, lambda b, g, pt: (b, 0, 0))
    const2 = lambda shape: pl.BlockSpec(shape, lambda b, g, pt: (0, 0))

    def page_spec(p):
        return pl.BlockSpec((PAGE_ROWS, LANES), lambda b, g, pt: (pt[b, g * pg + p], 0))

    grid_spec = pltpu.PrefetchScalarGridSpec(
        num_scalar_prefetch=1,
        grid=(bsz, n_groups),
        in_specs=[per_b((1, dec_seq, D_QKH)), per_b((1, new_rows, LANES)), per_b((1, new_rows, LANES)),
                  const2((q_rows, LANES)), const2((4, D_QK)), const2((1, D_V))]
                 + [page_spec(p) for p in range(pg)] + [page_spec(p) for p in range(pg)],
        out_specs=per_b((1, dec_seq, D_ATT)),
        scratch_shapes=[pltpu.VMEM((q_rows, LANES), F32), pltpu.VMEM((q_rows, LANES), BF16),
                        pltpu.VMEM((LANES, LANES), F32), pltpu.VMEM((LANES, D_V), F32),
                        pltpu.VMEM((q_rows, PAGE_ROWS), F32), pltpu.VMEM((q_rows, PAGE_ROWS), F32),
                        pltpu.VMEM((q_rows, LANES), F32),
                        pltpu.VMEM((q_rows, LANES), F32), pltpu.VMEM((q_rows, LANES), F32),
                        pltpu.VMEM((q_rows, D_V), F32)],
    )
    return pl.pallas_call(
        functools.partial(_attn_sample_kernel, n_groups=n_groups, dec_seq=dec_seq),
        grid_spec=grid_spec,
        out_shape=jax.ShapeDtypeStruct((bsz, dec_seq, D_ATT), F32),
        compiler_params=_cparams(("arbitrary", "arbitrary")),
        name="attn_sample",
    )(page_table, q_s, kn_rows, vn_rows, tbl_rows, lam_vecs, subln_w.reshape(1, D_V),
      *([cache_k_rows] * pg), *([cache_v_rows] * pg))


ROW_CHUNKS = D_MODEL // LANES
assert ROW_CHUNKS == SUBLANES


def _store_row_tiles(ref, mat):
    rows = mat.shape[0]
    for c in range(ROW_CHUNKS):
        ref[pl.ds(c, rows, stride=ROW_CHUNKS), :] = mat[:, c * LANES:(c + 1) * LANES]


def _load_row_tiles(ref, first_row, rows, dtype=F32):
    return jnp.concatenate(
        [ref[pl.ds(first_row * ROW_CHUNKS + c, rows, stride=ROW_CHUNKS), :].astype(dtype) for c in range(ROW_CHUNKS)],
        axis=1)


def _split_bf16(a):
    hi = a.astype(BF16)
    lo = (a - hi.astype(F32)).astype(BF16)
    return hi, lo


def _outproj_kernel(x_ref, yc_ref, o_ref, gt_ref, sc_ref, sh_ref, g_ref, wo_ref, wr_ref, br_ref, cin_ref,
                    x1_ref, h2_ref, idx_ref, gate_ref, rank_ref, cnt_ref):
    tm = x_ref.shape[0]

    @pl.when(pl.program_id(0) == 0)
    def _():
        cnt_ref[...] = cin_ref[...]

    mix = (jnp.dot(yc_ref[...].astype(BF16), wo_ref[0:D_CONV, :], preferred_element_type=F32)
           + jnp.dot(o_ref[...].astype(BF16), wo_ref[D_CONV:, :], preferred_element_type=F32))
    x1 = x_ref[...] + gt_ref[...] * mix
    x1_ref[...] = x1
    ms = jnp.mean(x1 * x1, axis=1, keepdims=True)
    h2 = x1 * lax.rsqrt(ms + EPS) * g_ref[...]
    h2 = h2 * (1.0 + sc_ref[...]) + sh_ref[...]
    _store_row_tiles(h2_ref, h2)
    h_hi, h_lo = _split_bf16(h2)
    w_hi, w_lo = _split_bf16(wr_ref[...])
    logits = (jnp.dot(h_hi, w_hi, preferred_element_type=F32)
              + jnp.dot(h_hi, w_lo, preferred_element_type=F32)
              + jnp.dot(h_lo, w_hi, preferred_element_type=F32)) + br_ref[...]
    lane = lax.broadcasted_iota(I32, (tm, LANES), 1)
    lane_f = lane.astype(F32)
    vals, ids = [], []
    cur = logits
    for _ in range(TOP_K):
        mx = jnp.max(cur, axis=1, keepdims=True)
        ik = jnp.min(jnp.where(cur == mx, lane_f, float(LANES)), axis=1, keepdims=True)
        vals.append(mx)
        ids.append(ik)
        cur = jnp.where(lane_f == ik, -jnp.inf, cur)
    es = [jnp.exp(v - vals[0]) for v in vals]
    denom = functools.reduce(lambda a, c: a + c, es)
    sel = jnp.zeros((tm, LANES), F32)
    idx_out = jnp.zeros((tm, LANES), F32)
    gate_out = jnp.zeros((tm, LANES), F32)
    for k in range(TOP_K):
        sel = sel + jnp.where(lane_f == ids[k], 1.0, 0.0)
        idx_out = jnp.where(lane == k, ids[k], idx_out)
        gate_out = jnp.where(lane == k, es[k] / denom, gate_out)
    r = lax.broadcasted_iota(I32, (tm, tm), 0)
    c = lax.broadcasted_iota(I32, (tm, tm), 1)
    lower = jnp.where(r > c, 1.0, 0.0).astype(BF16)
    before = jnp.dot(lower, sel.astype(BF16), preferred_element_type=F32) + cnt_ref[...]
    rank_out = jnp.zeros((tm, LANES), F32)
    for k in range(TOP_K):
        rk = jnp.sum(jnp.where(lane_f == ids[k], before, 0.0), axis=1, keepdims=True)
        rank_out = jnp.where(lane == k, rk, rank_out)
    cnt_ref[...] = cnt_ref[...] + jnp.sum(sel, axis=0, keepdims=True)
    idx_ref[...] = idx_out.astype(I32)
    gate_ref[...] = gate_out
    rank_ref[...] = rank_out.astype(I32)


def _outproj(x, yc, o, gt, sc, sh, g_ffn, w_out_bf, w_router_pad, b_router_pad, cnt_in, *, tm):
    n = x.shape[0]
    per_row = gt.shape[0] != 1
    mod_spec = pl.BlockSpec((tm, D_MODEL), lambda i: (i, 0)) if per_row else pl.BlockSpec((1, D_MODEL), lambda i: (0, 0))
    const = lambda shape: pl.BlockSpec(shape, lambda i: (0, 0))
    rows = lambda w: pl.BlockSpec((tm, w), lambda i: (i, 0))
    return pl.pallas_call(
        _outproj_kernel,
        grid=(n // tm,),
        in_specs=[rows(D_MODEL), rows(D_CONV), rows(D_ATT), mod_spec, mod_spec, mod_spec, const((1, D_MODEL)),
                  const((D_MODEL, D_MODEL)), const((D_MODEL, LANES)), const((1, LANES)), const((1, LANES))],
        out_specs=[rows(D_MODEL), pl.BlockSpec((tm * ROW_CHUNKS, LANES), lambda i: (i, 0)),
                   rows(LANES), rows(LANES), rows(LANES), const((1, LANES))],
        out_shape=[jax.ShapeDtypeStruct((n, D_MODEL), F32),
                   jax.ShapeDtypeStruct((n * ROW_CHUNKS, LANES), F32),
                   jax.ShapeDtypeStruct((n, LANES), I32),
                   jax.ShapeDtypeStruct((n, LANES), F32),
                   jax.ShapeDtypeStruct((n, LANES), I32),
                   jax.ShapeDtypeStruct((1, LANES), F32)],
        compiler_params=_cparams(("arbitrary",)),
        name="outproj",
    )(x, yc, o, gt, sc, sh, g_ffn.reshape(1, D_MODEL), w_out_bf, w_router_pad, b_router_pad, cnt_in)


def _push_kernel(pend_ref, cnt_ref, dest_hbm, hp_ref, hs_ref, buf_out, idx0, idx1, zero_ref, isem, rsem, zsem,
                 *, n_steps, n_p_steps, n_blocks):
    i = pl.program_id(0)
    tm = hp_ref.shape[0]
    idx_refs = (idx0, idx1)

    @pl.when(i == 0)
    def _():
        zero_ref[...] = jnp.zeros(zero_ref.shape, F32)
        blk_shift = _log2(MOE_BLOCK)
        n_used = pend_ref[N_EXPERTS - 1] >> blk_shift

        def zero_copy(block):
            start = pl.multiple_of(block * MOE_BLOCK, MOE_BLOCK)
            return pltpu.make_async_copy(zero_ref, buf_out.at[pl.ds(start, MOE_BLOCK)], zsem)

        def targets(e):
            return ((cnt_ref[e] > 0, (pend_ref[e] >> blk_shift) - 1), (n_used + e < n_blocks, n_used + e))

        for wait in (False, True):
            for e in range(N_EXPERTS):
                for cond, block in targets(e):
                    @pl.when(cond)
                    def _():
                        if wait:
                            zero_copy(block).wait()
                        else:
                            zero_copy(block).start()

    def idx_copy(step, sl):
        return pltpu.make_async_copy(dest_hbm.at[step], idx_refs[sl], isem.at[sl])

    @pl.when(i == 0)
    def _():
        idx_copy(0, 0).start()

    def push_tile(h_ref, sl):
        idx_copy(i, sl).wait()

        @pl.when(i + 1 < n_steps)
        def _():
            idx_copy(i + 1, 1 - sl).start()

        def row_copy(t, dst_row):
            return pltpu.make_async_copy(h_ref.at[t], buf_out.at[dst_row], rsem)

        def start_body(t, carry):
            for k in range(TOP_K):
                row_copy(t, idx_refs[sl][k * tm + t]).start(priority=k % 2)
            return carry

        lax.fori_loop(0, tm, start_body, 0, unroll=DMA_UNROLL)

        def wait_body(t, carry):
            for k in range(TOP_K):
                row_copy(t, 0).wait()
            return carry

        lax.fori_loop(0, tm, wait_body, 0, unroll=DMA_UNROLL)

    for sl in range(2):
        @pl.when(jnp.logical_and(i < n_p_steps, (i & 1) == sl))
        def _():
            push_tile(hp_ref, sl)

        @pl.when(jnp.logical_and(i >= n_p_steps, (i & 1) == sl))
        def _():
            push_tile(hs_ref, sl)


def _moe_push(h2_p, h2_s, dest_tiles, pend, counts, *, n_blocks):
    tm = TM_ROW
    n_p_steps = h2_p.shape[0] // tm
    n_steps = n_p_steps + h2_s.shape[0] // tm
    assert dest_tiles.shape[0] == n_steps
    any_spec = pl.BlockSpec(memory_space=pl.ANY)
    tile = (tm, ROW_CHUNKS, LANES)
    grid_spec = pltpu.PrefetchScalarGridSpec(
        num_scalar_prefetch=2,
        grid=(n_steps,),
        in_specs=[any_spec,
                  pl.BlockSpec(tile, lambda i, pe, cn: (jnp.minimum(i, n_p_steps - 1), 0, 0)),
                  pl.BlockSpec(tile, lambda i, pe, cn: (jnp.maximum(i - n_p_steps, 0), 0, 0))],
        out_specs=any_spec,
        scratch_shapes=[pltpu.SMEM((TOP_K * tm,), I32), pltpu.SMEM((TOP_K * tm,), I32),
                        pltpu.VMEM((MOE_BLOCK, ROW_CHUNKS, LANES), F32),
                        pltpu.SemaphoreType.DMA((2,)), pltpu.SemaphoreType.DMA(()), pltpu.SemaphoreType.DMA(())],
    )
    return pl.pallas_call(
        functools.partial(_push_kernel, n_steps=n_steps, n_p_steps=n_p_steps, n_blocks=n_blocks),
        grid_spec=grid_spec,
        out_shape=jax.ShapeDtypeStruct((n_blocks * MOE_BLOCK, ROW_CHUNKS, LANES), F32),
        compiler_params=_cparams(("arbitrary",)),
        name="moe_push",
    )(pend, counts, dest_tiles, h2_p, h2_s)


def _expert_kernel(be_ref, nxt_ref, par_ref, nu_ref, x_ref, wg_hbm, bg_ref, wu_hbm, bu_ref, wd_hbm, bd_ref, o_ref,
                   w32, wg_bf, wu_bf, wd_bf, wsem):
    i = pl.program_id(0)
    n_used = nu_ref[0]
    prev = be_ref[jnp.maximum(i - 1, 0)]
    new_expert = jnp.logical_or(i == 0, be_ref[i] != prev)
    w_hbm = (wg_hbm, wu_hbm, wd_hbm)

    def weight_copies(e, sl):
        return [pltpu.make_async_copy(w_hbm[m].at[e], w32.at[sl, m], wsem.at[sl, m]) for m in range(3)]

    @pl.when(i == 0)
    def _():
        for c in weight_copies(be_ref[0], par_ref[0]):
            c.start()

    @pl.when(jnp.logical_and(i < n_used, new_expert))
    def _():
        sl = par_ref[i]
        for c in weight_copies(be_ref[i], sl):
            c.wait()
        wg_bf[...] = w32[sl, 0].astype(BF16)
        wu_bf[...] = w32[sl, 1].astype(BF16)
        wd_bf[...] = w32[sl, 2].astype(BF16)

        @pl.when(nxt_ref[i] >= 0)
        def _():
            for c in weight_copies(nxt_ref[i], 1 - sl):
                c.start()

    @pl.when(i < n_used)
    def _():
        x = _load_row_tiles(x_ref, 0, MOE_BLOCK, BF16)
        g = jnp.dot(x, wg_bf[...], preferred_element_type=F32) + bg_ref[0]
        u = jnp.dot(x, wu_bf[...], preferred_element_type=F32) + bu_ref[0]
        g = jnp.minimum(g, SWIGLU_LIMIT)
        u = jnp.clip(u, -SWIGLU_LIMIT, SWIGLU_LIMIT)
        a = g * jax.nn.sigmoid(SWIGLU_ALPHA * g) * (u + 1.0)
        _store_row_tiles(o_ref, jnp.dot(a.astype(BF16), wd_bf[...], preferred_element_type=F32) + bd_ref[0])

    @pl.when(i >= n_used)
    def _():
        o_ref[...] = jnp.zeros(o_ref.shape, F32)


def _moe_expert(buf, blk_e, n_used, pend, w_gate, b_gate, w_up, b_up, w_down, b_down):
    rows = buf.shape[0] // ROW_CHUNKS
    n_blocks = rows // MOE_BLOCK
    d_ff = w_gate.shape[2]
    assert d_ff == D_MODEL

    def blk(i, be, nx, pa, nu):
        return jnp.minimum(i, nu[0] - 1)

    xspec = pl.BlockSpec((MOE_BLOCK * ROW_CHUNKS, LANES), lambda i, be, nx, pa, nu: (blk(i, be, nx, pa, nu), 0))
    bspec = pl.BlockSpec((1, 1, D_MODEL), lambda i, be, nx, pa, nu: (be[blk(i, be, nx, pa, nu)], 0, 0))
    any_spec = pl.BlockSpec(memory_space=pl.ANY)
    grid_spec = pltpu.PrefetchScalarGridSpec(
        num_scalar_prefetch=4,
        grid=(n_blocks,),
        in_specs=[xspec, any_spec, bspec, any_spec, bspec, any_spec, bspec],
        out_specs=pl.BlockSpec((MOE_BLOCK * ROW_CHUNKS, LANES), lambda i, be, nx, pa, nu: (i, 0)),
        scratch_shapes=[pltpu.VMEM((2, 3, D_MODEL, D_MODEL), F32),
                        pltpu.VMEM((D_MODEL, d_ff), BF16), pltpu.VMEM((D_MODEL, d_ff), BF16),
                        pltpu.VMEM((d_ff, D_MODEL), BF16), pltpu.SemaphoreType.DMA((2, 3))],
    )
    first = jnp.concatenate([jnp.ones((1,), bool), blk_e[1:] != blk_e[:-1]])
    parity = ((jnp.cumsum(first.astype(I32)) - 1) & 1).astype(I32)
    next_first = pend[blk_e] // MOE_BLOCK
    nxt_e = jnp.where(next_first < n_used[0], blk_e[jnp.minimum(next_first, n_blocks - 1)], -1).astype(I32)
    return pl.pallas_call(
        _expert_kernel,
        grid_spec=grid_spec,
        out_shape=jax.ShapeDtypeStruct((rows * ROW_CHUNKS, LANES), F32),
        compiler_params=_cparams(("arbitrary",)),
        name="moe_expert",
    )(blk_e, nxt_e, parity, n_used, buf, w_gate, b_gate.reshape(N_EXPERTS, 1, d_ff), w_up,
      b_up.reshape(N_EXPERTS, 1, d_ff), w_down, b_down.reshape(N_EXPERTS, 1, D_MODEL))


def _combine_kernel(dest_hbm, out_hbm, x1_ref, gate_ref, gt_ref, y_ref,
                    idx0, idx1, rows0, rows1, isem, rsem, *, n_steps):
    i = pl.program_id(0)
    tm = x1_ref.shape[0]
    n_rows = TOP_K * tm
    idx_refs = (idx0, idx1)
    rows_refs = (rows0, rows1)

    def idx_copy(step, sl):
        return pltpu.make_async_copy(dest_hbm.at[step], idx_refs[sl], isem.at[sl])

    def row_copy(src_row, sl, r):
        dst = rows_refs[sl].at[pl.ds(pl.multiple_of(r * ROW_CHUNKS, ROW_CHUNKS), ROW_CHUNKS), :]
        return pltpu.make_async_copy(out_hbm.at[src_row], dst, rsem.at[sl])

    def issue_rows(sl):
        def body(r2, carry):
            for par in range(2):
                r = 2 * r2 + par
                row_copy(idx_refs[sl][r], sl, r).start(priority=par)
            return carry
        lax.fori_loop(0, n_rows // 2, body, 0, unroll=DMA_UNROLL)

    def wait_rows(sl):
        def body(r, carry):
            row_copy(0, sl, r).wait()
            return carry
        lax.fori_loop(0, n_rows, body, 0, unroll=2 * DMA_UNROLL)

    @pl.when(i == 0)
    def _():
        idx_copy(0, 0).start()
        idx_copy(0, 0).wait()
        issue_rows(0)
        if n_steps > 1:
            idx_copy(1, 1).start()

    for sl in range(2):
        @pl.when((i & 1) == sl)
        def _():
            @pl.when(i + 1 < n_steps)
            def _():
                idx_copy(i + 1, 1 - sl).wait()
                issue_rows(1 - sl)

            @pl.when(i + 2 < n_steps)
            def _():
                idx_copy(i + 2, sl).start()

            wait_rows(sl)
            gates = gate_ref[...]
            y = jnp.zeros((tm, D_MODEL), F32)
            for k in range(TOP_K):
                y = y + gates[:, k:k + 1] * _load_row_tiles(rows_refs[sl], k * tm, tm)
            y_ref[...] = x1_ref[...] + gt_ref[...] * y


def _moe_combine(out_rows, dest_tiles, x1, gates, gt):
    n = x1.shape[0]
    tm = TM_ROW
    n_steps = n // tm
    per_row = gt.shape[0] != 1
    mod_spec = pl.BlockSpec((tm, D_MODEL), lambda i: (i, 0)) if per_row else pl.BlockSpec((1, D_MODEL), lambda i: (0, 0))
    return pl.pallas_call(
        functools.partial(_combine_kernel, n_steps=n_steps),
        grid=(n_steps,),
        in_specs=[pl.BlockSpec(memory_space=pl.ANY), pl.BlockSpec(memory_space=pl.ANY),
                  pl.BlockSpec((tm, D_MODEL), lambda i: (i, 0)),
                  pl.BlockSpec((tm, LANES), lambda i: (i, 0)),
                  mod_spec],
        out_specs=pl.BlockSpec((tm, D_MODEL), lambda i: (i, 0)),
        out_shape=jax.ShapeDtypeStruct((n, D_MODEL), F32),
        scratch_shapes=[pltpu.SMEM((TOP_K * tm,), I32), pltpu.SMEM((TOP_K * tm,), I32),
                        pltpu.VMEM((TOP_K * tm * ROW_CHUNKS, LANES), F32),
                        pltpu.VMEM((TOP_K * tm * ROW_CHUNKS, LANES), F32),
                        pltpu.SemaphoreType.DMA((2,)), pltpu.SemaphoreType.DMA((2,))],
        compiler_params=_cparams(("arbitrary",)),
        name="moe_combine",
    )(dest_tiles, out_rows, x1, gates, gt)


def _dest_tiles(dest, tm):
    n = dest.shape[0]
    return dest.reshape(n // tm, tm, TOP_K).transpose(0, 2, 1).reshape(n // tm, TOP_K * tm)


def kernel(x_prompt, x_sample, cache_k, cache_v, state_conv, page_table, c_prompt, c_sample, rel_table, w_ada, b_ada, g_mix, w_in, conv_w, q_norm_w, k_norm_w, lam_q1, lam_k1, lam_q2, lam_k2, subln_w, w_out, g_ffn, w_router, b_router, w_gate, b_gate, w_up, b_up, w_down, b_down):
    assert w_ada.shape[0] == 1, "single-layer trunk"
    bp, t_p, _ = x_prompt.shape
    bs, t_s, _ = x_sample.shape
    assert bp == 1
    n_s = bs * t_s
    n_all = t_p + n_s
    l = 0

    n_c = bp + bs
    c_rows = -(-n_c // SUBLANES) * SUBLANES
    c_all = jnp.concatenate([c_prompt, c_sample, jnp.zeros((c_rows - n_c, D_MODEL), F32)], axis=0)
    mod = _adaln(c_all, w_ada[l], b_ada[l]).reshape(c_rows, 6, D_MODEL)
    mod_p = [mod[0:1, j, :] for j in range(6)]
    per_token = jnp.broadcast_to(mod[1:1 + bs, None, :, :], (bs, t_s, 6, D_MODEL)).reshape(n_s, 6, D_MODEL)
    mod_s = [per_token[:, j, :] for j in range(6)]

    w_in_bf = w_in[l].astype(BF16)
    w_out_bf = w_out[l].astype(BF16)
    lam_vecs = jnp.stack([lam_q1[l], lam_k1[l], lam_q2[l], lam_k2[l]])

    xp = x_prompt.reshape(t_p, D_MODEL)
    xs = x_sample.reshape(n_s, D_MODEL)
    yc_p, q_p, k_p, v_p, kb_p, vt_p, u_tail = _inproj(
        xp, mod_p[1], mod_p[0], g_mix[l], w_in_bf, conv_w[l], q_norm_w[l], k_norm_w[l], None, None,
        tm=TM_IN, seq_len=None)
    st = state_conv[l]
    prev2 = jnp.broadcast_to(st[:, None, 0, :], (bs, t_s, D_CONV)).reshape(n_s, D_CONV)
    prev1 = jnp.broadcast_to(st[:, None, 1, :], (bs, t_s, D_CONV)).reshape(n_s, D_CONV)
    yc_s, q_s, k_s, v_s, _, _, u_s = _inproj(
        xs, mod_s[1], mod_s[0], g_mix[l], w_in_bf, conv_w[l], q_norm_w[l], k_norm_w[l], prev2, prev1,
        tm=n_s, seq_len=t_s)

    o_p = _attn_prompt(q_p, kb_p, vt_p, rel_table, lam_vecs, subln_w[l])
    n_phys = cache_k.shape[1]
    o_s = _attn_sample(q_s.astype(F32).reshape(bs, t_s, D_QKH),
                       k_s.reshape(bs, t_s * N_HEADS, 2 * D_QK), v_s.reshape(bs, t_s * N_HEADS, D_V),
                       cache_k[l].reshape(n_phys * PAGE_ROWS, 2 * D_QK), cache_v[l].reshape(n_phys * PAGE_ROWS, D_V),
                       page_table, rel_table, lam_vecs, subln_w[l]).reshape(n_s, D_ATT)

    w_router_pad = jnp.pad(w_router[l], ((0, 0), (0, LANES - N_EXPERTS)))
    b_router_pad = jnp.concatenate([b_router[l], jnp.full((LANES - N_EXPERTS,), NEG_INF, F32)]).reshape(1, LANES)
    cnt0 = jnp.zeros((1, LANES), F32)
    x1_p, h2_p, idx_p, gate_p, rank_p, cnt1 = _outproj(
        xp, yc_p, o_p, mod_p[2], mod_p[4], mod_p[3], g_ffn[l], w_out_bf, w_router_pad, b_router_pad, cnt0, tm=TM_OUT)
    x1_s, h2_s, idx_s, gate_s, rank_s, cnt2 = _outproj(
        xs, yc_s, o_s, mod_s[2], mod_s[4], mod_s[3], g_ffn[l], w_out_bf, w_router_pad, b_router_pad, cnt1, tm=TM_OUT)

    counts = cnt2[0, :N_EXPERTS].astype(I32)
    padded = (counts + MOE_BLOCK - 1) // MOE_BLOCK * MOE_BLOCK
    pend = jnp.cumsum(padded).astype(I32)
    pstart = pend - padded
    n_blocks = (n_all * TOP_K) // MOE_BLOCK + N_EXPERTS
    def slots(idx, rank):
        is_e = idx[:, :TOP_K, None] == jnp.arange(N_EXPERTS, dtype=I32)
        return jnp.sum(jnp.where(is_e, pstart, 0), axis=-1) + rank[:, :TOP_K]

    dest_p = slots(idx_p, rank_p)
    dest_s = slots(idx_s, rank_s)
    block_start = jnp.arange(n_blocks, dtype=I32) * MOE_BLOCK
    blk_e = jnp.minimum(jnp.sum(pend[None, :] <= block_start[:, None], axis=1), N_EXPERTS - 1).astype(I32)
    n_used = (pend[-1] // MOE_BLOCK).astype(I32).reshape(1)
    tiles_p = _dest_tiles(dest_p, TM_ROW)
    tiles_s = _dest_tiles(dest_s, TM_ROW)

    as_tiles = lambda a: a.reshape(a.shape[0] // ROW_CHUNKS, ROW_CHUNKS, LANES)
    buf = _moe_push(as_tiles(h2_p), as_tiles(h2_s), jnp.concatenate([tiles_p, tiles_s], axis=0), pend, counts,
                    n_blocks=n_blocks)
    out_rows = as_tiles(_moe_expert(buf.reshape(n_blocks * MOE_BLOCK * ROW_CHUNKS, LANES), blk_e, n_used, pend,
                                    w_gate[l], b_gate[l], w_up[l], b_up[l], w_down[l], b_down[l]))
    y_p = _moe_combine(out_rows, tiles_p, x1_p, gate_p, mod_p[5])
    y_s = _moe_combine(out_rows, tiles_s, x1_s, gate_s, mod_s[5])

    k_prompt = k_p.reshape(1, bp, t_p, N_HEADS, 2 * D_QK)
    v_prompt = v_p.reshape(1, bp, t_p, N_HEADS, D_V)
    conv_prompt = u_tail[SUBLANES - (CONV_W - 1):, :].reshape(1, bp, CONV_W - 1, D_CONV)
    k_sample = k_s.reshape(1, bs, t_s, N_HEADS, 2 * D_QK)
    v_sample = v_s.reshape(1, bs, t_s, N_HEADS, D_V)
    conv_sample = u_s.reshape(bs, t_s, D_CONV)[:, t_s - (CONV_W - 1):, :].reshape(1, bs, CONV_W - 1, D_CONV)
    return (y_p.reshape(bp, t_p, D_MODEL), y_s.reshape(bs, t_s, D_MODEL),
            k_prompt, v_prompt, conv_prompt, k_sample, v_sample, conv_sample)
```

```python
import functools
import math

import numpy as np
import jax
import jax.numpy as jnp
from jax import lax
from jax.experimental import pallas as pl
from jax.experimental.pallas import tpu as pltpu

F32 = jnp.float32
BF16 = jnp.bfloat16
I32 = jnp.int32

D_MODEL = 1024
D_CONV = 512
CONV_W = 3
N_HEADS = 4
D_QK = 64
D_V = 128
D_ATT = N_HEADS * D_V
D_QKH = N_HEADS * 2 * D_QK
D_IN = 3 * D_CONV + 2 * D_QKH + D_ATT
N_BUCKETS = 32
MAX_EXACT = 16
MAX_DIST = 128
N_EXPERTS = 32
TOP_K = 4
SWIGLU_LIMIT = 7.0
SWIGLU_ALPHA = 1.702
PAGE_SIZE = 128
EPS = 1e-6
NEG_INF = -1e30
LAM_INIT = 0.8 - 0.6 * math.exp(-0.3 * 0)
LOG2E = math.log2(math.e)
Q_SCALE = D_QK ** -0.5 * LOG2E
SCORE_HEADROOM = 60.0

LANES = 128
SUBLANES = 8
VMEM_LIMIT = 56 * 1024 * 1024

TM_IN = 512
TQ = 512
TK = 512
FAR_GROUP = 4
PAGES_PER_STEP = 32
TM_OUT = 256
MOE_BLOCK = 512
TM_ROW = 256
DMA_UNROLL = 4
assert TK == TM_IN


def _bucket_thresholds():
    n = np.arange(0, 4 * MAX_DIST)
    nf = np.maximum(n, 1).astype(np.float32)
    val = np.log(nf / np.float32(MAX_EXACT)) / np.float32(math.log(MAX_DIST / MAX_EXACT)) * np.float32(N_BUCKETS - MAX_EXACT)
    large = np.minimum(MAX_EXACT + val.astype(np.int32), N_BUCKETS - 1)
    bucket = np.where(n < MAX_EXACT, n, large)
    assert np.all(np.diff(bucket) >= 0) and bucket[-1] == N_BUCKETS - 1
    thr = [int(np.argmax(bucket >= b)) for b in range(N_BUCKETS)]
    return thr


BUCKET_THR = _bucket_thresholds()
FAR_DIST = BUCKET_THR[N_BUCKETS - 1]


def _log2(n):
    assert n > 0 and n & (n - 1) == 0
    return n.bit_length() - 1


def _cparams(sem):
    return pltpu.CompilerParams(dimension_semantics=sem, vmem_limit_bytes=VMEM_LIMIT)


def _adaln_kernel(c_ref, w_ref, b_ref, o_ref):
    c = c_ref[...]
    s = c * jax.nn.sigmoid(c)
    o_ref[...] = jnp.dot(s, w_ref[...], preferred_element_type=F32,
                         precision=lax.Precision.HIGHEST) + b_ref[...]


def _adaln(c_all, w_ada, b_ada):
    rows = c_all.shape[0]
    n = w_ada.shape[1]
    bn = D_MODEL
    return pl.pallas_call(
        _adaln_kernel,
        grid=(n // bn,),
        in_specs=[pl.BlockSpec((rows, D_MODEL), lambda j: (0, 0)),
                  pl.BlockSpec((D_MODEL, bn), lambda j: (0, j)),
                  pl.BlockSpec((1, bn), lambda j: (0, j))],
        out_specs=pl.BlockSpec((rows, bn), lambda j: (0, j)),
        out_shape=jax.ShapeDtypeStruct((rows, n), F32),
        compiler_params=_cparams(("arbitrary",)),
        name="adaln",
    )(c_all, w_ada, b_ada.reshape(1, n))


def _group_rms(xb, w_row):
    lane = lax.broadcasted_iota(I32, xb.shape, 1)
    lo_half = lane < D_QK
    sq = xb * xb
    lo = jnp.sum(jnp.where(lo_half, sq, 0.0), axis=1, keepdims=True)
    hi = jnp.sum(jnp.where(lo_half, 0.0, sq), axis=1, keepdims=True)
    ms = jnp.where(lo_half, lo, hi) * (1.0 / D_QK)
    return xb * lax.rsqrt(ms + EPS) * w_row


def _inproj_kernel(x_ref, sc_ref, sh_ref, g_ref, w_ref, cw_ref, qw_ref, kw_ref, p2_ref, p1_ref,
                   yc_ref, q_ref, k_ref, v_ref, kb_ref, vt_ref, u_ref, carry_ref, *, seq_len):
    tm = x_ref.shape[0]
    x = x_ref[...]
    ms = jnp.mean(x * x, axis=1, keepdims=True)
    h = x * lax.rsqrt(ms + EPS) * g_ref[...]
    h = h * (1.0 + sc_ref[...]) + sh_ref[...]
    z = jnp.dot(h.astype(BF16), w_ref[...], preferred_element_type=F32)
    gb = z[:, 0:D_CONV]
    u = z[:, D_CONV:2 * D_CONV] * z[:, 2 * D_CONV:3 * D_CONV]
    row = lax.broadcasted_iota(I32, (tm, D_CONV), 0)
    u1 = pltpu.roll(u, 1, 0)
    u2 = pltpu.roll(u, 2, 0)
    if seq_len is None:
        @pl.when(pl.program_id(0) == 0)
        def _():
            carry_ref[...] = jnp.zeros_like(carry_ref)
        prev2 = carry_ref[SUBLANES - 2:SUBLANES - 1, :]
        prev1 = carry_ref[SUBLANES - 1:SUBLANES, :]
        u1 = jnp.where(row == 0, prev1, u1)
        u2 = jnp.where(row == 0, prev2, jnp.where(row == 1, prev1, u2))
        carry_ref[...] = u[tm - SUBLANES:tm, :]
        u_ref[...] = u[tm - SUBLANES:tm, :]
    else:
        pos = row & (seq_len - 1)
        u1 = jnp.where(pos == 0, p1_ref[...], u1)
        u2 = jnp.where(pos == 0, p2_ref[...], jnp.where(pos == 1, p1_ref[...], u2))
        u_ref[...] = u
    cw = cw_ref[...]
    yc = gb * (cw[0:1, :] * u2 + cw[1:2, :] * u1 + cw[2:3, :] * u)
    yc_ref[...] = yc.astype(yc_ref.dtype)
    o1 = 3 * D_CONV
    for hh in range(N_HEADS):
        c0 = o1 + hh * LANES
        qn = _group_rms(z[:, c0:c0 + LANES], qw_ref[...])
        q_ref[:, hh * LANES:(hh + 1) * LANES] = (qn * Q_SCALE).astype(q_ref.dtype)
        c1 = o1 + D_QKH + hh * LANES
        kn = _group_rms(z[:, c1:c1 + LANES], kw_ref[...])
        k_ref[pl.ds(hh, tm, stride=N_HEADS), :] = kn
        kb_ref[:, hh * LANES:(hh + 1) * LANES] = kn.astype(BF16)
    v = z[:, o1 + 2 * D_QKH:]
    for hh in range(N_HEADS):
        vh = v[:, hh * D_V:(hh + 1) * D_V]
        v_ref[pl.ds(hh, tm, stride=N_HEADS), :] = vh
        vt_ref[0, hh * D_V:(hh + 1) * D_V, :] = vh.T.astype(BF16)


def _inproj(x, sc, sh, g_mix, w_in_bf, conv_w, qw, kw, prev2, prev1, *, tm, seq_len):
    n = x.shape[0]
    grid = (n // tm,)
    per_row = sc.shape[0] != 1
    mod_spec = pl.BlockSpec((tm, D_MODEL), lambda i: (i, 0)) if per_row else pl.BlockSpec((1, D_MODEL), lambda i: (0, 0))
    if seq_len is None:
        prev2 = jnp.zeros((SUBLANES, D_CONV), F32)
        prev1 = prev2
        prev_spec = pl.BlockSpec((SUBLANES, D_CONV), lambda i: (0, 0))
        u_shape = jax.ShapeDtypeStruct((SUBLANES, D_CONV), F32)
        u_spec = pl.BlockSpec((SUBLANES, D_CONV), lambda i: (0, 0))
    else:
        assert seq_len & (seq_len - 1) == 0
        prev_spec = pl.BlockSpec((tm, D_CONV), lambda i: (i, 0))
        u_shape = jax.ShapeDtypeStruct((n, D_CONV), F32)
        u_spec = pl.BlockSpec((tm, D_CONV), lambda i: (i, 0))
    const = lambda shape: pl.BlockSpec(shape, lambda i: (0, 0))
    rows = lambda w: pl.BlockSpec((tm, w), lambda i: (i, 0))
    head_rows = pl.BlockSpec((tm * N_HEADS, LANES), lambda i: (i, 0))
    qw2 = jnp.tile(qw.reshape(1, D_QK), (1, 2))
    kw2 = jnp.tile(kw.reshape(1, D_QK), (1, 2))
    return pl.pallas_call(
        functools.partial(_inproj_kernel, seq_len=seq_len),
        grid=grid,
        in_specs=[rows(D_MODEL), mod_spec, mod_spec, const((1, D_MODEL)), const((D_MODEL, D_IN)),
                  const((CONV_W, D_CONV)), const((1, LANES)), const((1, LANES)), prev_spec, prev_spec],
        out_specs=[rows(D_CONV), rows(D_QKH), head_rows, head_rows, rows(D_QKH),
                   pl.BlockSpec((1, D_ATT, tm), lambda i: (i, 0, 0)), u_spec],
        out_shape=[jax.ShapeDtypeStruct((n, D_CONV), BF16),
                   jax.ShapeDtypeStruct((n, D_QKH), BF16),
                   jax.ShapeDtypeStruct((n * N_HEADS, 2 * D_QK), F32),
                   jax.ShapeDtypeStruct((n * N_HEADS, D_V), F32),
                   jax.ShapeDtypeStruct((n, D_QKH), BF16),
                   jax.ShapeDtypeStruct((n // tm, D_ATT, tm), BF16),
                   u_shape],
        scratch_shapes=[pltpu.VMEM((SUBLANES, D_CONV), F32)],
        compiler_params=_cparams(("arbitrary",)),
        name="inproj_seq" if seq_len is None else "inproj_batch",
    )(x, sc, sh, g_mix.reshape(1, D_MODEL), w_in_bf, conv_w, qw2, kw2, prev2, prev1)


def _bias_from_dist(dist, table_of_bucket):
    b = jnp.zeros(dist.shape, F32) + table_of_bucket(0)
    for bk in range(1, N_BUCKETS):
        b = jnp.where(dist >= BUCKET_THR[bk], table_of_bucket(bk), b)
    return (b - table_of_bucket(N_BUCKETS - 1)) * LOG2E


def _lambda_value(lam_ref):
    lq1, lk1, lq2, lk2 = (lam_ref[i:i + 1, :] for i in range(4))
    return (jnp.exp(jnp.sum(lq1 * lk1, axis=1, keepdims=True))
            - jnp.exp(jnp.sum(lq2 * lk2, axis=1, keepdims=True)) + LAM_INIT)


def _sub_norm(o, sw_row):
    ms = jnp.mean(o * o, axis=1, keepdims=True)
    return o * lax.rsqrt(ms + EPS) * sw_row * (1.0 - LAM_INIT)


def _attn_prompt_kernel(tbl_ref, q_ref, k_ref, vt_ref, lam_ref, sw_ref, o_ref,
                        bdiag_ref, bprev_ref, m_ref, l_ref, acc_ref):
    h = pl.program_id(0)
    i = pl.program_id(1)
    tq = q_ref.shape[0]
    tk = vt_ref.shape[2]
    assert tq == tk and tk >= FAR_DIST

    @pl.when(i == 0)
    def _():
        key = lax.broadcasted_iota(I32, (tk, tq), 0)
        qry = lax.broadcasted_iota(I32, (tk, tq), 1)
        d0 = qry - key
        tb = lambda b: tbl_ref[b, h]
        bdiag_ref[...] = jnp.where(d0 >= 0, _bias_from_dist(jnp.maximum(d0, 0), tb), NEG_INF)
        bprev_ref[...] = _bias_from_dist(d0 + tk, tb)

    m_ref[...] = jnp.full(m_ref.shape, -jnp.inf, F32)
    l_ref[...] = jnp.zeros(l_ref.shape, F32)
    acc_ref[...] = jnp.zeros(acc_ref.shape, F32)

    q = q_ref[...]
    lane = lax.broadcasted_iota(I32, q.shape, 1)
    zero = jnp.zeros_like(q)
    q_maps = (jnp.where(lane < D_QK, q, zero), jnp.where(lane < D_QK, zero, q))

    def scores_t(kb, mi):
        return lax.dot_general(kb, q_maps[mi], (((1,), (1,)), ((), ())), preferred_element_type=F32)

    def key_block(j):
        return k_ref[pl.ds(pl.multiple_of(j * tk, tk), tk), :], vt_ref[j]

    def exact_step(j, bias_ref):
        kb, vtb = key_block(j)
        for mi in range(2):
            st = scores_t(kb, mi)
            if bias_ref is not None:
                st = st + bias_ref[...]
            m_prev = m_ref[mi]
            m_new = jnp.maximum(m_prev, jnp.max(st, axis=0, keepdims=True))
            alpha = jnp.exp2(m_prev - m_new)
            pt = jnp.exp2(st - m_new)
            l_ref[mi] = alpha * l_ref[mi] + jnp.sum(pt, axis=0, keepdims=True)
            acc_ref[mi] = alpha * acc_ref[mi] + jnp.dot(vtb, pt.astype(BF16), preferred_element_type=F32)
            m_ref[mi] = m_new

    def streamed_step(j, n_blk, bias_ref=None):
        kb = k_ref[pl.ds(pl.multiple_of(j * tk, tk), n_blk * tk), :]
        sts = [scores_t(kb, mi) for mi in range(2)]
        if bias_ref is not None:
            assert n_blk == 1
            sts = [st + bias_ref[...] for st in sts]
        parts = []
        worst = None
        for mi in range(2):
            ref_pt = m_ref[mi]
            pt = jnp.exp2(sts[mi] - ref_pt)
            blk_max = jnp.max(sts[mi], axis=0, keepdims=True)
            l_add = jnp.sum(pt, axis=0, keepdims=True)
            acc_add = functools.reduce(lambda a, b: a + b, [
                jnp.dot(vt_ref[j + u], pt[u * tk:(u + 1) * tk, :].astype(BF16), preferred_element_type=F32)
                for u in range(n_blk)])
            parts.append((ref_pt, blk_max, l_add, acc_add))
            excess = blk_max - ref_pt
            worst = excess if worst is None else jnp.maximum(worst, excess)
        in_range = jnp.max(worst) <= SCORE_HEADROOM

        @pl.when(in_range)
        def _():
            for mi, (ref_pt, blk_max, l_add, acc_add) in enumerate(parts):
                m_new = jnp.maximum(ref_pt, blk_max)
                alpha = jnp.exp2(ref_pt - m_new)
                l_ref[mi] = alpha * (l_ref[mi] + l_add)
                acc_ref[mi] = alpha * (acc_ref[mi] + acc_add)
                m_ref[mi] = m_new

        @pl.when(jnp.logical_not(in_range))
        def _():
            for u in range(n_blk):
                exact_step(j + u, bias_ref)

    exact_step(i, bdiag_ref)

    @pl.when(i > 0)
    def _():
        streamed_step(i - 1, 1, bprev_ref)

    n_far = jnp.maximum(i - 1, 0)

    group_shift = _log2(FAR_GROUP)

    def far_group(jj, carry):
        streamed_step(FAR_GROUP * jj, FAR_GROUP)
        return carry

    lax.fori_loop(0, n_far >> group_shift, far_group, 0)

    def far_single(j, carry):
        streamed_step(j, 1)
        return carry

    lax.fori_loop((n_far >> group_shift) << group_shift, n_far, far_single, 0)

    lam = _lambda_value(lam_ref)
    ot = acc_ref[0] / l_ref[0] - lam * (acc_ref[1] / l_ref[1])
    o_ref[...] = _sub_norm(ot.T, sw_ref[...]).astype(o_ref.dtype)


def _attn_prompt(q_bf, k_bf, vt_bf, rel_table, lam_vecs, subln_w):
    t = q_bf.shape[0]
    nk, _, tk = vt_bf.shape
    assert tk == TK and nk * tk == t
    nq = t // TQ
    grid_spec = pltpu.PrefetchScalarGridSpec(
        num_scalar_prefetch=0,
        grid=(N_HEADS, nq),
        in_specs=[pl.BlockSpec(memory_space=pltpu.SMEM),
                  pl.BlockSpec((TQ, LANES), lambda h, i: (i, h)),
                  pl.BlockSpec((t, LANES), lambda h, i: (0, h)),
                  pl.BlockSpec((nk, D_V, tk), lambda h, i: (0, h, 0)),
                  pl.BlockSpec((4, D_QK), lambda h, i: (0, 0)),
                  pl.BlockSpec((1, D_V), lambda h, i: (0, 0))],
        out_specs=pl.BlockSpec((TQ, D_V), lambda h, i: (i, h)),
        scratch_shapes=[pltpu.VMEM((TK, TQ), F32), pltpu.VMEM((TK, TQ), F32),
                        pltpu.VMEM((2, 1, TQ), F32), pltpu.VMEM((2, 1, TQ), F32),
                        pltpu.VMEM((2, D_V, TQ), F32)],
    )
    return pl.pallas_call(
        _attn_prompt_kernel,
        grid_spec=grid_spec,
        out_shape=jax.ShapeDtypeStruct((t, D_ATT), BF16),
        compiler_params=_cparams(("arbitrary", "arbitrary")),
        name="attn_prompt",
    )(rel_table, q_bf, k_bf, vt_bf, lam_vecs, subln_w.reshape(1, D_V))


PAGE_ROWS = PAGE_SIZE * N_HEADS


def _attn_sample_kernel(pt_ref, q_ref, kn_ref, vn_ref, tblr_ref, lam_ref, sw_ref, *rest, n_groups, dec_seq):
    pg = PAGES_PER_STEP
    k_refs = rest[:pg]
    v_refs = rest[pg:2 * pg]
    o_ref = rest[2 * pg]
    qf_ref, qb_ref, knp_ref, vnp_ref, bmask_ref, blast_ref, bnew_ref, m_ref, l_ref, acc_ref = rest[2 * pg + 1:]
    b = pl.program_id(0)
    g = pl.program_id(1)
    q_rows = N_HEADS * 2 * dec_seq
    new_rows = dec_seq * N_HEADS
    head_shift = _log2(2 * dec_seq)
    key_shift = _log2(N_HEADS)
    assert PAGE_SIZE + 1 >= FAR_DIST and new_rows <= LANES

    @pl.when(jnp.logical_and(b == 0, g == 0))
    def _():
        tb = lambda bk: tblr_ref[:, bk:bk + 1]
        r = lax.broadcasted_iota(I32, (q_rows, PAGE_ROWS), 0)
        c = lax.broadcasted_iota(I32, (q_rows, PAGE_ROWS), 1)
        same_head = (c & (N_HEADS - 1)) == (r >> head_shift)
        tok = r & (dec_seq - 1)
        key = c >> key_shift
        bmask_ref[...] = jnp.where(same_head, 0.0, NEG_INF)
        blast_ref[...] = jnp.where(same_head, _bias_from_dist(tok + PAGE_SIZE - key, tb), NEG_INF)
        r2 = lax.broadcasted_iota(I32, (q_rows, LANES), 0)
        c2 = lax.broadcasted_iota(I32, (q_rows, LANES), 1)
        d2 = (r2 & (dec_seq - 1)) - (c2 >> key_shift)
        ok2 = jnp.logical_and((c2 & (N_HEADS - 1)) == (r2 >> head_shift),
                              jnp.logical_and(d2 >= 0, c2 < new_rows))
        bnew_ref[...] = jnp.where(ok2, _bias_from_dist(jnp.maximum(d2, 0), tb), NEG_INF)

    @pl.when(g == 0)
    def _():
        q = q_ref[0]
        lane = lax.broadcasted_iota(I32, (dec_seq, LANES), 1)
        for hh in range(N_HEADS):
            qh = q[:, hh * LANES:(hh + 1) * LANES]
            r0 = hh * 2 * dec_seq
            qf_ref[r0:r0 + dec_seq, :] = jnp.where(lane < D_QK, qh, 0.0)
            qf_ref[r0 + dec_seq:r0 + 2 * dec_seq, :] = jnp.where(lane < D_QK, 0.0, qh)
        qb_ref[...] = qf_ref[...].astype(BF16)
        knp_ref[...] = jnp.zeros(knp_ref.shape, F32)
        vnp_ref[...] = jnp.zeros(vnp_ref.shape, F32)
        knp_ref[0:new_rows, :] = kn_ref[0]
        vnp_ref[0:new_rows, :] = vn_ref[0]
        m_ref[...] = jnp.full(m_ref.shape, -jnp.inf, F32)
        l_ref[...] = jnp.zeros(l_ref.shape, F32)
        acc_ref[...] = jnp.zeros(acc_ref.shape, F32)

    qb = qb_ref[...]
    is_last = g == n_groups - 1

    def scores(k_rows):
        return lax.dot_general(qb, k_rows.astype(BF16), (((1,), (1,)), ((), ())), preferred_element_type=F32)

    def update(s_list, v_list):
        m_prev = m_ref[...]
        m_cur = functools.reduce(jnp.maximum, [jnp.max(s, axis=1, keepdims=True) for s in s_list])
        m_new = jnp.maximum(m_prev, m_cur)
        alpha = jnp.exp2(m_prev - m_new)
        l_new = alpha * l_ref[...]
        pv = None
        for s, v_rows in zip(s_list, v_list):
            p = jnp.exp2(s - jnp.concatenate([m_new] * (s.shape[1] // LANES), axis=1))
            l_new = l_new + jnp.sum(p, axis=1, keepdims=True)
            d = jnp.dot(p.astype(BF16), v_rows.astype(BF16), preferred_element_type=F32)
            pv = d if pv is None else pv + d
        acc_ref[...] = alpha * acc_ref[...] + pv
        l_ref[...] = l_new
        m_ref[...] = m_new

    s_list = [scores(k_refs[p][...]) + bmask_ref[...] for p in range(pg - 1)]
    s_list.append(scores(k_refs[pg - 1][...]) + jnp.where(is_last, blast_ref[...], bmask_ref[...]))
    update(s_list, [v_refs[p][...] for p in range(pg)])

    @pl.when(is_last)
    def _():
        update([scores(knp_ref[...]) + bnew_ref[...]], [vnp_ref[...]])
        lam = _lambda_value(lam_ref)
        for hh in range(N_HEADS):
            r0 = hh * 2 * dec_seq
            o1 = acc_ref[r0:r0 + dec_seq, :] / l_ref[r0:r0 + dec_seq, :]
            o2 = acc_ref[r0 + dec_seq:r0 + 2 * dec_seq, :] / l_ref[r0 + dec_seq:r0 + 2 * dec_seq, :]
            o_ref[0, :, hh * D_V:(hh + 1) * D_V] = _sub_norm(o1 - lam * o2, sw_ref[...])


def _attn_sample(q_s, kn_rows, vn_rows, cache_k_rows, cache_v_rows, page_table, rel_table, lam_vecs, subln_w):
    bsz, dec_seq, _ = q_s.shape
    n_pages = page_table.shape[1]
    pg = PAGES_PER_STEP
    assert n_pages % pg == 0
    n_groups = n_pages // pg
    q_rows = N_HEADS * 2 * dec_seq
    new_rows = dec_seq * N_HEADS
    tbl_rows = jnp.repeat(rel_table.T, 2 * dec_seq, axis=0)
    tbl_rows = jnp.pad(tbl_rows, ((0, 0), (0, LANES - N_BUCKETS)))
    per_b = lambda shape: pl.BlockSpec(shape, lambda b, g, pt: (b, 0, 0))
    const2 = lambda shape: pl.BlockSpec(shape, lambda b, g, pt: (0, 0))

    def page_spec(p):
        return pl.BlockSpec((PAGE_ROWS, LANES), lambda b, g, pt: (pt[b, g * pg + p], 0))

    grid_spec = pltpu.PrefetchScalarGridSpec(
        num_scalar_prefetch=1,
        grid=(bsz, n_groups),
        in_specs=[per_b((1, dec_seq, D_QKH)), per_b((1, new_rows, LANES)), per_b((1, new_rows, LANES)),
                  const2((q_rows, LANES)), const2((4, D_QK)), const2((1, D_V))]
                 + [page_spec(p) for p in range(pg)] + [page_spec(p) for p in range(pg)],
        out_specs=per_b((1, dec_seq, D_ATT)),
        scratch_shapes=[pltpu.VMEM((q_rows, LANES), F32), pltpu.VMEM((q_rows, LANES), BF16),
                        pltpu.VMEM((LANES, LANES), F32), pltpu.VMEM((LANES, D_V), F32),
                        pltpu.VMEM((q_rows, PAGE_ROWS), F32), pltpu.VMEM((q_rows, PAGE_ROWS), F32),
                        pltpu.VMEM((q_rows, LANES), F32),
                        pltpu.VMEM((q_rows, LANES), F32), pltpu.VMEM((q_rows, LANES), F32),
                        pltpu.VMEM((q_rows, D_V), F32)],
    )
    return pl.pallas_call(
        functools.partial(_attn_sample_kernel, n_groups=n_groups, dec_seq=dec_seq),
        grid_spec=grid_spec,
        out_shape=jax.ShapeDtypeStruct((bsz, dec_seq, D_ATT), F32),
        compiler_params=_cparams(("arbitrary", "arbitrary")),
        name="attn_sample",
    )(page_table, q_s, kn_rows, vn_rows, tbl_rows, lam_vecs, subln_w.reshape(1, D_V),
      *([cache_k_rows] * pg), *([cache_v_rows] * pg))


ROW_CHUNKS = D_MODEL // LANES
assert ROW_CHUNKS == SUBLANES


def _store_row_tiles(ref, mat):
    rows = mat.shape[0]
    for c in range(ROW_CHUNKS):
        ref[pl.ds(c, rows, stride=ROW_CHUNKS), :] = mat[:, c * LANES:(c + 1) * LANES]


def _load_row_tiles(ref, first_row, rows, dtype=F32):
    return jnp.concatenate(
        [ref[pl.ds(first_row * ROW_CHUNKS + c, rows, stride=ROW_CHUNKS), :].astype(dtype) for c in range(ROW_CHUNKS)],
        axis=1)


def _split_bf16(a):
    hi = a.astype(BF16)
    lo = (a - hi.astype(F32)).astype(BF16)
    return hi, lo


def _outproj_kernel(x_ref, yc_ref, o_ref, gt_ref, sc_ref, sh_ref, g_ref, wo_ref, wr_ref, br_ref, cin_ref,
                    x1_ref, h2_ref, idx_ref, gate_ref, rank_ref, cnt_ref):
    tm = x_ref.shape[0]

    @pl.when(pl.program_id(0) == 0)
    def _():
        cnt_ref[...] = cin_ref[...]

    mix = (jnp.dot(yc_ref[...].astype(BF16), wo_ref[0:D_CONV, :], preferred_element_type=F32)
           + jnp.dot(o_ref[...].astype(BF16), wo_ref[D_CONV:, :], preferred_element_type=F32))
    x1 = x_ref[...] + gt_ref[...] * mix
    x1_ref[...] = x1
    ms = jnp.mean(x1 * x1, axis=1, keepdims=True)
    h2 = x1 * lax.rsqrt(ms + EPS) * g_ref[...]
    h2 = h2 * (1.0 + sc_ref[...]) + sh_ref[...]
    _store_row_tiles(h2_ref, h2)
    h_hi, h_lo = _split_bf16(h2)
    w_hi, w_lo = _split_bf16(wr_ref[...])
    logits = (jnp.dot(h_hi, w_hi, preferred_element_type=F32)
              + jnp.dot(h_hi, w_lo, preferred_element_type=F32)
              + jnp.dot(h_lo, w_hi, preferred_element_type=F32)) + br_ref[...]
    lane = lax.broadcasted_iota(I32, (tm, LANES), 1)
    lane_f = lane.astype(F32)
    vals, ids = [], []
    cur = logits
    for _ in range(TOP_K):
        mx = jnp.max(cur, axis=1, keepdims=True)
        ik = jnp.min(jnp.where(cur == mx, lane_f, float(LANES)), axis=1, keepdims=True)
        vals.append(mx)
        ids.append(ik)
        cur = jnp.where(lane_f == ik, -jnp.inf, cur)
    es = [jnp.exp(v - vals[0]) for v in vals]
    denom = functools.reduce(lambda a, c: a + c, es)
    sel = jnp.zeros((tm, LANES), F32)
    idx_out = jnp.zeros((tm, LANES), F32)
    gate_out = jnp.zeros((tm, LANES), F32)
    for k in range(TOP_K):
        sel = sel + jnp.where(lane_f == ids[k], 1.0, 0.0)
        idx_out = jnp.where(lane == k, ids[k], idx_out)
        gate_out = jnp.where(lane == k, es[k] / denom, gate_out)
    r = lax.broadcasted_iota(I32, (tm, tm), 0)
    c = lax.broadcasted_iota(I32, (tm, tm), 1)
    lower = jnp.where(r > c, 1.0, 0.0).astype(BF16)
    before = jnp.dot(lower, sel.astype(BF16), preferred_element_type=F32) + cnt_ref[...]
    rank_out = jnp.zeros((tm, LANES), F32)
    for k in range(TOP_K):
        rk = jnp.sum(jnp.where(lane_f == ids[k], before, 0.0), axis=1, keepdims=True)
        rank_out = jnp.where(lane == k, rk, rank_out)
    cnt_ref[...] = cnt_ref[...] + jnp.sum(sel, axis=0, keepdims=True)
    idx_ref[...] = idx_out.astype(I32)
    gate_ref[...] = gate_out
    rank_ref[...] = rank_out.astype(I32)


def _outproj(x, yc, o, gt, sc, sh, g_ffn, w_out_bf, w_router_pad, b_router_pad, cnt_in, *, tm):
    n = x.shape[0]
    per_row = gt.shape[0] != 1
    mod_spec = pl.BlockSpec((tm, D_MODEL), lambda i: (i, 0)) if per_row else pl.BlockSpec((1, D_MODEL), lambda i: (0, 0))
    const = lambda shape: pl.BlockSpec(shape, lambda i: (0, 0))
    rows = lambda w: pl.BlockSpec((tm, w), lambda i: (i, 0))
    return pl.pallas_call(
        _outproj_kernel,
        grid=(n // tm,),
        in_specs=[rows(D_MODEL), rows(D_CONV), rows(D_ATT), mod_spec, mod_spec, mod_spec, const((1, D_MODEL)),
                  const((D_MODEL, D_MODEL)), const((D_MODEL, LANES)), const((1, LANES)), const((1, LANES))],
        out_specs=[rows(D_MODEL), pl.BlockSpec((tm * ROW_CHUNKS, LANES), lambda i: (i, 0)),
                   rows(LANES), rows(LANES), rows(LANES), const((1, LANES))],
        out_shape=[jax.ShapeDtypeStruct((n, D_MODEL), F32),
                   jax.ShapeDtypeStruct((n * ROW_CHUNKS, LANES), F32),
                   jax.ShapeDtypeStruct((n, LANES), I32),
                   jax.ShapeDtypeStruct((n, LANES), F32),
                   jax.ShapeDtypeStruct((n, LANES), I32),
                   jax.ShapeDtypeStruct((1, LANES), F32)],
        compiler_params=_cparams(("arbitrary",)),
        name="outproj",
    )(x, yc, o, gt, sc, sh, g_ffn.reshape(1, D_MODEL), w_out_bf, w_router_pad, b_router_pad, cnt_in)


def _push_kernel(pend_ref, cnt_ref, dest_hbm, hp_ref, hs_ref, buf_out, idx0, idx1, zero_ref, isem, rsem, zsem,
                 *, n_steps, n_p_steps, n_blocks):
    i = pl.program_id(0)
    tm = hp_ref.shape[0]
    idx_refs = (idx0, idx1)

    @pl.when(i == 0)
    def _():
        zero_ref[...] = jnp.zeros(zero_ref.shape, F32)
        blk_shift = _log2(MOE_BLOCK)
        n_used = pend_ref[N_EXPERTS - 1] >> blk_shift

        def zero_copy(block):
            start = pl.multiple_of(block * MOE_BLOCK, MOE_BLOCK)
            return pltpu.make_async_copy(zero_ref, buf_out.at[pl.ds(start, MOE_BLOCK)], zsem)

        def targets(e):
            return ((cnt_ref[e] > 0, (pend_ref[e] >> blk_shift) - 1), (n_used + e < n_blocks, n_used + e))

        for wait in (False, True):
            for e in range(N_EXPERTS):
                for cond, block in targets(e):
                    @pl.when(cond)
                    def _():
                        if wait:
                            zero_copy(block).wait()
                        else:
                            zero_copy(block).start()

    def idx_copy(step, sl):
        return pltpu.make_async_copy(dest_hbm.at[step], idx_refs[sl], isem.at[sl])

    @pl.when(i == 0)
    def _():
        idx_copy(0, 0).start()

    def push_tile(h_ref, sl):
        idx_copy(i, sl).wait()

        @pl.when(i + 1 < n_steps)
        def _():
            idx_copy(i + 1, 1 - sl).start()

        def row_copy(t, dst_row):
            return pltpu.make_async_copy(h_ref.at[t], buf_out.at[dst_row], rsem)

        def start_body(t, carry):
            for k in range(TOP_K):
                row_copy(t, idx_refs[sl][k * tm + t]).start(priority=k % 2)
            return carry

        lax.fori_loop(0, tm, start_body, 0, unroll=DMA_UNROLL)

        def wait_body(t, carry):
            for k in range(TOP_K):
                row_copy(t, 0).wait()
            return carry

        lax.fori_loop(0, tm, wait_body, 0, unroll=DMA_UNROLL)

    for sl in range(2):
        @pl.when(jnp.logical_and(i < n_p_steps, (i & 1) == sl))
        def _():
            push_tile(hp_ref, sl)

        @pl.when(jnp.logical_and(i >= n_p_steps, (i & 1) == sl))
        def _():
            push_tile(hs_ref, sl)


def _moe_push(h2_p, h2_s, dest_tiles, pend, counts, *, n_blocks):
    tm = TM_ROW
    n_p_steps = h2_p.shape[0] // tm
    n_steps = n_p_steps + h2_s.shape[0] // tm
    assert dest_tiles.shape[0] == n_steps
    any_spec = pl.BlockSpec(memory_space=pl.ANY)
    tile = (tm, ROW_CHUNKS, LANES)
    grid_spec = pltpu.PrefetchScalarGridSpec(
        num_scalar_prefetch=2,
        grid=(n_steps,),
        in_specs=[any_spec,
                  pl.BlockSpec(tile, lambda i, pe, cn: (jnp.minimum(i, n_p_steps - 1), 0, 0)),
                  pl.BlockSpec(tile, lambda i, pe, cn: (jnp.maximum(i - n_p_steps, 0), 0, 0))],
        out_specs=any_spec,
        scratch_shapes=[pltpu.SMEM((TOP_K * tm,), I32), pltpu.SMEM((TOP_K * tm,), I32),
                        pltpu.VMEM((MOE_BLOCK, ROW_CHUNKS, LANES), F32),
                        pltpu.SemaphoreType.DMA((2,)), pltpu.SemaphoreType.DMA(()), pltpu.SemaphoreType.DMA(())],
    )
    return pl.pallas_call(
        functools.partial(_push_kernel, n_steps=n_steps, n_p_steps=n_p_steps, n_blocks=n_blocks),
        grid_spec=grid_spec,
        out_shape=jax.ShapeDtypeStruct((n_blocks * MOE_BLOCK, ROW_CHUNKS, LANES), F32),
        compiler_params=_cparams(("arbitrary",)),
        name="moe_push",
    )(pend, counts, dest_tiles, h2_p, h2_s)


def _expert_kernel(be_ref, nxt_ref, par_ref, nu_ref, x_ref, wg_hbm, bg_ref, wu_hbm, bu_ref, wd_hbm, bd_ref, o_ref,
                   w32, wg_bf, wu_bf, wd_bf, wsem):
    i = pl.program_id(0)
    n_used = nu_ref[0]
    prev = be_ref[jnp.maximum(i - 1, 0)]
    new_expert = jnp.logical_or(i == 0, be_ref[i] != prev)
    w_hbm = (wg_hbm, wu_hbm, wd_hbm)

    def weight_copies(e, sl):
        return [pltpu.make_async_copy(w_hbm[m].at[e], w32.at[sl, m], wsem.at[sl, m]) for m in range(3)]

    @pl.when(i == 0)
    def _():
        for c in weight_copies(be_ref[0], par_ref[0]):
            c.start()

    @pl.when(jnp.logical_and(i < n_used, new_expert))
    def _():
        sl = par_ref[i]
        for c in weight_copies(be_ref[i], sl):
            c.wait()
        wg_bf[...] = w32[sl, 0].astype(BF16)
        wu_bf[...] = w32[sl, 1].astype(BF16)
        wd_bf[...] = w32[sl, 2].astype(BF16)

        @pl.when(nxt_ref[i] >= 0)
        def _():
            for c in weight_copies(nxt_ref[i], 1 - sl):
                c.start()

    @pl.when(i < n_used)
    def _():
        x = _load_row_tiles(x_ref, 0, MOE_BLOCK, BF16)
        g = jnp.dot(x, wg_bf[...], preferred_element_type=F32) + bg_ref[0]
        u = jnp.dot(x, wu_bf[...], preferred_element_type=F32) + bu_ref[0]
        g = jnp.minimum(g, SWIGLU_LIMIT)
        u = jnp.clip(u, -SWIGLU_LIMIT, SWIGLU_LIMIT)
        a = g * jax.nn.sigmoid(SWIGLU_ALPHA * g) * (u + 1.0)
        _store_row_tiles(o_ref, jnp.dot(a.astype(BF16), wd_bf[...], preferred_element_type=F32) + bd_ref[0])

    @pl.when(i >= n_used)
    def _():
        o_ref[...] = jnp.zeros(o_ref.shape, F32)


def _moe_expert(buf, blk_e, n_used, pend, w_gate, b_gate, w_up, b_up, w_down, b_down):
    rows = buf.shape[0] // ROW_CHUNKS
    n_blocks = rows // MOE_BLOCK
    d_ff = w_gate.shape[2]
    assert d_ff == D_MODEL

    def blk(i, be, nx, pa, nu):
        return jnp.minimum(i, nu[0] - 1)

    xspec = pl.BlockSpec((MOE_BLOCK * ROW_CHUNKS, LANES), lambda i, be, nx, pa, nu: (blk(i, be, nx, pa, nu), 0))
    bspec = pl.BlockSpec((1, 1, D_MODEL), lambda i, be, nx, pa, nu: (be[blk(i, be, nx, pa, nu)], 0, 0))
    any_spec = pl.BlockSpec(memory_space=pl.ANY)
    grid_spec = pltpu.PrefetchScalarGridSpec(
        num_scalar_prefetch=4,
        grid=(n_blocks,),
        in_specs=[xspec, any_spec, bspec, any_spec, bspec, any_spec, bspec],
        out_specs=pl.BlockSpec((MOE_BLOCK * ROW_CHUNKS, LANES), lambda i, be, nx, pa, nu: (i, 0)),
        scratch_shapes=[pltpu.VMEM((2, 3, D_MODEL, D_MODEL), F32),
                        pltpu.VMEM((D_MODEL, d_ff), BF16), pltpu.VMEM((D_MODEL, d_ff), BF16),
                        pltpu.VMEM((d_ff, D_MODEL), BF16), pltpu.SemaphoreType.DMA((2, 3))],
    )
    first = jnp.concatenate([jnp.ones((1,), bool), blk_e[1:] != blk_e[:-1]])
    parity = ((jnp.cumsum(first.astype(I32)) - 1) & 1).astype(I32)
    next_first = pend[blk_e] // MOE_BLOCK
    nxt_e = jnp.where(next_first < n_used[0], blk_e[jnp.minimum(next_first, n_blocks - 1)], -1).astype(I32)
    return pl.pallas_call(
        _expert_kernel,
        grid_spec=grid_spec,
        out_shape=jax.ShapeDtypeStruct((rows * ROW_CHUNKS, LANES), F32),
        compiler_params=_cparams(("arbitrary",)),
        name="moe_expert",
    )(blk_e, nxt_e, parity, n_used, buf, w_gate, b_gate.reshape(N_EXPERTS, 1, d_ff), w_up,
      b_up.reshape(N_EXPERTS, 1, d_ff), w_down, b_down.reshape(N_EXPERTS, 1, D_MODEL))


def _combine_kernel(dest_hbm, out_hbm, x1_ref, gate_ref, gt_ref, y_ref,
                    idx0, idx1, rows0, rows1, isem, rsem, *, n_steps):
    i = pl.program_id(0)
    tm = x1_ref.shape[0]
    n_rows = TOP_K * tm
    idx_refs = (idx0, idx1)
    rows_refs = (rows0, rows1)

    def idx_copy(step, sl):
        return pltpu.make_async_copy(dest_hbm.at[step], idx_refs[sl], isem.at[sl])

    def row_copy(src_row, sl, r):
        dst = rows_refs[sl].at[pl.ds(pl.multiple_of(r * ROW_CHUNKS, ROW_CHUNKS), ROW_CHUNKS), :]
        return pltpu.make_async_copy(out_hbm.at[src_row], dst, rsem.at[sl])

    def issue_rows(sl):
        def body(r2, carry):
            for par in range(2):
                r = 2 * r2 + par
                row_copy(idx_refs[sl][r], sl, r).start(priority=par)
            return carry
        lax.fori_loop(0, n_rows // 2, body, 0, unroll=DMA_UNROLL)

    def wait_rows(sl):
        def body(r, carry):
            row_copy(0, sl, r).wait()
            return carry
        lax.fori_loop(0, n_rows, body, 0, unroll=2 * DMA_UNROLL)

    @pl.when(i == 0)
    def _():
        idx_copy(0, 0).start()
        idx_copy(0, 0).wait()
        issue_rows(0)
        if n_steps > 1:
            idx_copy(1, 1).start()

    for sl in range(2):
        @pl.when((i & 1) == sl)
        def _():
            @pl.when(i + 1 < n_steps)
            def _():
                idx_copy(i + 1, 1 - sl).wait()
                issue_rows(1 - sl)

            @pl.when(i + 2 < n_steps)
            def _():
                idx_copy(i + 2, sl).start()

            wait_rows(sl)
            gates = gate_ref[...]
            y = jnp.zeros((tm, D_MODEL), F32)
            for k in range(TOP_K):
                y = y + gates[:, k:k + 1] * _load_row_tiles(rows_refs[sl], k * tm, tm)
            y_ref[...] = x1_ref[...] + gt_ref[...] * y


def _moe_combine(out_rows, dest_tiles, x1, gates, gt):
    n = x1.shape[0]
    tm = TM_ROW
    n_steps = n // tm
    per_row = gt.shape[0] != 1
    mod_spec = pl.BlockSpec((tm, D_MODEL), lambda i: (i, 0)) if per_row else pl.BlockSpec((1, D_MODEL), lambda i: (0, 0))
    return pl.pallas_call(
        functools.partial(_combine_kernel, n_steps=n_steps),
        grid=(n_steps,),
        in_specs=[pl.BlockSpec(memory_space=pl.ANY), pl.BlockSpec(memory_space=pl.ANY),
                  pl.BlockSpec((tm, D_MODEL), lambda i: (i, 0)),
                  pl.BlockSpec((tm, LANES), lambda i: (i, 0)),
                  mod_spec],
        out_specs=pl.BlockSpec((tm, D_MODEL), lambda i: (i, 0)),
        out_shape=jax.ShapeDtypeStruct((n, D_MODEL), F32),
        scratch_shapes=[pltpu.SMEM((TOP_K * tm,), I32), pltpu.SMEM((TOP_K * tm,), I32),
                        pltpu.VMEM((TOP_K * tm * ROW_CHUNKS, LANES), F32),
                        pltpu.VMEM((TOP_K * tm * ROW_CHUNKS, LANES), F32),
                        pltpu.SemaphoreType.DMA((2,)), pltpu.SemaphoreType.DMA((2,))],
        compiler_params=_cparams(("arbitrary",)),
        name="moe_combine",
    )(dest_tiles, out_rows, x1, gates, gt)


def _dest_tiles(dest, tm):
    n = dest.shape[0]
    return dest.reshape(n // tm, tm, TOP_K).transpose(0, 2, 1).reshape(n // tm, TOP_K * tm)


def kernel(x_prompt, x_sample, cache_k, cache_v, state_conv, page_table, c_prompt, c_sample, rel_table, w_ada, b_ada, g_mix, w_in, conv_w, q_norm_w, k_norm_w, lam_q1, lam_k1, lam_q2, lam_k2, subln_w, w_out, g_ffn, w_router, b_router, w_gate, b_gate, w_up, b_up, w_down, b_down):
    assert w_ada.shape[0] == 1, "single-layer trunk"
    bp, t_p, _ = x_prompt.shape
    bs, t_s, _ = x_sample.shape
    assert bp == 1
    n_s = bs * t_s
    n_all = t_p + n_s
    l = 0

    n_c = bp + bs
    c_rows = -(-n_c // SUBLANES) * SUBLANES
    c_all = jnp.concatenate([c_prompt, c_sample, jnp.zeros((c_rows - n_c, D_MODEL), F32)], axis=0)
    mod = _adaln(c_all, w_ada[l], b_ada[l]).reshape(c_rows, 6, D_MODEL)
    mod_p = [mod[0:1, j, :] for j in range(6)]
    per_token = jnp.broadcast_to(mod[1:1 + bs, None, :, :], (bs, t_s, 6, D_MODEL)).reshape(n_s, 6, D_MODEL)
    mod_s = [per_token[:, j, :] for j in range(6)]

    w_in_bf = w_in[l].astype(BF16)
    w_out_bf = w_out[l].astype(BF16)
    lam_vecs = jnp.stack([lam_q1[l], lam_k1[l], lam_q2[l], lam_k2[l]])

    xp = x_prompt.reshape(t_p, D_MODEL)
    xs = x_sample.reshape(n_s, D_MODEL)
    yc_p, q_p, k_p, v_p, kb_p, vt_p, u_tail = _inproj(
        xp, mod_p[1], mod_p[0], g_mix[l], w_in_bf, conv_w[l], q_norm_w[l], k_norm_w[l], None, None,
        tm=TM_IN, seq_len=None)
    st = state_conv[l]
    prev2 = jnp.broadcast_to(st[:, None, 0, :], (bs, t_s, D_CONV)).reshape(n_s, D_CONV)
    prev1 = jnp.broadcast_to(st[:, None, 1, :], (bs, t_s, D_CONV)).reshape(n_s, D_CONV)
    yc_s, q_s, k_s, v_s, _, _, u_s = _inproj(
        xs, mod_s[1], mod_s[0], g_mix[l], w_in_bf, conv_w[l], q_norm_w[l], k_norm_w[l], prev2, prev1,
        tm=n_s, seq_len=t_s)

    o_p = _attn_prompt(q_p, kb_p, vt_p, rel_table, lam_vecs, subln_w[l])
    n_phys = cache_k.shape[1]
    o_s = _attn_sample(q_s.astype(F32).reshape(bs, t_s, D_QKH),
                       k_s.reshape(bs, t_s * N_HEADS, 2 * D_QK), v_s.reshape(bs, t_s * N_HEADS, D_V),
                       cache_k[l].reshape(n_phys * PAGE_ROWS, 2 * D_QK), cache_v[l].reshape(n_phys * PAGE_ROWS, D_V),
                       page_table, rel_table, lam_vecs, subln_w[l]).reshape(n_s, D_ATT)

    w_router_pad = jnp.pad(w_router[l], ((0, 0), (0, LANES - N_EXPERTS)))
    b_router_pad = jnp.concatenate([b_router[l], jnp.full((LANES - N_EXPERTS,), NEG_INF, F32)]).reshape(1, LANES)
    cnt0 = jnp.zeros((1, LANES), F32)
    x1_p, h2_p, idx_p, gate_p, rank_p, cnt1 = _outproj(
        xp, yc_p, o_p, mod_p[2], mod_p[4], mod_p[3], g_ffn[l], w_out_bf, w_router_pad, b_router_pad, cnt0, tm=TM_OUT)
    x1_s, h2_s, idx_s, gate_s, rank_s, cnt2 = _outproj(
        xs, yc_s, o_s, mod_s[2], mod_s[4], mod_s[3], g_ffn[l], w_out_bf, w_router_pad, b_router_pad, cnt1, tm=TM_OUT)

    counts = cnt2[0, :N_EXPERTS].astype(I32)
    padded = (counts + MOE_BLOCK - 1) // MOE_BLOCK * MOE_BLOCK
    pend = jnp.cumsum(padded).astype(I32)
    pstart = pend - padded
    n_blocks = (n_all * TOP_K) // MOE_BLOCK + N_EXPERTS
    def slots(idx, rank):
        is_e = idx[:, :TOP_K, None] == jnp.arange(N_EXPERTS, dtype=I32)
        return jnp.sum(jnp.where(is_e, pstart, 0), axis=-1) + rank[:, :TOP_K]

    dest_p = slots(idx_p, rank_p)
    dest_s = slots(idx_s, rank_s)
    block_start = jnp.arange(n_blocks, dtype=I32) * MOE_BLOCK
    blk_e = jnp.minimum(jnp.sum(pend[None, :] <= block_start[:, None], axis=1), N_EXPERTS - 1).astype(I32)
    n_used = (pend[-1] // MOE_BLOCK).astype(I32).reshape(1)
    tiles_p = _dest_tiles(dest_p, TM_ROW)
    tiles_s = _dest_tiles(dest_s, TM_ROW)

    as_tiles = lambda a: a.reshape(a.shape[0] // ROW_CHUNKS, ROW_CHUNKS, LANES)
    buf = _moe_push(as_tiles(h2_p), as_tiles(h2_s), jnp.concatenate([tiles_p, tiles_s], axis=0), pend, counts,
                    n_blocks=n_blocks)
    out_rows = as_tiles(_moe_expert(buf.reshape(n_blocks * MOE_BLOCK * ROW_CHUNKS, LANES), blk_e, n_used, pend,
                                    w_gate[l], b_gate[l], w_up[l], b_up[l], w_down[l], b_down[l]))
    y_p = _moe_combine(out_rows, tiles_p, x1_p, gate_p, mod_p[5])
    y_s = _moe_combine(out_rows, tiles_s, x1_s, gate_s, mod_s[5])

    k_prompt = k_p.reshape(1, bp, t_p, N_HEADS, 2 * D_QK)
    v_prompt = v_p.reshape(1, bp, t_p, N_HEADS, D_V)
    conv_prompt = u_tail[SUBLANES - (CONV_W - 1):, :].reshape(1, bp, CONV_W - 1, D_CONV)
    k_sample = k_s.reshape(1, bs, t_s, N_HEADS, 2 * D_QK)
    v_sample = v_s.reshape(1, bs, t_s, N_HEADS, D_V)
    conv_sample = u_s.reshape(bs, t_s, D_CONV)[:, t_s - (CONV_W - 1):, :].reshape(1, bs, CONV_W - 1, D_CONV)
    return (y_p.reshape(bp, t_p, D_MODEL), y_s.reshape(bs, t_s, D_MODEL),
            k_prompt, v_prompt, conv_prompt, k_sample, v_sample, conv_sample)
```

```python
import functools
import math

import numpy as np
import jax
import jax.numpy as jnp
from jax import lax
from jax.experimental import pallas as pl
from jax.experimental.pallas import tpu as pltpu

F32 = jnp.float32
BF16 = jnp.bfloat16
I32 = jnp.int32

D_MODEL = 1024
D_CONV = 512
CONV_W = 3
N_HEADS = 4
D_QK = 64
D_V = 128
D_ATT = N_HEADS * D_V
D_QKH = N_HEADS * 2 * D_QK
D_IN = 3 * D_CONV + 2 * D_QKH + D_ATT
N_BUCKETS = 32
MAX_EXACT = 16
MAX_DIST = 128
N_EXPERTS = 32
TOP_K = 4
SWIGLU_LIMIT = 7.0
SWIGLU_ALPHA = 1.702
PAGE_SIZE = 128
EPS = 1e-6
NEG_INF = -1e30
LAM_INIT = 0.8 - 0.6 * math.exp(-0.3 * 0)
LOG2E = math.log2(math.e)
Q_SCALE = D_QK ** -0.5 * LOG2E
SCORE_HEADROOM = 60.0

LANES = 128
SUBLANES = 8
VMEM_LIMIT = 56 * 1024 * 1024

TM_IN = 512
TQ = 512
TK = 512
FAR_GROUP = 4
PAGES_PER_STEP = 32
TM_OUT = 512
MOE_BLOCK = 512
TM_ROW = 256
DMA_UNROLL = 4
assert TK == TM_IN


def _bucket_thresholds():
    n = np.arange(0, 4 * MAX_DIST)
    nf = np.maximum(n, 1).astype(np.float32)
    val = np.log(nf / np.float32(MAX_EXACT)) / np.float32(math.log(MAX_DIST / MAX_EXACT)) * np.float32(N_BUCKETS - MAX_EXACT)
    large = np.minimum(MAX_EXACT + val.astype(np.int32), N_BUCKETS - 1)
    bucket = np.where(n < MAX_EXACT, n, large)
    assert np.all(np.diff(bucket) >= 0) and bucket[-1] == N_BUCKETS - 1
    thr = [int(np.argmax(bucket >= b)) for b in range(N_BUCKETS)]
    return thr


BUCKET_THR = _bucket_thresholds()
FAR_DIST = BUCKET_THR[N_BUCKETS - 1]


def _log2(n):
    assert n > 0 and n & (n - 1) == 0
    return n.bit_length() - 1


def _cparams(sem):
    return pltpu.CompilerParams(dimension_semantics=sem, vmem_limit_bytes=VMEM_LIMIT)


def _adaln_kernel(c_ref, w_ref, b_ref, o_ref):
    c = c_ref[...]
    s = c * jax.nn.sigmoid(c)
    o_ref[...] = jnp.dot(s, w_ref[...], preferred_element_type=F32,
                         precision=lax.Precision.HIGHEST) + b_ref[...]


def _adaln(c_all, w_ada, b_ada):
    rows = c_all.shape[0]
    n = w_ada.shape[1]
    bn = D_MODEL
    return pl.pallas_call(
        _adaln_kernel,
        grid=(n // bn,),
        in_specs=[pl.BlockSpec((rows, D_MODEL), lambda j: (0, 0)),
                  pl.BlockSpec((D_MODEL, bn), lambda j: (0, j)),
                  pl.BlockSpec((1, bn), lambda j: (0, j))],
        out_specs=pl.BlockSpec((rows, bn), lambda j: (0, j)),
        out_shape=jax.ShapeDtypeStruct((rows, n), F32),
        compiler_params=_cparams(("arbitrary",)),
        name="adaln",
    )(c_all, w_ada, b_ada.reshape(1, n))


def _group_rms(xb, w_row):
    lane = lax.broadcasted_iota(I32, xb.shape, 1)
    lo_half = lane < D_QK
    sq = xb * xb
    lo = jnp.sum(jnp.where(lo_half, sq, 0.0), axis=1, keepdims=True)
    hi = jnp.sum(jnp.where(lo_half, 0.0, sq), axis=1, keepdims=True)
    ms = jnp.where(lo_half, lo, hi) * (1.0 / D_QK)
    return xb * lax.rsqrt(ms + EPS) * w_row


def _inproj_kernel(x_ref, sc_ref, sh_ref, g_ref, w_ref, cw_ref, qw_ref, kw_ref, p2_ref, p1_ref,
                   yc_ref, q_ref, k_ref, v_ref, kb_ref, vt_ref, u_ref, carry_ref, z0_ref, z1_ref, *, seq_len):
    step = pl.program_id(0)

    @pl.when(step == 0)
    def _():
        z1_ref[...] = jnp.zeros(z1_ref.shape, F32)
        carry_ref[...] = jnp.zeros_like(carry_ref)

    for par, (z_write, z_read) in enumerate(((z0_ref, z1_ref), (z1_ref, z0_ref))):
        @pl.when((step & 1) == par)
        def _():
            _inproj_tile(x_ref, sc_ref, sh_ref, g_ref, w_ref, cw_ref, qw_ref, kw_ref, p2_ref, p1_ref,
                         yc_ref, q_ref, k_ref, v_ref, kb_ref, vt_ref, u_ref, carry_ref, z_write, z_read,
                         seq_len=seq_len)


def _inproj_tile(x_ref, sc_ref, sh_ref, g_ref, w_ref, cw_ref, qw_ref, kw_ref, p2_ref, p1_ref,
                 yc_ref, q_ref, k_ref, v_ref, kb_ref, vt_ref, u_ref, carry_ref, z_write, z, *, seq_len):
    tm = x_ref.shape[0]
    x = x_ref[...]
    ms = jnp.mean(x * x, axis=1, keepdims=True)
    h = x * lax.rsqrt(ms + EPS) * g_ref[...]
    h = h * (1.0 + sc_ref[...]) + sh_ref[...]
    z_write[...] = jnp.dot(h.astype(BF16), w_ref[...], preferred_element_type=F32)
    gb = z[:, 0:D_CONV]
    u = z[:, D_CONV:2 * D_CONV] * z[:, 2 * D_CONV:3 * D_CONV]
    row = lax.broadcasted_iota(I32, (tm, D_CONV), 0)
    u1 = pltpu.roll(u, 1, 0)
    u2 = pltpu.roll(u, 2, 0)
    if seq_len is None:
        prev2 = carry_ref[SUBLANES - 2:SUBLANES - 1, :]
        prev1 = carry_ref[SUBLANES - 1:SUBLANES, :]
        u1 = jnp.where(row == 0, prev1, u1)
        u2 = jnp.where(row == 0, prev2, jnp.where(row == 1, prev1, u2))
        carry_ref[...] = u[tm - SUBLANES:tm, :]
        u_ref[...] = u[tm - SUBLANES:tm, :]
    else:
        pos = row & (seq_len - 1)
        u1 = jnp.where(pos == 0, p1_ref[...], u1)
        u2 = jnp.where(pos == 0, p2_ref[...], jnp.where(pos == 1, p1_ref[...], u2))
        u_ref[...] = u
    cw = cw_ref[...]
    yc = gb * (cw[0:1, :] * u2 + cw[1:2, :] * u1 + cw[2:3, :] * u)
    yc_ref[...] = yc.astype(yc_ref.dtype)
    o1 = 3 * D_CONV
    for hh in range(N_HEADS):
        c0 = o1 + hh * LANES
        qn = _group_rms(z[:, c0:c0 + LANES], qw_ref[...])
        q_ref[:, hh * LANES:(hh + 1) * LANES] = (qn * Q_SCALE).astype(q_ref.dtype)
        c1 = o1 + D_QKH + hh * LANES
        kn = _group_rms(z[:, c1:c1 + LANES], kw_ref[...])
        k_ref[pl.ds(hh, tm, stride=N_HEADS), :] = kn
        kb_ref[:, hh * LANES:(hh + 1) * LANES] = kn.astype(BF16)
    v = z[:, o1 + 2 * D_QKH:]
    for hh in range(N_HEADS):
        vh = v[:, hh * D_V:(hh + 1) * D_V]
        v_ref[pl.ds(hh, tm, stride=N_HEADS), :] = vh
        vt_ref[0, hh * D_V:(hh + 1) * D_V, :] = vh.T.astype(BF16)


def _inproj(x, sc, sh, g_mix, w_in_bf, conv_w, qw, kw, prev2, prev1, *, tm, seq_len):
    n = x.shape[0]
    n_tiles = n // tm
    grid = (n_tiles + 1,)
    proj = lambda i: jnp.minimum(i, n_tiles - 1)
    epi = lambda i: jnp.maximum(i - 1, 0)
    per_row = sc.shape[0] != 1
    mod_spec = (pl.BlockSpec((tm, D_MODEL), lambda i: (proj(i), 0)) if per_row
                else pl.BlockSpec((1, D_MODEL), lambda i: (0, 0)))
    if seq_len is None:
        prev2 = jnp.zeros((SUBLANES, D_CONV), F32)
        prev1 = prev2
        prev_spec = pl.BlockSpec((SUBLANES, D_CONV), lambda i: (0, 0))
        u_shape = jax.ShapeDtypeStruct((SUBLANES, D_CONV), F32)
        u_spec = pl.BlockSpec((SUBLANES, D_CONV), lambda i: (0, 0))
    else:
        assert seq_len & (seq_len - 1) == 0
        prev_spec = pl.BlockSpec((tm, D_CONV), lambda i: (epi(i), 0))
        u_shape = jax.ShapeDtypeStruct((n, D_CONV), F32)
        u_spec = pl.BlockSpec((tm, D_CONV), lambda i: (epi(i), 0))
    const = lambda shape: pl.BlockSpec(shape, lambda i: (0, 0))
    rows = lambda w: pl.BlockSpec((tm, w), lambda i: (epi(i), 0))
    head_rows = pl.BlockSpec((tm * N_HEADS, LANES), lambda i: (epi(i), 0))
    qw2 = jnp.tile(qw.reshape(1, D_QK), (1, 2))
    kw2 = jnp.tile(kw.reshape(1, D_QK), (1, 2))
    return pl.pallas_call(
        functools.partial(_inproj_kernel, seq_len=seq_len),
        grid=grid,
        in_specs=[pl.BlockSpec((tm, D_MODEL), lambda i: (proj(i), 0)), mod_spec, mod_spec, const((1, D_MODEL)),
                  const((D_MODEL, D_IN)), const((CONV_W, D_CONV)), const((1, LANES)), const((1, LANES)),
                  prev_spec, prev_spec],
        out_specs=[rows(D_CONV), rows(D_QKH), head_rows, head_rows, rows(D_QKH),
                   pl.BlockSpec((1, D_ATT, tm), lambda i: (epi(i), 0, 0)), u_spec],
        out_shape=[jax.ShapeDtypeStruct((n, D_CONV), BF16),
                   jax.ShapeDtypeStruct((n, D_QKH), BF16),
                   jax.ShapeDtypeStruct((n * N_HEADS, 2 * D_QK), F32),
                   jax.ShapeDtypeStruct((n * N_HEADS, D_V), F32),
                   jax.ShapeDtypeStruct((n, D_QKH), BF16),
                   jax.ShapeDtypeStruct((n // tm, D_ATT, tm), BF16),
                   u_shape],
        scratch_shapes=[pltpu.VMEM((SUBLANES, D_CONV), F32), pltpu.VMEM((tm, D_IN), F32), pltpu.VMEM((tm, D_IN), F32)],
        compiler_params=_cparams(("arbitrary",)),
        name="inproj_seq" if seq_len is None else "inproj_batch",
    )(x, sc, sh, g_mix.reshape(1, D_MODEL), w_in_bf, conv_w, qw2, kw2, prev2, prev1)


def _bias_from_dist(dist, table_of_bucket):
    b = jnp.zeros(dist.shape, F32) + table_of_bucket(0)
    for bk in range(1, N_BUCKETS):
        b = jnp.where(dist >= BUCKET_THR[bk], table_of_bucket(bk), b)
    return (b - table_of_bucket(N_BUCKETS - 1)) * LOG2E


def _lambda_value(lam_ref):
    lq1, lk1, lq2, lk2 = (lam_ref[i:i + 1, :] for i in range(4))
    return (jnp.exp(jnp.sum(lq1 * lk1, axis=1, keepdims=True))
            - jnp.exp(jnp.sum(lq2 * lk2, axis=1, keepdims=True)) + LAM_INIT)


def _sub_norm(o, sw_row):
    ms = jnp.mean(o * o, axis=1, keepdims=True)
    return o * lax.rsqrt(ms + EPS) * sw_row * (1.0 - LAM_INIT)


def _attn_prompt_kernel(tbl_ref, q_ref, k_ref, vt_ref, lam_ref, sw_ref, o_ref,
                        bdiag_ref, bprev_ref, m_ref, l_ref, acc_ref):
    h = pl.program_id(0)
    i = pl.program_id(1)
    tq = q_ref.shape[0]
    tk = vt_ref.shape[2]
    assert tq == tk and tk >= FAR_DIST

    @pl.when(i == 0)
    def _():
        key = lax.broadcasted_iota(I32, (tk, tq), 0)
        qry = lax.broadcasted_iota(I32, (tk, tq), 1)
        d0 = qry - key
        tb = lambda b: tbl_ref[b, h]
        bdiag_ref[...] = jnp.where(d0 >= 0, _bias_from_dist(jnp.maximum(d0, 0), tb), NEG_INF)
        bprev_ref[...] = _bias_from_dist(d0 + tk, tb)

    m_ref[...] = jnp.full(m_ref.shape, -jnp.inf, F32)
    l_ref[...] = jnp.zeros(l_ref.shape, F32)
    acc_ref[...] = jnp.zeros(acc_ref.shape, F32)

    q = q_ref[...]
    lane = lax.broadcasted_iota(I32, q.shape, 1)
    zero = jnp.zeros_like(q)
    q_maps = (jnp.where(lane < D_QK, q, zero), jnp.where(lane < D_QK, zero, q))

    def scores_t(kb, mi):
        return lax.dot_general(kb, q_maps[mi], (((1,), (1,)), ((), ())), preferred_element_type=F32)

    def key_block(j):
        return k_ref[pl.ds(pl.multiple_of(j * tk, tk), tk), :], vt_ref[j]

    def exact_step(j, bias_ref):
        kb, vtb = key_block(j)
        for mi in range(2):
            st = scores_t(kb, mi)
            if bias_ref is not None:
                st = st + bias_ref[...]
            m_prev = m_ref[mi]
            m_new = jnp.maximum(m_prev, jnp.max(st, axis=0, keepdims=True))
            alpha = jnp.exp2(m_prev - m_new)
            pt = jnp.exp2(st - m_new)
            l_ref[mi] = alpha * l_ref[mi] + jnp.sum(pt, axis=0, keepdims=True)
            acc_ref[mi] = alpha * acc_ref[mi] + jnp.dot(vtb, pt.astype(BF16), preferred_element_type=F32)
            m_ref[mi] = m_new

    def streamed_step(j, n_blk, bias_ref=None):
        kb = k_ref[pl.ds(pl.multiple_of(j * tk, tk), n_blk * tk), :]
        sts = [scores_t(kb, mi) for mi in range(2)]
        if bias_ref is not None:
            assert n_blk == 1
            sts = [st + bias_ref[...] for st in sts]
        parts = []
        worst = None
        for mi in range(2):
            ref_pt = m_ref[mi]
            pt = jnp.exp2(sts[mi] - ref_pt)
            blk_max = jnp.max(sts[mi], axis=0, keepdims=True)
            l_add = jnp.sum(pt, axis=0, keepdims=True)
            acc_add = functools.reduce(lambda a, b: a + b, [
                jnp.dot(vt_ref[j + u], pt[u * tk:(u + 1) * tk, :].astype(BF16), preferred_element_type=F32)
                for u in range(n_blk)])
            parts.append((ref_pt, blk_max, l_add, acc_add))
            excess = blk_max - ref_pt
            worst = excess if worst is None else jnp.maximum(worst, excess)
        in_range = jnp.max(worst) <= SCORE_HEADROOM

        @pl.when(in_range)
        def _():
            for mi, (ref_pt, blk_max, l_add, acc_add) in enumerate(parts):
                m_new = jnp.maximum(ref_pt, blk_max)
                alpha = jnp.exp2(ref_pt - m_new)
                l_ref[mi] = alpha * (l_ref[mi] + l_add)
                acc_ref[mi] = alpha * (acc_ref[mi] + acc_add)
                m_ref[mi] = m_new

        @pl.when(jnp.logical_not(in_range))
        def _():
            for u in range(n_blk):
                exact_step(j + u, bias_ref)

    exact_step(i, bdiag_ref)

    @pl.when(i > 0)
    def _():
        streamed_step(i - 1, 1, bprev_ref)

    n_far = jnp.maximum(i - 1, 0)

    group_shift = _log2(FAR_GROUP)

    def far_group(jj, carry):
        streamed_step(FAR_GROUP * jj, FAR_GROUP)
        return carry

    lax.fori_loop(0, n_far >> group_shift, far_group, 0)

    def far_single(j, carry):
        streamed_step(j, 1)
        return carry

    lax.fori_loop((n_far >> group_shift) << group_shift, n_far, far_single, 0)

    lam = _lambda_value(lam_ref)
    ot = acc_ref[0] / l_ref[0] - lam * (acc_ref[1] / l_ref[1])
    o_ref[...] = _sub_norm(ot.T, sw_ref[...]).astype(o_ref.dtype)


def _attn_prompt(q_bf, k_bf, vt_bf, rel_table, lam_vecs, subln_w):
    t = q_bf.shape[0]
    nk, _, tk = vt_bf.shape
    assert tk == TK and nk * tk == t
    nq = t // TQ
    grid_spec = pltpu.PrefetchScalarGridSpec(
        num_scalar_prefetch=0,
        grid=(N_HEADS, nq),
        in_specs=[pl.BlockSpec(memory_space=pltpu.SMEM),
                  pl.BlockSpec((TQ, LANES), lambda h, i: (i, h)),
                  pl.BlockSpec((t, LANES), lambda h, i: (0, h)),
                  pl.BlockSpec((nk, D_V, tk), lambda h, i: (0, h, 0)),
                  pl.BlockSpec((4, D_QK), lambda h, i: (0, 0)),
                  pl.BlockSpec((1, D_V), lambda h, i: (0, 0))],
        out_specs=pl.BlockSpec((TQ, D_V), lambda h, i: (i, h)),
        scratch_shapes=[pltpu.VMEM((TK, TQ), F32), pltpu.VMEM((TK, TQ), F32),
                        pltpu.VMEM((2, 1, TQ), F32), pltpu.VMEM((2, 1, TQ), F32),
                        pltpu.VMEM((2, D_V, TQ), F32)],
    )
    return pl.pallas_call(
        _attn_prompt_kernel,
        grid_spec=grid_spec,
        out_shape=jax.ShapeDtypeStruct((t, D_ATT), BF16),
        compiler_params=_cparams(("arbitrary", "arbitrary")),
        name="attn_prompt",
    )(rel_table, q_bf, k_bf, vt_bf, lam_vecs, subln_w.reshape(1, D_V))


PAGE_ROWS = PAGE_SIZE * N_HEADS


def _attn_sample_kernel(pt_ref, q_ref, kn_ref, vn_ref, tblr_ref, lam_ref, sw_ref, *rest, n_groups, dec_seq):
    pg = PAGES_PER_STEP
    k_refs = rest[:pg]
    v_refs = rest[pg:2 * pg]
    o_ref = rest[2 * pg]
    qf_ref, qb_ref, knp_ref, vnp_ref, bmask_ref, blast_ref, bnew_ref, m_ref, l_ref, acc_ref = rest[2 * pg + 1:]
    b = pl.program_id(0)
    g = pl.program_id(1)
    q_rows = N_HEADS * 2 * dec_seq
    new_rows = dec_seq * N_HEADS
    head_shift = _log2(2 * dec_seq)
    key_shift = _log2(N_HEADS)
    assert PAGE_SIZE + 1 >= FAR_DIST and new_rows <= LANES

    @pl.when(jnp.logical_and(b == 0, g == 0))
    def _():
        tb = lambda bk: tblr_ref[:, bk:bk + 1]
        r = lax.broadcasted_iota(I32, (q_rows, PAGE_ROWS), 0)
        c = lax.broadcasted_iota(I32, (q_rows, PAGE_ROWS), 1)
        same_head = (c & (N_HEADS - 1)) == (r >> head_shift)
        tok = r & (dec_seq - 1)
        key = c >> key_shift
        bmask_ref[...] = jnp.where(same_head, 0.0, NEG_INF)
        blast_ref[...] = jnp.where(same_head, _bias_from_dist(tok + PAGE_SIZE - key, tb), NEG_INF)
        r2 = lax.broadcasted_iota(I32, (q_rows, LANES), 0)
        c2 = lax.broadcasted_iota(I32, (q_rows, LANES), 1)
        d2 = (r2 & (dec_seq - 1)) - (c2 >> key_shift)
        ok2 = jnp.logical_and((c2 & (N_HEADS - 1)) == (r2 >> head_shift),
                              jnp.logical_and(d2 >= 0, c2 < new_rows))
        bnew_ref[...] = jnp.where(ok2, _bias_from_dist(jnp.maximum(d2, 0), tb), NEG_INF)

    @pl.when(g == 0)
    def _():
        q = q_ref[0]
        lane = lax.broadcasted_iota(I32, (dec_seq, LANES), 1)
        for hh in range(N_HEADS):
            qh = q[:, hh * LANES:(hh + 1) * LANES]
            r0 = hh * 2 * dec_seq
            qf_ref[r0:r0 + dec_seq, :] = jnp.where(lane < D_QK, qh, 0.0)
            qf_ref[r0 + dec_seq:r0 + 2 * dec_seq, :] = jnp.where(lane < D_QK, 0.0, qh)
        qb_ref[...] = qf_ref[...].astype(BF16)
        knp_ref[...] = jnp.zeros(knp_ref.shape, F32)
        vnp_ref[...] = jnp.zeros(vnp_ref.shape, F32)
        knp_ref[0:new_rows, :] = kn_ref[0]
        vnp_ref[0:new_rows, :] = vn_ref[0]
        m_ref[...] = jnp.full(m_ref.shape, -jnp.inf, F32)
        l_ref[...] = jnp.zeros(l_ref.shape, F32)
        acc_ref[...] = jnp.zeros(acc_ref.shape, F32)

    qb = qb_ref[...]
    is_last = g == n_groups - 1

    def scores(k_rows):
        return lax.dot_general(qb, k_rows.astype(BF16), (((1,), (1,)), ((), ())), preferred_element_type=F32)

    def update(s_list, v_list):
        m_prev = m_ref[...]
        m_cur = functools.reduce(jnp.maximum, [jnp.max(s, axis=1, keepdims=True) for s in s_list])
        m_new = jnp.maximum(m_prev, m_cur)
        alpha = jnp.exp2(m_prev - m_new)
        l_new = alpha * l_ref[...]
        pv = None
        for s, v_rows in zip(s_list, v_list):
            p = jnp.exp2(s - jnp.concatenate([m_new] * (s.shape[1] // LANES), axis=1))
            l_new = l_new + jnp.sum(p, axis=1, keepdims=True)
            d = jnp.dot(p.astype(BF16), v_rows.astype(BF16), preferred_element_type=F32)
            pv = d if pv is None else pv + d
        acc_ref[...] = alpha * acc_ref[...] + pv
        l_ref[...] = l_new
        m_ref[...] = m_new

    s_list = [scores(k_refs[p][...]) + bmask_ref[...] for p in range(pg - 1)]
    s_list.append(scores(k_refs[pg - 1][...]) + jnp.where(is_last, blast_ref[...], bmask_ref[...]))
    update(s_list, [v_refs[p][...] for p in range(pg)])

    @pl.when(is_last)
    def _():
        update([scores(knp_ref[...]) + bnew_ref[...]], [vnp_ref[...]])
        lam = _lambda_value(lam_ref)
        for hh in range(N_HEADS):
            r0 = hh * 2 * dec_seq
            o1 = acc_ref[r0:r0 + dec_seq, :] / l_ref[r0:r0 + dec_seq, :]
            o2 = acc_ref[r0 + dec_seq:r0 + 2 * dec_seq, :] / l_ref[r0 + dec_seq:r0 + 2 * dec_seq, :]
            o_ref[0, :, hh * D_V:(hh + 1) * D_V] = _sub_norm(o1 - lam * o2, sw_ref[...])


def _attn_sample(q_s, kn_rows, vn_rows, cache_k_rows, cache_v_rows, page_table, rel_table, lam_vecs, subln_w):
    bsz, dec_seq, _ = q_s.shape
    n_pages = page_table.shape[1]
    pg = PAGES_PER_STEP
    assert n_pages % pg == 0
    n_groups = n_pages // pg
    q_rows = N_HEADS * 2 * dec_seq
    new_rows = dec_seq * N_HEADS
    tbl_rows = jnp.repeat(rel_table.T, 2 * dec_seq, axis=0)
    tbl_rows = jnp.pad(tbl_rows, ((0, 0), (0, LANES - N_BUCKETS)))
    per_b = lambda shape: pl.BlockSpec(shape, lambda b, g, pt: (b, 0, 0))
    const2 = lambda shape: pl.BlockSpec(shape, lambda b, g, pt: (0, 0))

    def page_spec(p):
        return pl.BlockSpec((PAGE_ROWS, LANES), lambda b, g, pt: (pt[b, g * pg + p], 0))

    grid_spec = pltpu.PrefetchScalarGridSpec(
        num_scalar_prefetch=1,
        grid=(bsz, n_groups),
        in_specs=[per_b((1, dec_seq, D_QKH)), per_b((1, new_rows, LANES)), per_b((1, new_rows, LANES)),
                  const2((q_rows, LANES)), const2((4, D_QK)), const2((1, D_V))]
                 + [page_spec(p) for p in range(pg)] + [page_spec(p) for p in range(pg)],
        out_specs=per_b((1, dec_seq, D_ATT)),
        scratch_shapes=[pltpu.VMEM((q_rows, LANES), F32), pltpu.VMEM((q_rows, LANES), BF16),
                        pltpu.VMEM((LANES, LANES), F32), pltpu.VMEM((LANES, D_V), F32),
                        pltpu.VMEM((q_rows, PAGE_ROWS), F32), pltpu.VMEM((q_rows, PAGE_ROWS), F32),
                        pltpu.VMEM((q_rows, LANES), F32),
                        pltpu.VMEM((q_rows, LANES), F32), pltpu.VMEM((q_rows, LANES), F32),
                        pltpu.VMEM((q_rows, D_V), F32)],
    )
    return pl.pallas_call(
        functools.partial(_attn_sample_kernel, n_groups=n_groups, dec_seq=dec_seq),
        grid_spec=grid_spec,
        out_shape=jax.ShapeDtypeStruct((bsz, dec_seq, D_ATT), F32),
        compiler_params=_cparams(("arbitrary", "arbitrary")),
        name="attn_sample",
    )(page_table, q_s, kn_rows, vn_rows, tbl_rows, lam_vecs, subln_w.reshape(1, D_V),
      *([cache_k_rows] * pg), *([cache_v_rows] * pg))


ROW_CHUNKS = D_MODEL // LANES
assert ROW_CHUNKS == SUBLANES


def _store_row_tiles(ref, mat):
    rows = mat.shape[0]
    for c in range(ROW_CHUNKS):
        ref[pl.ds(c, rows, stride=ROW_CHUNKS), :] = mat[:, c * LANES:(c + 1) * LANES]


def _load_row_tiles(ref, first_row, rows, dtype=F32):
    return jnp.concatenate(
        [ref[pl.ds(first_row * ROW_CHUNKS + c, rows, stride=ROW_CHUNKS), :].astype(dtype) for c in range(ROW_CHUNKS)],
        axis=1)


def _split_bf16(a):
    hi = a.astype(BF16)
    lo = (a - hi.astype(F32)).astype(BF16)
    return hi, lo


def _outproj_kernel(x_ref, yc_ref, o_ref, gt_ref, sc_ref, sh_ref, g_ref, wo_ref, wr_ref, br_ref, cin_ref,
                    x1_ref, h2_ref, idx_ref, gate_ref, rank_ref, cnt_ref):
    tm = x_ref.shape[0]

    @pl.when(pl.program_id(0) == 0)
    def _():
        cnt_ref[...] = cin_ref[...]

    mix = (jnp.dot(yc_ref[...].astype(BF16), wo_ref[0:D_CONV, :], preferred_element_type=F32)
           + jnp.dot(o_ref[...].astype(BF16), wo_ref[D_CONV:, :], preferred_element_type=F32))
    x1 = x_ref[...] + gt_ref[...] * mix
    x1_ref[...] = x1
    ms = jnp.mean(x1 * x1, axis=1, keepdims=True)
    h2 = x1 * lax.rsqrt(ms + EPS) * g_ref[...]
    h2 = h2 * (1.0 + sc_ref[...]) + sh_ref[...]
    _store_row_tiles(h2_ref, h2)
    h_hi, h_lo = _split_bf16(h2)
    w_hi, w_lo = _split_bf16(wr_ref[...])
    logits = (jnp.dot(h_hi, w_hi, preferred_element_type=F32)
              + jnp.dot(h_hi, w_lo, preferred_element_type=F32)
              + jnp.dot(h_lo, w_hi, preferred_element_type=F32)) + br_ref[...]
    lane = lax.broadcasted_iota(I32, (tm, LANES), 1)
    lane_f = lane.astype(F32)
    vals, ids = [], []
    cur = logits
    for _ in range(TOP_K):
        mx = jnp.max(cur, axis=1, keepdims=True)
        ik = jnp.min(jnp.where(cur == mx, lane_f, float(LANES)), axis=1, keepdims=True)
        vals.append(mx)
        ids.append(ik)
        cur = jnp.where(lane_f == ik, -jnp.inf, cur)
    es = [jnp.exp(v - vals[0]) for v in vals]
    denom = functools.reduce(lambda a, c: a + c, es)
    sel = jnp.zeros((tm, LANES), F32)
    idx_out = jnp.zeros((tm, LANES), F32)
    gate_out = jnp.zeros((tm, LANES), F32)
    for k in range(TOP_K):
        sel = sel + jnp.where(lane_f == ids[k], 1.0, 0.0)
        idx_out = jnp.where(lane == k, ids[k], idx_out)
        gate_out = jnp.where(lane == k, es[k] / denom, gate_out)
    r = lax.broadcasted_iota(I32, (tm, tm), 0)
    c = lax.broadcasted_iota(I32, (tm, tm), 1)
    lower = jnp.where(r > c, 1.0, 0.0).astype(BF16)
    before = jnp.dot(lower, sel.astype(BF16), preferred_element_type=F32) + cnt_ref[...]
    rank_out = jnp.zeros((tm, LANES), F32)
    for k in range(TOP_K):
        rk = jnp.sum(jnp.where(lane_f == ids[k], before, 0.0), axis=1, keepdims=True)
        rank_out = jnp.where(lane == k, rk, rank_out)
    cnt_ref[...] = cnt_ref[...] + jnp.sum(sel, axis=0, keepdims=True)
    idx_ref[...] = idx_out.astype(I32)
    gate_ref[...] = gate_out
    rank_ref[...] = rank_out.astype(I32)


def _outproj(x, yc, o, gt, sc, sh, g_ffn, w_out_bf, w_router_pad, b_router_pad, cnt_in, *, tm):
    n = x.shape[0]
    per_row = gt.shape[0] != 1
    mod_spec = pl.BlockSpec((tm, D_MODEL), lambda i: (i, 0)) if per_row else pl.BlockSpec((1, D_MODEL), lambda i: (0, 0))
    const = lambda shape: pl.BlockSpec(shape, lambda i: (0, 0))
    rows = lambda w: pl.BlockSpec((tm, w), lambda i: (i, 0))
    return pl.pallas_call(
        _outproj_kernel,
        grid=(n // tm,),
        in_specs=[rows(D_MODEL), rows(D_CONV), rows(D_ATT), mod_spec, mod_spec, mod_spec, const((1, D_MODEL)),
                  const((D_MODEL, D_MODEL)), const((D_MODEL, LANES)), const((1, LANES)), const((1, LANES))],
        out_specs=[rows(D_MODEL), pl.BlockSpec((tm * ROW_CHUNKS, LANES), lambda i: (i, 0)),
                   rows(LANES), rows(LANES), rows(LANES), const((1, LANES))],
        out_shape=[jax.ShapeDtypeStruct((n, D_MODEL), F32),
                   jax.ShapeDtypeStruct((n * ROW_CHUNKS, LANES), F32),
                   jax.ShapeDtypeStruct((n, LANES), I32),
                   jax.ShapeDtypeStruct((n, LANES), F32),
                   jax.ShapeDtypeStruct((n, LANES), I32),
                   jax.ShapeDtypeStruct((1, LANES), F32)],
        compiler_params=_cparams(("arbitrary",)),
        name="outproj",
    )(x, yc, o, gt, sc, sh, g_ffn.reshape(1, D_MODEL), w_out_bf, w_router_pad, b_router_pad, cnt_in)


def _push_kernel(pend_ref, cnt_ref, dest_hbm, hp_ref, hs_ref, buf_out, idx0, idx1, zero_ref, isem, rsem, zsem,
                 *, n_steps, n_p_steps, n_blocks):
    i = pl.program_id(0)
    tm = hp_ref.shape[0]
    idx_refs = (idx0, idx1)

    @pl.when(i == 0)
    def _():
        zero_ref[...] = jnp.zeros(zero_ref.shape, F32)
        blk_shift = _log2(MOE_BLOCK)
        n_used = pend_ref[N_EXPERTS - 1] >> blk_shift

        def zero_copy(block):
            start = pl.multiple_of(block * MOE_BLOCK, MOE_BLOCK)
            return pltpu.make_async_copy(zero_ref, buf_out.at[pl.ds(start, MOE_BLOCK)], zsem)

        def targets(e):
            return ((cnt_ref[e] > 0, (pend_ref[e] >> blk_shift) - 1), (n_used + e < n_blocks, n_used + e))

        for wait in (False, True):
            for e in range(N_EXPERTS):
                for cond, block in targets(e):
                    @pl.when(cond)
                    def _():
                        if wait:
                            zero_copy(block).wait()
                        else:
                            zero_copy(block).start()

    def idx_copy(step, sl):
        return pltpu.make_async_copy(dest_hbm.at[step], idx_refs[sl], isem.at[sl])

    @pl.when(i == 0)
    def _():
        idx_copy(0, 0).start()

    def push_tile(h_ref, sl):
        idx_copy(i, sl).wait()

        @pl.when(i + 1 < n_steps)
        def _():
            idx_copy(i + 1, 1 - sl).start()

        def row_copy(t, dst_row):
            return pltpu.make_async_copy(h_ref.at[t], buf_out.at[dst_row], rsem)

        def start_body(t, carry):
            for k in range(TOP_K):
                row_copy(t, idx_refs[sl][k * tm + t]).start(priority=k % 2)
            return carry

        lax.fori_loop(0, tm, start_body, 0, unroll=DMA_UNROLL)

        def wait_body(t, carry):
            for k in range(TOP_K):
                row_copy(t, 0).wait()
            return carry

        lax.fori_loop(0, tm, wait_body, 0, unroll=DMA_UNROLL)

    for sl in range(2):
        @pl.when(jnp.logical_and(i < n_p_steps, (i & 1) == sl))
        def _():
            push_tile(hp_ref, sl)

        @pl.when(jnp.logical_and(i >= n_p_steps, (i & 1) == sl))
        def _():
            push_tile(hs_ref, sl)


def _moe_push(h2_p, h2_s, dest_tiles, pend, counts, *, n_blocks):
    tm = TM_ROW
    n_p_steps = h2_p.shape[0] // tm
    n_steps = n_p_steps + h2_s.shape[0] // tm
    assert dest_tiles.shape[0] == n_steps
    any_spec = pl.BlockSpec(memory_space=pl.ANY)
    tile = (tm, ROW_CHUNKS, LANES)
    grid_spec = pltpu.PrefetchScalarGridSpec(
        num_scalar_prefetch=2,
        grid=(n_steps,),
        in_specs=[any_spec,
                  pl.BlockSpec(tile, lambda i, pe, cn: (jnp.minimum(i, n_p_steps - 1), 0, 0)),
                  pl.BlockSpec(tile, lambda i, pe, cn: (jnp.maximum(i - n_p_steps, 0), 0, 0))],
        out_specs=any_spec,
        scratch_shapes=[pltpu.SMEM((TOP_K * tm,), I32), pltpu.SMEM((TOP_K * tm,), I32),
                        pltpu.VMEM((MOE_BLOCK, ROW_CHUNKS, LANES), F32),
                        pltpu.SemaphoreType.DMA((2,)), pltpu.SemaphoreType.DMA(()), pltpu.SemaphoreType.DMA(())],
    )
    return pl.pallas_call(
        functools.partial(_push_kernel, n_steps=n_steps, n_p_steps=n_p_steps, n_blocks=n_blocks),
        grid_spec=grid_spec,
        out_shape=jax.ShapeDtypeStruct((n_blocks * MOE_BLOCK, ROW_CHUNKS, LANES), F32),
        compiler_params=_cparams(("arbitrary",)),
        name="moe_push",
    )(pend, counts, dest_tiles, h2_p, h2_s)


def _expert_kernel(be_ref, nxt_ref, par_ref, nu_ref, x_ref, wg_hbm, bg_ref, wu_hbm, bu_ref, wd_hbm, bd_ref, o_ref,
                   w32, wg_bf, wu_bf, wd_bf, wsem):
    i = pl.program_id(0)
    n_used = nu_ref[0]
    prev = be_ref[jnp.maximum(i - 1, 0)]
    new_expert = jnp.logical_or(i == 0, be_ref[i] != prev)
    w_hbm = (wg_hbm, wu_hbm, wd_hbm)

    def weight_copies(e, sl):
        return [pltpu.make_async_copy(w_hbm[m].at[e], w32.at[sl, m], wsem.at[sl, m]) for m in range(3)]

    @pl.when(i == 0)
    def _():
        for c in weight_copies(be_ref[0], par_ref[0]):
            c.start()

    @pl.when(jnp.logical_and(i < n_used, new_expert))
    def _():
        sl = par_ref[i]
        for c in weight_copies(be_ref[i], sl):
            c.wait()
        wg_bf[...] = w32[sl, 0].astype(BF16)
        wu_bf[...] = w32[sl, 1].astype(BF16)
        wd_bf[...] = w32[sl, 2].astype(BF16)

        @pl.when(nxt_ref[i] >= 0)
        def _():
            for c in weight_copies(nxt_ref[i], 1 - sl):
                c.start()

    @pl.when(i < n_used)
    def _():
        x = _load_row_tiles(x_ref, 0, MOE_BLOCK, BF16)
        g = jnp.dot(x, wg_bf[...], preferred_element_type=F32) + bg_ref[0]
        u = jnp.dot(x, wu_bf[...], preferred_element_type=F32) + bu_ref[0]
        g = jnp.minimum(g, SWIGLU_LIMIT)
        u = jnp.clip(u, -SWIGLU_LIMIT, SWIGLU_LIMIT)
        a = g * jax.nn.sigmoid(SWIGLU_ALPHA * g) * (u + 1.0)
        _store_row_tiles(o_ref, jnp.dot(a.astype(BF16), wd_bf[...], preferred_element_type=F32) + bd_ref[0])

    @pl.when(i >= n_used)
    def _():
        o_ref[...] = jnp.zeros(o_ref.shape, F32)


def _moe_expert(buf, blk_e, n_used, pend, w_gate, b_gate, w_up, b_up, w_down, b_down):
    rows = buf.shape[0] // ROW_CHUNKS
    n_blocks = rows // MOE_BLOCK
    d_ff = w_gate.shape[2]
    assert d_ff == D_MODEL

    def blk(i, be, nx, pa, nu):
        return jnp.minimum(i, nu[0] - 1)

    xspec = pl.BlockSpec((MOE_BLOCK * ROW_CHUNKS, LANES), lambda i, be, nx, pa, nu: (blk(i, be, nx, pa, nu), 0))
    bspec = pl.BlockSpec((1, 1, D_MODEL), lambda i, be, nx, pa, nu: (be[blk(i, be, nx, pa, nu)], 0, 0))
    any_spec = pl.BlockSpec(memory_space=pl.ANY)
    grid_spec = pltpu.PrefetchScalarGridSpec(
        num_scalar_prefetch=4,
        grid=(n_blocks,),
        in_specs=[xspec, any_spec, bspec, any_spec, bspec, any_spec, bspec],
        out_specs=pl.BlockSpec((MOE_BLOCK * ROW_CHUNKS, LANES), lambda i, be, nx, pa, nu: (i, 0)),
        scratch_shapes=[pltpu.VMEM((2, 3, D_MODEL, D_MODEL), F32),
                        pltpu.VMEM((D_MODEL, d_ff), BF16), pltpu.VMEM((D_MODEL, d_ff), BF16),
                        pltpu.VMEM((d_ff, D_MODEL), BF16), pltpu.SemaphoreType.DMA((2, 3))],
    )
    first = jnp.concatenate([jnp.ones((1,), bool), blk_e[1:] != blk_e[:-1]])
    parity = ((jnp.cumsum(first.astype(I32)) - 1) & 1).astype(I32)
    next_first = pend[blk_e] // MOE_BLOCK
    nxt_e = jnp.where(next_first < n_used[0], blk_e[jnp.minimum(next_first, n_blocks - 1)], -1).astype(I32)
    return pl.pallas_call(
        _expert_kernel,
        grid_spec=grid_spec,
        out_shape=jax.ShapeDtypeStruct((rows * ROW_CHUNKS, LANES), F32),
        compiler_params=_cparams(("arbitrary",)),
        name="moe_expert",
    )(blk_e, nxt_e, parity, n_used, buf, w_gate, b_gate.reshape(N_EXPERTS, 1, d_ff), w_up,
      b_up.reshape(N_EXPERTS, 1, d_ff), w_down, b_down.reshape(N_EXPERTS, 1, D_MODEL))


def _combine_kernel(dest_hbm, out_hbm, x1_ref, gate_ref, gt_ref, y_ref,
                    idx0, idx1, rows0, rows1, isem, rsem, *, n_steps):
    i = pl.program_id(0)
    tm = x1_ref.shape[0]
    n_rows = TOP_K * tm
    idx_refs = (idx0, idx1)
    rows_refs = (rows0, rows1)

    def idx_copy(step, sl):
        return pltpu.make_async_copy(dest_hbm.at[step], idx_refs[sl], isem.at[sl])

    def row_copy(src_row, sl, r):
        dst = rows_refs[sl].at[pl.ds(pl.multiple_of(r * ROW_CHUNKS, ROW_CHUNKS), ROW_CHUNKS), :]
        return pltpu.make_async_copy(out_hbm.at[src_row], dst, rsem.at[sl])

    def issue_rows(sl):
        def body(r2, carry):
            for par in range(2):
                r = 2 * r2 + par
                row_copy(idx_refs[sl][r], sl, r).start(priority=par)
            return carry
        lax.fori_loop(0, n_rows // 2, body, 0, unroll=DMA_UNROLL)

    def wait_rows(sl):
        def body(r, carry):
            row_copy(0, sl, r).wait()
            return carry
        lax.fori_loop(0, n_rows, body, 0, unroll=2 * DMA_UNROLL)

    @pl.when(i == 0)
    def _():
        idx_copy(0, 0).start()
        idx_copy(0, 0).wait()
        issue_rows(0)
        if n_steps > 1:
            idx_copy(1, 1).start()

    for sl in range(2):
        @pl.when((i & 1) == sl)
        def _():
            @pl.when(i + 1 < n_steps)
            def _():
                idx_copy(i + 1, 1 - sl).wait()
                issue_rows(1 - sl)

            @pl.when(i + 2 < n_steps)
            def _():
                idx_copy(i + 2, sl).start()

            wait_rows(sl)
            gates = gate_ref[...]
            y = jnp.zeros((tm, D_MODEL), F32)
            for k in range(TOP_K):
                y = y + gates[:, k:k + 1] * _load_row_tiles(rows_refs[sl], k * tm, tm)
            y_ref[...] = x1_ref[...] + gt_ref[...] * y


def _moe_combine(out_rows, dest_tiles, x1, gates, gt):
    n = x1.shape[0]
    tm = TM_ROW
    n_steps = n // tm
    per_row = gt.shape[0] != 1
    mod_spec = pl.BlockSpec((tm, D_MODEL), lambda i: (i, 0)) if per_row else pl.BlockSpec((1, D_MODEL), lambda i: (0, 0))
    return pl.pallas_call(
        functools.partial(_combine_kernel, n_steps=n_steps),
        grid=(n_steps,),
        in_specs=[pl.BlockSpec(memory_space=pl.ANY), pl.BlockSpec(memory_space=pl.ANY),
                  pl.BlockSpec((tm, D_MODEL), lambda i: (i, 0)),
                  pl.BlockSpec((tm, LANES), lambda i: (i, 0)),
                  mod_spec],
        out_specs=pl.BlockSpec((tm, D_MODEL), lambda i: (i, 0)),
        out_shape=jax.ShapeDtypeStruct((n, D_MODEL), F32),
        scratch_shapes=[pltpu.SMEM((TOP_K * tm,), I32), pltpu.SMEM((TOP_K * tm,), I32),
                        pltpu.VMEM((TOP_K * tm * ROW_CHUNKS, LANES), F32),
                        pltpu.VMEM((TOP_K * tm * ROW_CHUNKS, LANES), F32),
                        pltpu.SemaphoreType.DMA((2,)), pltpu.SemaphoreType.DMA((2,))],
        compiler_params=_cparams(("arbitrary",)),
        name="moe_combine",
    )(dest_tiles, out_rows, x1, gates, gt)


def _dest_tiles(dest, tm):
    n = dest.shape[0]
    return dest.reshape(n // tm, tm, TOP_K).transpose(0, 2, 1).reshape(n // tm, TOP_K * tm)


def kernel(x_prompt, x_sample, cache_k, cache_v, state_conv, page_table, c_prompt, c_sample, rel_table, w_ada, b_ada, g_mix, w_in, conv_w, q_norm_w, k_norm_w, lam_q1, lam_k1, lam_q2, lam_k2, subln_w, w_out, g_ffn, w_router, b_router, w_gate, b_gate, w_up, b_up, w_down, b_down):
    assert w_ada.shape[0] == 1, "single-layer trunk"
    bp, t_p, _ = x_prompt.shape
    bs, t_s, _ = x_sample.shape
    assert bp == 1
    n_s = bs * t_s
    n_all = t_p + n_s
    l = 0

    n_c = bp + bs
    c_rows = -(-n_c // SUBLANES) * SUBLANES
    c_all = jnp.concatenate([c_prompt, c_sample, jnp.zeros((c_rows - n_c, D_MODEL), F32)], axis=0)
    mod = _adaln(c_all, w_ada[l], b_ada[l]).reshape(c_rows, 6, D_MODEL)
    mod_p = [mod[0:1, j, :] for j in range(6)]
    per_token = jnp.broadcast_to(mod[1:1 + bs, None, :, :], (bs, t_s, 6, D_MODEL)).reshape(n_s, 6, D_MODEL)
    mod_s = [per_token[:, j, :] for j in range(6)]

    w_in_bf = w_in[l].astype(BF16)
    w_out_bf = w_out[l].astype(BF16)
    lam_vecs = jnp.stack([lam_q1[l], lam_k1[l], lam_q2[l], lam_k2[l]])

    xp = x_prompt.reshape(t_p, D_MODEL)
    xs = x_sample.reshape(n_s, D_MODEL)
    yc_p, q_p, k_p, v_p, kb_p, vt_p, u_tail = _inproj(
        xp, mod_p[1], mod_p[0], g_mix[l], w_in_bf, conv_w[l], q_norm_w[l], k_norm_w[l], None, None,
        tm=TM_IN, seq_len=None)
    st = state_conv[l]
    prev2 = jnp.broadcast_to(st[:, None, 0, :], (bs, t_s, D_CONV)).reshape(n_s, D_CONV)
    prev1 = jnp.broadcast_to(st[:, None, 1, :], (bs, t_s, D_CONV)).reshape(n_s, D_CONV)
    yc_s, q_s, k_s, v_s, _, _, u_s = _inproj(
        xs, mod_s[1], mod_s[0], g_mix[l], w_in_bf, conv_w[l], q_norm_w[l], k_norm_w[l], prev2, prev1,
        tm=n_s, seq_len=t_s)

    o_p = _attn_prompt(q_p, kb_p, vt_p, rel_table, lam_vecs, subln_w[l])
    n_phys = cache_k.shape[1]
    o_s = _attn_sample(q_s.astype(F32).reshape(bs, t_s, D_QKH),
                       k_s.reshape(bs, t_s * N_HEADS, 2 * D_QK), v_s.reshape(bs, t_s * N_HEADS, D_V),
                       cache_k[l].reshape(n_phys * PAGE_ROWS, 2 * D_QK), cache_v[l].reshape(n_phys * PAGE_ROWS, D_V),
                       page_table, rel_table, lam_vecs, subln_w[l]).reshape(n_s, D_ATT)

    w_router_pad = jnp.pad(w_router[l], ((0, 0), (0, LANES - N_EXPERTS)))
    b_router_pad = jnp.concatenate([b_router[l], jnp.full((LANES - N_EXPERTS,), NEG_INF, F32)]).reshape(1, LANES)
    cnt0 = jnp.zeros((1, LANES), F32)
    x1_p, h2_p, idx_p, gate_p, rank_p, cnt1 = _outproj(
        xp, yc_p, o_p, mod_p[2], mod_p[4], mod_p[3], g_ffn[l], w_out_bf, w_router_pad, b_router_pad, cnt0, tm=TM_OUT)
    x1_s, h2_s, idx_s, gate_s, rank_s, cnt2 = _outproj(
        xs, yc_s, o_s, mod_s[2], mod_s[4], mod_s[3], g_ffn[l], w_out_bf, w_router_pad, b_router_pad, cnt1,
        tm=min(TM_OUT, n_s))

    counts = cnt2[0, :N_EXPERTS].astype(I32)
    padded = (counts + MOE_BLOCK - 1) // MOE_BLOCK * MOE_BLOCK
    pend = jnp.cumsum(padded).astype(I32)
    pstart = pend - padded
    n_blocks = (n_all * TOP_K) // MOE_BLOCK + N_EXPERTS
    def slots(idx, rank):
        is_e = idx[:, :TOP_K, None] == jnp.arange(N_EXPERTS, dtype=I32)
        return jnp.sum(jnp.where(is_e, pstart, 0), axis=-1) + rank[:, :TOP_K]

    dest_p = slots(idx_p, rank_p)
    dest_s = slots(idx_s, rank_s)
    block_start = jnp.arange(n_blocks, dtype=I32) * MOE_BLOCK
    blk_e = jnp.minimum(jnp.sum(pend[None, :] <= block_start[:, None], axis=1), N_EXPERTS - 1).astype(I32)
    n_used = (pend[-1] // MOE_BLOCK).astype(I32).reshape(1)
    tiles_p = _dest_tiles(dest_p, TM_ROW)
    tiles_s = _dest_tiles(dest_s, TM_ROW)

    as_tiles = lambda a: a.reshape(a.shape[0] // ROW_CHUNKS, ROW_CHUNKS, LANES)
    buf = _moe_push(as_tiles(h2_p), as_tiles(h2_s), jnp.concatenate([tiles_p, tiles_s], axis=0), pend, counts,
                    n_blocks=n_blocks)
    out_rows = as_tiles(_moe_expert(buf.reshape(n_blocks * MOE_BLOCK * ROW_CHUNKS, LANES), blk_e, n_used, pend,
                                    w_gate[l], b_gate[l], w_up[l], b_up[l], w_down[l], b_down[l]))
    y_p = _moe_combine(out_rows, tiles_p, x1_p, gate_p, mod_p[5])
    y_s = _moe_combine(out_rows, tiles_s, x1_s, gate_s, mod_s[5])

    k_prompt = k_p.reshape(1, bp, t_p, N_HEADS, 2 * D_QK)
    v_prompt = v_p.reshape(1, bp, t_p, N_HEADS, D_V)
    conv_prompt = u_tail[SUBLANES - (CONV_W - 1):, :].reshape(1, bp, CONV_W - 1, D_CONV)
    k_sample = k_s.reshape(1, bs, t_s, N_HEADS, 2 * D_QK)
    v_sample = v_s.reshape(1, bs, t_s, N_HEADS, D_V)
    conv_sample = u_s.reshape(bs, t_s, D_CONV)[:, t_s - (CONV_W - 1):, :].reshape(1, bs, CONV_W - 1, D_CONV)
    return (y_p.reshape(bp, t_p, D_MODEL), y_s.reshape(bs, t_s, D_MODEL),
            k_prompt, v_prompt, conv_prompt, k_sample, v_sample, conv_sample)
```

```python
import functools
import math

import numpy as np
import jax
import jax.numpy as jnp
from jax import lax
from jax.experimental import pallas as pl
from jax.experimental.pallas import tpu as pltpu

F32 = jnp.float32
BF16 = jnp.bfloat16
I32 = jnp.int32

D_MODEL = 1024
D_CONV = 512
CONV_W = 3
N_HEADS = 4
D_QK = 64
D_V = 128
D_ATT = N_HEADS * D_V
D_QKH = N_HEADS * 2 * D_QK
D_IN = 3 * D_CONV + 2 * D_QKH + D_ATT
N_BUCKETS = 32
MAX_EXACT = 16
MAX_DIST = 128
N_EXPERTS = 32
TOP_K = 4
SWIGLU_LIMIT = 7.0
SWIGLU_ALPHA = 1.702
PAGE_SIZE = 128
EPS = 1e-6
NEG_INF = -1e30
LAM_INIT = 0.8 - 0.6 * math.exp(-0.3 * 0)
LOG2E = math.log2(math.e)
Q_SCALE = D_QK ** -0.5 * LOG2E
SCORE_HEADROOM = 60.0

LANES = 128
SUBLANES = 8
VMEM_LIMIT = 56 * 1024 * 1024

TM_IN = 512
TQ = 512
TK = 512
FAR_GROUP = 4
PAGES_PER_STEP = 32
TM_OUT = 512
MOE_BLOCK = 512
TM_ROW = 256
DMA_UNROLL = 4
assert TK == TM_IN


def _bucket_thresholds():
    n = np.arange(0, 4 * MAX_DIST)
    nf = np.maximum(n, 1).astype(np.float32)
    val = np.log(nf / np.float32(MAX_EXACT)) / np.float32(math.log(MAX_DIST / MAX_EXACT)) * np.float32(N_BUCKETS - MAX_EXACT)
    large = np.minimum(MAX_EXACT + val.astype(np.int32), N_BUCKETS - 1)
    bucket = np.where(n < MAX_EXACT, n, large)
    assert np.all(np.diff(bucket) >= 0) and bucket[-1] == N_BUCKETS - 1
    thr = [int(np.argmax(bucket >= b)) for b in range(N_BUCKETS)]
    return thr


BUCKET_THR = _bucket_thresholds()
FAR_DIST = BUCKET_THR[N_BUCKETS - 1]


def _log2(n):
    assert n > 0 and n & (n - 1) == 0
    return n.bit_length() - 1


def _cparams(sem):
    return pltpu.CompilerParams(dimension_semantics=sem, vmem_limit_bytes=VMEM_LIMIT)


def _adaln_kernel(c_ref, w_ref, b_ref, o_ref):
    c = c_ref[...]
    s = c * jax.nn.sigmoid(c)
    o_ref[...] = jnp.dot(s, w_ref[...], preferred_element_type=F32,
                         precision=lax.Precision.HIGHEST) + b_ref[...]


def _adaln(c_all, w_ada, b_ada):
    rows = c_all.shape[0]
    n = w_ada.shape[1]
    bn = D_MODEL
    return pl.pallas_call(
        _adaln_kernel,
        grid=(n // bn,),
        in_specs=[pl.BlockSpec((rows, D_MODEL), lambda j: (0, 0)),
                  pl.BlockSpec((D_MODEL, bn), lambda j: (0, j)),
                  pl.BlockSpec((1, bn), lambda j: (0, j))],
        out_specs=pl.BlockSpec((rows, bn), lambda j: (0, j)),
        out_shape=jax.ShapeDtypeStruct((rows, n), F32),
        compiler_params=_cparams(("arbitrary",)),
        name="adaln",
    )(c_all, w_ada, b_ada.reshape(1, n))


def _group_rms(xb, w_row):
    lane = lax.broadcasted_iota(I32, xb.shape, 1)
    lo_half = lane < D_QK
    sq = xb * xb
    lo = jnp.sum(jnp.where(lo_half, sq, 0.0), axis=1, keepdims=True)
    hi = jnp.sum(jnp.where(lo_half, 0.0, sq), axis=1, keepdims=True)
    ms = jnp.where(lo_half, lo, hi) * (1.0 / D_QK)
    return xb * lax.rsqrt(ms + EPS) * w_row


def _inproj_kernel(x_ref, sc_ref, sh_ref, g_ref, w_ref, cw_ref, qw_ref, kw_ref, p2_ref, p1_ref,
                   yc_ref, q_ref, k_ref, v_ref, kb_ref, vt_ref, u_ref, carry_ref, z0_ref, z1_ref, *, seq_len):
    step = pl.program_id(0)

    @pl.when(step == 0)
    def _():
        z1_ref[...] = jnp.zeros(z1_ref.shape, F32)
        carry_ref[...] = jnp.zeros_like(carry_ref)

    for par, (z_write, z_read) in enumerate(((z0_ref, z1_ref), (z1_ref, z0_ref))):
        @pl.when((step & 1) == par)
        def _():
            _inproj_tile(x_ref, sc_ref, sh_ref, g_ref, w_ref, cw_ref, qw_ref, kw_ref, p2_ref, p1_ref,
                         yc_ref, q_ref, k_ref, v_ref, kb_ref, vt_ref, u_ref, carry_ref, z_write, z_read,
                         seq_len=seq_len)


def _inproj_tile(x_ref, sc_ref, sh_ref, g_ref, w_ref, cw_ref, qw_ref, kw_ref, p2_ref, p1_ref,
                 yc_ref, q_ref, k_ref, v_ref, kb_ref, vt_ref, u_ref, carry_ref, z_write, z, *, seq_len):
    tm = x_ref.shape[0]
    x = x_ref[...]
    ms = jnp.mean(x * x, axis=1, keepdims=True)
    h = x * lax.rsqrt(ms + EPS) * g_ref[...]
    h = h * (1.0 + sc_ref[...]) + sh_ref[...]
    z_write[...] = jnp.dot(h.astype(BF16), w_ref[...], preferred_element_type=F32)
    gb = z[:, 0:D_CONV]
    u = z[:, D_CONV:2 * D_CONV] * z[:, 2 * D_CONV:3 * D_CONV]
    row = lax.broadcasted_iota(I32, (tm, D_CONV), 0)
    u1 = pltpu.roll(u, 1, 0)
    u2 = pltpu.roll(u, 2, 0)
    if seq_len is None:
        prev2 = carry_ref[SUBLANES - 2:SUBLANES - 1, :]
        prev1 = carry_ref[SUBLANES - 1:SUBLANES, :]
        u1 = jnp.where(row == 0, prev1, u1)
        u2 = jnp.where(row == 0, prev2, jnp.where(row == 1, prev1, u2))
        carry_ref[...] = u[tm - SUBLANES:tm, :]
        u_ref[...] = u[tm - SUBLANES:tm, :]
    else:
        pos = row & (seq_len - 1)
        u1 = jnp.where(pos == 0, p1_ref[...], u1)
        u2 = jnp.where(pos == 0, p2_ref[...], jnp.where(pos == 1, p1_ref[...], u2))
        u_ref[...] = u
    cw = cw_ref[...]
    yc = gb * (cw[0:1, :] * u2 + cw[1:2, :] * u1 + cw[2:3, :] * u)
    yc_ref[...] = yc.astype(yc_ref.dtype)
    o1 = 3 * D_CONV
    for hh in range(N_HEADS):
        c0 = o1 + hh * LANES
        qn = _group_rms(z[:, c0:c0 + LANES], qw_ref[...])
        q_ref[:, hh * LANES:(hh + 1) * LANES] = (qn * Q_SCALE).astype(q_ref.dtype)
        c1 = o1 + D_QKH + hh * LANES
        kn = _group_rms(z[:, c1:c1 + LANES], kw_ref[...])
        k_ref[pl.ds(hh, tm, stride=N_HEADS), :] = kn
        kb_ref[:, hh * LANES:(hh + 1) * LANES] = kn.astype(BF16)
    v = z[:, o1 + 2 * D_QKH:]
    for hh in range(N_HEADS):
        vh = v[:, hh * D_V:(hh + 1) * D_V]
        v_ref[pl.ds(hh, tm, stride=N_HEADS), :] = vh
        vt_ref[0, hh * D_V:(hh + 1) * D_V, :] = vh.T.astype(BF16)


def _inproj(x, sc, sh, g_mix, w_in_bf, conv_w, qw, kw, prev2, prev1, *, tm, seq_len):
    n = x.shape[0]
    n_tiles = n // tm
    grid = (n_tiles + 1,)
    proj = lambda i: jnp.minimum(i, n_tiles - 1)
    epi = lambda i: jnp.maximum(i - 1, 0)
    per_row = sc.shape[0] != 1
    mod_spec = (pl.BlockSpec((tm, D_MODEL), lambda i: (proj(i), 0)) if per_row
                else pl.BlockSpec((1, D_MODEL), lambda i: (0, 0)))
    if seq_len is None:
        prev2 = jnp.zeros((SUBLANES, D_CONV), F32)
        prev1 = prev2
        prev_spec = pl.BlockSpec((SUBLANES, D_CONV), lambda i: (0, 0))
        u_shape = jax.ShapeDtypeStruct((SUBLANES, D_CONV), F32)
        u_spec = pl.BlockSpec((SUBLANES, D_CONV), lambda i: (0, 0))
    else:
        assert seq_len & (seq_len - 1) == 0
        prev_spec = pl.BlockSpec((tm, D_CONV), lambda i: (epi(i), 0))
        u_shape = jax.ShapeDtypeStruct((n, D_CONV), F32)
        u_spec = pl.BlockSpec((tm, D_CONV), lambda i: (epi(i), 0))
    const = lambda shape: pl.BlockSpec(shape, lambda i: (0, 0))
    rows = lambda w: pl.BlockSpec((tm, w), lambda i: (epi(i), 0))
    head_rows = pl.BlockSpec((tm * N_HEADS, LANES), lambda i: (epi(i), 0))
    qw2 = jnp.tile(qw.reshape(1, D_QK), (1, 2))
    kw2 = jnp.tile(kw.reshape(1, D_QK), (1, 2))
    return pl.pallas_call(
        functools.partial(_inproj_kernel, seq_len=seq_len),
        grid=grid,
        in_specs=[pl.BlockSpec((tm, D_MODEL), lambda i: (proj(i), 0)), mod_spec, mod_spec, const((1, D_MODEL)),
                  const((D_MODEL, D_IN)), const((CONV_W, D_CONV)), const((1, LANES)), const((1, LANES)),
                  prev_spec, prev_spec],
        out_specs=[rows(D_CONV), rows(D_QKH), head_rows, head_rows, rows(D_QKH),
                   pl.BlockSpec((1, D_ATT, tm), lambda i: (epi(i), 0, 0)), u_spec],
        out_shape=[jax.ShapeDtypeStruct((n, D_CONV), BF16),
                   jax.ShapeDtypeStruct((n, D_QKH), BF16),
                   jax.ShapeDtypeStruct((n * N_HEADS, 2 * D_QK), F32),
                   jax.ShapeDtypeStruct((n * N_HEADS, D_V), F32),
                   jax.ShapeDtypeStruct((n, D_QKH), BF16),
                   jax.ShapeDtypeStruct((n // tm, D_ATT, tm), BF16),
                   u_shape],
        scratch_shapes=[pltpu.VMEM((SUBLANES, D_CONV), F32), pltpu.VMEM((tm, D_IN), F32), pltpu.VMEM((tm, D_IN), F32)],
        compiler_params=_cparams(("arbitrary",)),
        name="inproj_seq" if seq_len is None else "inproj_batch",
    )(x, sc, sh, g_mix.reshape(1, D_MODEL), w_in_bf, conv_w, qw2, kw2, prev2, prev1)


def _bias_from_dist(dist, table_of_bucket):
    b = jnp.zeros(dist.shape, F32) + table_of_bucket(0)
    for bk in range(1, N_BUCKETS):
        b = jnp.where(dist >= BUCKET_THR[bk], table_of_bucket(bk), b)
    return (b - table_of_bucket(N_BUCKETS - 1)) * LOG2E


def _lambda_value(lam_ref):
    lq1, lk1, lq2, lk2 = (lam_ref[i:i + 1, :] for i in range(4))
    return (jnp.exp(jnp.sum(lq1 * lk1, axis=1, keepdims=True))
            - jnp.exp(jnp.sum(lq2 * lk2, axis=1, keepdims=True)) + LAM_INIT)


def _sub_norm(o, sw_row):
    ms = jnp.mean(o * o, axis=1, keepdims=True)
    return o * lax.rsqrt(ms + EPS) * sw_row * (1.0 - LAM_INIT)


def _attn_prompt_kernel(tbl_ref, q_ref, k_ref, vt_ref, lam_ref, sw_ref, o_ref,
                        bdiag_ref, bprev_ref, m_ref, l_ref, acc_ref):
    h = pl.program_id(0)
    i = pl.program_id(1)
    tq = q_ref.shape[0]
    tk = vt_ref.shape[2]
    assert tq == tk and tk >= FAR_DIST

    @pl.when(i == 0)
    def _():
        key = lax.broadcasted_iota(I32, (tk, tq), 0)
        qry = lax.broadcasted_iota(I32, (tk, tq), 1)
        d0 = qry - key
        tb = lambda b: tbl_ref[b, h]
        bdiag_ref[...] = jnp.where(d0 >= 0, _bias_from_dist(jnp.maximum(d0, 0), tb), NEG_INF)
        bprev_ref[...] = _bias_from_dist(d0 + tk, tb)

    m_ref[...] = jnp.full(m_ref.shape, -jnp.inf, F32)
    l_ref[...] = jnp.zeros(l_ref.shape, F32)
    acc_ref[...] = jnp.zeros(acc_ref.shape, F32)

    q = q_ref[...]
    lane = lax.broadcasted_iota(I32, q.shape, 1)
    zero = jnp.zeros_like(q)
    q_maps = (jnp.where(lane < D_QK, q, zero), jnp.where(lane < D_QK, zero, q))

    def scores_t(kb, mi):
        return lax.dot_general(kb, q_maps[mi], (((1,), (1,)), ((), ())), preferred_element_type=F32)

    def key_block(j):
        return k_ref[pl.ds(pl.multiple_of(j * tk, tk), tk), :], vt_ref[j]

    def exact_step(j, bias_ref):
        kb, vtb = key_block(j)
        for mi in range(2):
            st = scores_t(kb, mi)
            if bias_ref is not None:
                st = st + bias_ref[...]
            m_prev = m_ref[mi]
            m_new = jnp.maximum(m_prev, jnp.max(st, axis=0, keepdims=True))
            alpha = jnp.exp2(m_prev - m_new)
            pt = jnp.exp2(st - m_new)
            l_ref[mi] = alpha * l_ref[mi] + jnp.sum(pt, axis=0, keepdims=True)
            acc_ref[mi] = alpha * acc_ref[mi] + jnp.dot(vtb, pt.astype(BF16), preferred_element_type=F32)
            m_ref[mi] = m_new

    def streamed_step(j, n_blk, bias_ref=None):
        kb = k_ref[pl.ds(pl.multiple_of(j * tk, tk), n_blk * tk), :]
        sts = [scores_t(kb, mi) for mi in range(2)]
        if bias_ref is not None:
            assert n_blk == 1
            sts = [st + bias_ref[...] for st in sts]
        parts = []
        worst = None
        for mi in range(2):
            ref_pt = m_ref[mi]
            pt = jnp.exp2(sts[mi] - ref_pt)
            blk_max = jnp.max(sts[mi], axis=0, keepdims=True)
            l_add = jnp.sum(pt, axis=0, keepdims=True)
            acc_add = functools.reduce(lambda a, b: a + b, [
                jnp.dot(vt_ref[j + u], pt[u * tk:(u + 1) * tk, :].astype(BF16), preferred_element_type=F32)
                for u in range(n_blk)])
            parts.append((ref_pt, blk_max, l_add, acc_add))
            excess = blk_max - ref_pt
            worst = excess if worst is None else jnp.maximum(worst, excess)
        in_range = jnp.max(worst) <= SCORE_HEADROOM

        @pl.when(in_range)
        def _():
            for mi, (ref_pt, blk_max, l_add, acc_add) in enumerate(parts):
                m_new = jnp.maximum(ref_pt, blk_max)
                alpha = jnp.exp2(ref_pt - m_new)
                l_ref[mi] = alpha * (l_ref[mi] + l_add)
                acc_ref[mi] = alpha * (acc_ref[mi] + acc_add)
                m_ref[mi] = m_new

        @pl.when(jnp.logical_not(in_range))
        def _():
            for u in range(n_blk):
                exact_step(j + u, bias_ref)

    kd = k_ref[pl.ds(pl.multiple_of(i * tk, tk), tk), :]
    ones = jnp.ones((SUBLANES, LANES), BF16)
    self_bias = (tbl_ref[0, h] - tbl_ref[N_BUCKETS - 1, h]) * LOG2E
    for mi in range(2):
        self_score = lax.dot_general(ones, q_maps[mi] * kd, (((1,), (1,)), ((), ())), preferred_element_type=F32)
        m_ref[mi] = self_score[0:1, :] + self_bias
    streamed_step(i, 1, bdiag_ref)

    @pl.when(i > 0)
    def _():
        streamed_step(i - 1, 1, bprev_ref)

    n_far = jnp.maximum(i - 1, 0)

    group_shift = _log2(FAR_GROUP)

    def far_group(jj, carry):
        streamed_step(FAR_GROUP * jj, FAR_GROUP)
        return carry

    lax.fori_loop(0, n_far >> group_shift, far_group, 0)

    def far_single(j, carry):
        streamed_step(j, 1)
        return carry

    lax.fori_loop((n_far >> group_shift) << group_shift, n_far, far_single, 0)

    lam = _lambda_value(lam_ref)
    ot = acc_ref[0] / l_ref[0] - lam * (acc_ref[1] / l_ref[1])
    o_ref[...] = _sub_norm(ot.T, sw_ref[...]).astype(o_ref.dtype)


def _attn_prompt(q_bf, k_bf, vt_bf, rel_table, lam_vecs, subln_w):
    t = q_bf.shape[0]
    nk, _, tk = vt_bf.shape
    assert tk == TK and nk * tk == t
    nq = t // TQ
    grid_spec = pltpu.PrefetchScalarGridSpec(
        num_scalar_prefetch=0,
        grid=(N_HEADS, nq),
        in_specs=[pl.BlockSpec(memory_space=pltpu.SMEM),
                  pl.BlockSpec((TQ, LANES), lambda h, i: (i, h)),
                  pl.BlockSpec((t, LANES), lambda h, i: (0, h)),
                  pl.BlockSpec((nk, D_V, tk), lambda h, i: (0, h, 0)),
                  pl.BlockSpec((4, D_QK), lambda h, i: (0, 0)),
                  pl.BlockSpec((1, D_V), lambda h, i: (0, 0))],
        out_specs=pl.BlockSpec((TQ, D_V), lambda h, i: (i, h)),
        scratch_shapes=[pltpu.VMEM((TK, TQ), F32), pltpu.VMEM((TK, TQ), F32),
                        pltpu.VMEM((2, 1, TQ), F32), pltpu.VMEM((2, 1, TQ), F32),
                        pltpu.VMEM((2, D_V, TQ), F32)],
    )
    return pl.pallas_call(
        _attn_prompt_kernel,
        grid_spec=grid_spec,
        out_shape=jax.ShapeDtypeStruct((t, D_ATT), BF16),
        compiler_params=_cparams(("arbitrary", "arbitrary")),
        name="attn_prompt",
    )(rel_table, q_bf, k_bf, vt_bf, lam_vecs, subln_w.reshape(1, D_V))


PAGE_ROWS = PAGE_SIZE * N_HEADS


def _attn_sample_kernel(pt_ref, q_ref, kn_ref, vn_ref, tblr_ref, lam_ref, sw_ref, *rest, n_groups, dec_seq):
    pg = PAGES_PER_STEP
    k_refs = rest[:pg]
    v_refs = rest[pg:2 * pg]
    o_ref = rest[2 * pg]
    qf_ref, qb_ref, knp_ref, vnp_ref, bmask_ref, blast_ref, bnew_ref, m_ref, l_ref, acc_ref = rest[2 * pg + 1:]
    b = pl.program_id(0)
    g = pl.program_id(1)
    q_rows = N_HEADS * 2 * dec_seq
    new_rows = dec_seq * N_HEADS
    head_shift = _log2(2 * dec_seq)
    key_shift = _log2(N_HEADS)
    assert PAGE_SIZE + 1 >= FAR_DIST and new_rows <= LANES

    @pl.when(jnp.logical_and(b == 0, g == 0))
    def _():
        tb = lambda bk: tblr_ref[:, bk:bk + 1]
        r = lax.broadcasted_iota(I32, (q_rows, PAGE_ROWS), 0)
        c = lax.broadcasted_iota(I32, (q_rows, PAGE_ROWS), 1)
        same_head = (c & (N_HEADS - 1)) == (r >> head_shift)
        tok = r & (dec_seq - 1)
        key = c >> key_shift
        bmask_ref[...] = jnp.where(same_head, 0.0, NEG_INF)
        blast_ref[...] = jnp.where(same_head, _bias_from_dist(tok + PAGE_SIZE - key, tb), NEG_INF)
        r2 = lax.broadcasted_iota(I32, (q_rows, LANES), 0)
        c2 = lax.broadcasted_iota(I32, (q_rows, LANES), 1)
        d2 = (r2 & (dec_seq - 1)) - (c2 >> key_shift)
        ok2 = jnp.logical_and((c2 & (N_HEADS - 1)) == (r2 >> head_shift),
                              jnp.logical_and(d2 >= 0, c2 < new_rows))
        bnew_ref[...] = jnp.where(ok2, _bias_from_dist(jnp.maximum(d2, 0), tb), NEG_INF)

    @pl.when(g == 0)
    def _():
        q = q_ref[0]
        lane = lax.broadcasted_iota(I32, (dec_seq, LANES), 1)
        for hh in range(N_HEADS):
            qh = q[:, hh * LANES:(hh + 1) * LANES]
            r0 = hh * 2 * dec_seq
            qf_ref[r0:r0 + dec_seq, :] = jnp.where(lane < D_QK, qh, 0.0)
            qf_ref[r0 + dec_seq:r0 + 2 * dec_seq, :] = jnp.where(lane < D_QK, 0.0, qh)
        qb_ref[...] = qf_ref[...].astype(BF16)
        knp_ref[...] = jnp.zeros(knp_ref.shape, F32)
        vnp_ref[...] = jnp.zeros(vnp_ref.shape, F32)
        knp_ref[0:new_rows, :] = kn_ref[0]
        vnp_ref[0:new_rows, :] = vn_ref[0]
        m_ref[...] = jnp.full(m_ref.shape, -jnp.inf, F32)
        l_ref[...] = jnp.zeros(l_ref.shape, F32)
        acc_ref[...] = jnp.zeros(acc_ref.shape, F32)

    qb = qb_ref[...]
    is_last = g == n_groups - 1

    def scores(k_rows):
        return lax.dot_general(qb, k_rows.astype(BF16), (((1,), (1,)), ((), ())), preferred_element_type=F32)

    def update(s_list, v_list):
        m_prev = m_ref[...]
        m_cur = functools.reduce(jnp.maximum, [jnp.max(s, axis=1, keepdims=True) for s in s_list])
        m_new = jnp.maximum(m_prev, m_cur)
        alpha = jnp.exp2(m_prev - m_new)
        l_new = alpha * l_ref[...]
        pv = None
        for s, v_rows in zip(s_list, v_list):
            p = jnp.exp2(s - jnp.concatenate([m_new] * (s.shape[1] // LANES), axis=1))
            l_new = l_new + jnp.sum(p, axis=1, keepdims=True)
            d = jnp.dot(p.astype(BF16), v_rows.astype(BF16), preferred_element_type=F32)
            pv = d if pv is None else pv + d
        acc_ref[...] = alpha * acc_ref[...] + pv
        l_ref[...] = l_new
        m_ref[...] = m_new

    s_list = [scores(k_refs[p][...]) + bmask_ref[...] for p in range(pg - 1)]
    s_list.append(scores(k_refs[pg - 1][...]) + jnp.where(is_last, blast_ref[...], bmask_ref[...]))
    update(s_list, [v_refs[p][...] for p in range(pg)])

    @pl.when(is_last)
    def _():
        update([scores(knp_ref[...]) + bnew_ref[...]], [vnp_ref[...]])
        lam = _lambda_value(lam_ref)
        for hh in range(N_HEADS):
            r0 = hh * 2 * dec_seq
            o1 = acc_ref[r0:r0 + dec_seq, :] / l_ref[r0:r0 + dec_seq, :]
            o2 = acc_ref[r0 + dec_seq:r0 + 2 * dec_seq, :] / l_ref[r0 + dec_seq:r0 + 2 * dec_seq, :]
            o_ref[0, :, hh * D_V:(hh + 1) * D_V] = _sub_norm(o1 - lam * o2, sw_ref[...])


def _attn_sample(q_s, kn_rows, vn_rows, cache_k_rows, cache_v_rows, page_table, rel_table, lam_vecs, subln_w):
    bsz, dec_seq, _ = q_s.shape
    n_pages = page_table.shape[1]
    pg = PAGES_PER_STEP
    assert n_pages % pg == 0
    n_groups = n_pages // pg
    q_rows = N_HEADS * 2 * dec_seq
    new_rows = dec_seq * N_HEADS
    tbl_rows = jnp.repeat(rel_table.T, 2 * dec_seq, axis=0)
    tbl_rows = jnp.pad(tbl_rows, ((0, 0), (0, LANES - N_BUCKETS)))
    per_b = lambda shape: pl.BlockSpec(shape, lambda b, g, pt: (b, 0, 0))
    const2 = lambda shape: pl.BlockSpec(shape, lambda b, g, pt: (0, 0))

    def page_spec(p):
        return pl.BlockSpec((PAGE_ROWS, LANES), lambda b, g, pt: (pt[b, g * pg + p], 0))

    grid_spec = pltpu.PrefetchScalarGridSpec(
        num_scalar_prefetch=1,
        grid=(bsz, n_groups),
        in_specs=[per_b((1, dec_seq, D_QKH)), per_b((1, new_rows, LANES)), per_b((1, new_rows, LANES)),
                  const2((q_rows, LANES)), const2((4, D_QK)), const2((1, D_V))]
                 + [page_spec(p) for p in range(pg)] + [page_spec(p) for p in range(pg)],
        out_specs=per_b((1, dec_seq, D_ATT)),
        scratch_shapes=[pltpu.VMEM((q_rows, LANES), F32), pltpu.VMEM((q_rows, LANES), BF16),
                        pltpu.VMEM((LANES, LANES), F32), pltpu.VMEM((LANES, D_V), F32),
                        pltpu.VMEM((q_rows, PAGE_ROWS), F32), pltpu.VMEM((q_rows, PAGE_ROWS), F32),
                        pltpu.VMEM((q_rows, LANES), F32),
                        pltpu.VMEM((q_rows, LANES), F32), pltpu.VMEM((q_rows, LANES), F32),
                        pltpu.VMEM((q_rows, D_V), F32)],
    )
    return pl.pallas_call(
        functools.partial(_attn_sample_kernel, n_groups=n_groups, dec_seq=dec_seq),
        grid_spec=grid_spec,
        out_shape=jax.ShapeDtypeStruct((bsz, dec_seq, D_ATT), F32),
        compiler_params=_cparams(("arbitrary", "arbitrary")),
        name="attn_sample",
    )(page_table, q_s, kn_rows, vn_rows, tbl_rows, lam_vecs, subln_w.reshape(1, D_V),
      *([cache_k_rows] * pg), *([cache_v_rows] * pg))


ROW_CHUNKS = D_MODEL // LANES
assert ROW_CHUNKS == SUBLANES


def _store_row_tiles(ref, mat):
    rows = mat.shape[0]
    for c in range(ROW_CHUNKS):
        ref[pl.ds(c, rows, stride=ROW_CHUNKS), :] = mat[:, c * LANES:(c + 1) * LANES]


def _load_row_tiles(ref, first_row, rows, dtype=F32):
    return jnp.concatenate(
        [ref[pl.ds(first_row * ROW_CHUNKS + c, rows, stride=ROW_CHUNKS), :].astype(dtype) for c in range(ROW_CHUNKS)],
        axis=1)


def _split_bf16(a):
    hi = a.astype(BF16)
    lo = (a - hi.astype(F32)).astype(BF16)
    return hi, lo


def _outproj_kernel(x_ref, yc_ref, o_ref, gt_ref, sc_ref, sh_ref, g_ref, wo_ref, wr_ref, br_ref, cin_ref,
                    x1_ref, h2_ref, idx_ref, gate_ref, rank_ref, cnt_ref):
    tm = x_ref.shape[0]

    @pl.when(pl.program_id(0) == 0)
    def _():
        cnt_ref[...] = cin_ref[...]

    mix = (jnp.dot(yc_ref[...].astype(BF16), wo_ref[0:D_CONV, :], preferred_element_type=F32)
           + jnp.dot(o_ref[...].astype(BF16), wo_ref[D_CONV:, :], preferred_element_type=F32))
    x1 = x_ref[...] + gt_ref[...] * mix
    x1_ref[...] = x1
    ms = jnp.mean(x1 * x1, axis=1, keepdims=True)
    h2 = x1 * lax.rsqrt(ms + EPS) * g_ref[...]
    h2 = h2 * (1.0 + sc_ref[...]) + sh_ref[...]
    _store_row_tiles(h2_ref, h2)
    h_hi, h_lo = _split_bf16(h2)
    w_hi, w_lo = _split_bf16(wr_ref[...])
    logits = (jnp.dot(h_hi, w_hi, preferred_element_type=F32)
              + jnp.dot(h_hi, w_lo, preferred_element_type=F32)
              + jnp.dot(h_lo, w_hi, preferred_element_type=F32)) + br_ref[...]
    lane = lax.broadcasted_iota(I32, (tm, LANES), 1)
    lane_f = lane.astype(F32)
    vals, ids = [], []
    cur = logits
    for _ in range(TOP_K):
        mx = jnp.max(cur, axis=1, keepdims=True)
        ik = jnp.min(jnp.where(cur == mx, lane_f, float(LANES)), axis=1, keepdims=True)
        vals.append(mx)
        ids.append(ik)
        cur = jnp.where(lane_f == ik, -jnp.inf, cur)
    es = [jnp.exp(v - vals[0]) for v in vals]
    denom = functools.reduce(lambda a, c: a + c, es)
    sel = jnp.zeros((tm, LANES), F32)
    idx_out = jnp.zeros((tm, LANES), F32)
    gate_out = jnp.zeros((tm, LANES), F32)
    for k in range(TOP_K):
        sel = sel + jnp.where(lane_f == ids[k], 1.0, 0.0)
        idx_out = jnp.where(lane == k, ids[k], idx_out)
        gate_out = jnp.where(lane == k, es[k] / denom, gate_out)
    r = lax.broadcasted_iota(I32, (tm, tm), 0)
    c = lax.broadcasted_iota(I32, (tm, tm), 1)
    lower = jnp.where(r > c, 1.0, 0.0).astype(BF16)
    before = jnp.dot(lower, sel.astype(BF16), preferred_element_type=F32) + cnt_ref[...]
    rank_out = jnp.zeros((tm, LANES), F32)
    for k in range(TOP_K):
        rk = jnp.sum(jnp.where(lane_f == ids[k], before, 0.0), axis=1, keepdims=True)
        rank_out = jnp.where(lane == k, rk, rank_out)
    cnt_ref[...] = cnt_ref[...] + jnp.sum(sel, axis=0, keepdims=True)
    idx_ref[...] = idx_out.astype(I32)
    gate_ref[...] = gate_out
    rank_ref[...] = rank_out.astype(I32)


def _outproj(x, yc, o, gt, sc, sh, g_ffn, w_out_bf, w_router_pad, b_router_pad, cnt_in, *, tm):
    n = x.shape[0]
    per_row = gt.shape[0] != 1
    mod_spec = pl.BlockSpec((tm, D_MODEL), lambda i: (i, 0)) if per_row else pl.BlockSpec((1, D_MODEL), lambda i: (0, 0))
    const = lambda shape: pl.BlockSpec(shape, lambda i: (0, 0))
    rows = lambda w: pl.BlockSpec((tm, w), lambda i: (i, 0))
    return pl.pallas_call(
        _outproj_kernel,
        grid=(n // tm,),
        in_specs=[rows(D_MODEL), rows(D_CONV), rows(D_ATT), mod_spec, mod_spec, mod_spec, const((1, D_MODEL)),
                  const((D_MODEL, D_MODEL)), const((D_MODEL, LANES)), const((1, LANES)), const((1, LANES))],
        out_specs=[rows(D_MODEL), pl.BlockSpec((tm * ROW_CHUNKS, LANES), lambda i: (i, 0)),
                   rows(LANES), rows(LANES), rows(LANES), const((1, LANES))],
        out_shape=[jax.ShapeDtypeStruct((n, D_MODEL), F32),
                   jax.ShapeDtypeStruct((n * ROW_CHUNKS, LANES), F32),
                   jax.ShapeDtypeStruct((n, LANES), I32),
                   jax.ShapeDtypeStruct((n, LANES), F32),
                   jax.ShapeDtypeStruct((n, LANES), I32),
                   jax.ShapeDtypeStruct((1, LANES), F32)],
        compiler_params=_cparams(("arbitrary",)),
        name="outproj",
    )(x, yc, o, gt, sc, sh, g_ffn.reshape(1, D_MODEL), w_out_bf, w_router_pad, b_router_pad, cnt_in)


def _push_kernel(pend_ref, cnt_ref, dest_hbm, hp_ref, hs_ref, buf_out, idx0, idx1, zero_ref, isem, rsem, zsem,
                 *, n_steps, n_p_steps, n_blocks):
    i = pl.program_id(0)
    tm = hp_ref.shape[0]
    idx_refs = (idx0, idx1)

    @pl.when(i == 0)
    def _():
        zero_ref[...] = jnp.zeros(zero_ref.shape, F32)
        blk_shift = _log2(MOE_BLOCK)
        n_used = pend_ref[N_EXPERTS - 1] >> blk_shift

        def zero_copy(block):
            start = pl.multiple_of(block * MOE_BLOCK, MOE_BLOCK)
            return pltpu.make_async_copy(zero_ref, buf_out.at[pl.ds(start, MOE_BLOCK)], zsem)

        def targets(e):
            return ((cnt_ref[e] > 0, (pend_ref[e] >> blk_shift) - 1), (n_used + e < n_blocks, n_used + e))

        for wait in (False, True):
            for e in range(N_EXPERTS):
                for cond, block in targets(e):
                    @pl.when(cond)
                    def _():
                        if wait:
                            zero_copy(block).wait()
                        else:
                            zero_copy(block).start()

    def idx_copy(step, sl):
        return pltpu.make_async_copy(dest_hbm.at[step], idx_refs[sl], isem.at[sl])

    @pl.when(i == 0)
    def _():
        idx_copy(0, 0).start()

    def push_tile(h_ref, sl):
        idx_copy(i, sl).wait()

        @pl.when(i + 1 < n_steps)
        def _():
            idx_copy(i + 1, 1 - sl).start()

        def row_copy(t, dst_row):
            return pltpu.make_async_copy(h_ref.at[t], buf_out.at[dst_row], rsem)

        def start_body(t, carry):
            for k in range(TOP_K):
                row_copy(t, idx_refs[sl][k * tm + t]).start(priority=k % 2)
            return carry

        lax.fori_loop(0, tm, start_body, 0, unroll=DMA_UNROLL)

        def wait_body(t, carry):
            for k in range(TOP_K):
                row_copy(t, 0).wait()
            return carry

        lax.fori_loop(0, tm, wait_body, 0, unroll=DMA_UNROLL)

    for sl in range(2):
        @pl.when(jnp.logical_and(i < n_p_steps, (i & 1) == sl))
        def _():
            push_tile(hp_ref, sl)

        @pl.when(jnp.logical_and(i >= n_p_steps, (i & 1) == sl))
        def _():
            push_tile(hs_ref, sl)


def _moe_push(h2_p, h2_s, dest_tiles, pend, counts, *, n_blocks):
    tm = TM_ROW
    n_p_steps = h2_p.shape[0] // tm
    n_steps = n_p_steps + h2_s.shape[0] // tm
    assert dest_tiles.shape[0] == n_steps
    any_spec = pl.BlockSpec(memory_space=pl.ANY)
    tile = (tm, ROW_CHUNKS, LANES)
    grid_spec = pltpu.PrefetchScalarGridSpec(
        num_scalar_prefetch=2,
        grid=(n_steps,),
        in_specs=[any_spec,
                  pl.BlockSpec(tile, lambda i, pe, cn: (jnp.minimum(i, n_p_steps - 1), 0, 0)),
                  pl.BlockSpec(tile, lambda i, pe, cn: (jnp.maximum(i - n_p_steps, 0), 0, 0))],
        out_specs=any_spec,
        scratch_shapes=[pltpu.SMEM((TOP_K * tm,), I32), pltpu.SMEM((TOP_K * tm,), I32),
                        pltpu.VMEM((MOE_BLOCK, ROW_CHUNKS, LANES), F32),
                        pltpu.SemaphoreType.DMA((2,)), pltpu.SemaphoreType.DMA(()), pltpu.SemaphoreType.DMA(())],
    )
    return pl.pallas_call(
        functools.partial(_push_kernel, n_steps=n_steps, n_p_steps=n_p_steps, n_blocks=n_blocks),
        grid_spec=grid_spec,
        out_shape=jax.ShapeDtypeStruct((n_blocks * MOE_BLOCK, ROW_CHUNKS, LANES), F32),
        compiler_params=_cparams(("arbitrary",)),
        name="moe_push",
    )(pend, counts, dest_tiles, h2_p, h2_s)


def _expert_kernel(be_ref, nxt_ref, par_ref, nu_ref, x_ref, wg_hbm, bg_ref, wu_hbm, bu_ref, wd_hbm, bd_ref, o_ref,
                   w32, wg_bf, wu_bf, wd_bf, wsem):
    i = pl.program_id(0)
    n_used = nu_ref[0]
    prev = be_ref[jnp.maximum(i - 1, 0)]
    new_expert = jnp.logical_or(i == 0, be_ref[i] != prev)
    w_hbm = (wg_hbm, wu_hbm, wd_hbm)

    def weight_copies(e, sl):
        return [pltpu.make_async_copy(w_hbm[m].at[e], w32.at[sl, m], wsem.at[sl, m]) for m in range(3)]

    @pl.when(i == 0)
    def _():
        for c in weight_copies(be_ref[0], par_ref[0]):
            c.start()

    @pl.when(jnp.logical_and(i < n_used, new_expert))
    def _():
        sl = par_ref[i]
        for c in weight_copies(be_ref[i], sl):
            c.wait()
        wg_bf[...] = w32[sl, 0].astype(BF16)
        wu_bf[...] = w32[sl, 1].astype(BF16)
        wd_bf[...] = w32[sl, 2].astype(BF16)

        @pl.when(nxt_ref[i] >= 0)
        def _():
            for c in weight_copies(nxt_ref[i], 1 - sl):
                c.start()

    @pl.when(i < n_used)
    def _():
        x = _load_row_tiles(x_ref, 0, MOE_BLOCK, BF16)
        g = jnp.dot(x, wg_bf[...], preferred_element_type=F32) + bg_ref[0]
        u = jnp.dot(x, wu_bf[...], preferred_element_type=F32) + bu_ref[0]
        g = jnp.minimum(g, SWIGLU_LIMIT)
        u = jnp.clip(u, -SWIGLU_LIMIT, SWIGLU_LIMIT)
        a = g * jax.nn.sigmoid(SWIGLU_ALPHA * g) * (u + 1.0)
        _store_row_tiles(o_ref, jnp.dot(a.astype(BF16), wd_bf[...], preferred_element_type=F32) + bd_ref[0])

    @pl.when(i >= n_used)
    def _():
        o_ref[...] = jnp.zeros(o_ref.shape, F32)


def _moe_expert(buf, blk_e, n_used, pend, w_gate, b_gate, w_up, b_up, w_down, b_down):
    rows = buf.shape[0] // ROW_CHUNKS
    n_blocks = rows // MOE_BLOCK
    d_ff = w_gate.shape[2]
    assert d_ff == D_MODEL

    def blk(i, be, nx, pa, nu):
        return jnp.minimum(i, nu[0] - 1)

    xspec = pl.BlockSpec((MOE_BLOCK * ROW_CHUNKS, LANES), lambda i, be, nx, pa, nu: (blk(i, be, nx, pa, nu), 0))
    bspec = pl.BlockSpec((1, 1, D_MODEL), lambda i, be, nx, pa, nu: (be[blk(i, be, nx, pa, nu)], 0, 0))
    any_spec = pl.BlockSpec(memory_space=pl.ANY)
    grid_spec = pltpu.PrefetchScalarGridSpec(
        num_scalar_prefetch=4,
        grid=(n_blocks,),
        in_specs=[xspec, any_spec, bspec, any_spec, bspec, any_spec, bspec],
        out_specs=pl.BlockSpec((MOE_BLOCK * ROW_CHUNKS, LANES), lambda i, be, nx, pa, nu: (i, 0)),
        scratch_shapes=[pltpu.VMEM((2, 3, D_MODEL, D_MODEL), F32),
                        pltpu.VMEM((D_MODEL, d_ff), BF16), pltpu.VMEM((D_MODEL, d_ff), BF16),
                        pltpu.VMEM((d_ff, D_MODEL), BF16), pltpu.SemaphoreType.DMA((2, 3))],
    )
    first = jnp.concatenate([jnp.ones((1,), bool), blk_e[1:] != blk_e[:-1]])
    parity = ((jnp.cumsum(first.astype(I32)) - 1) & 1).astype(I32)
    next_first = pend[blk_e] // MOE_BLOCK
    nxt_e = jnp.where(next_first < n_used[0], blk_e[jnp.minimum(next_first, n_blocks - 1)], -1).astype(I32)
    return pl.pallas_call(
        _expert_kernel,
        grid_spec=grid_spec,
        out_shape=jax.ShapeDtypeStruct((rows * ROW_CHUNKS, LANES), F32),
        compiler_params=_cparams(("arbitrary",)),
        name="moe_expert",
    )(blk_e, nxt_e, parity, n_used, buf, w_gate, b_gate.reshape(N_EXPERTS, 1, d_ff), w_up,
      b_up.reshape(N_EXPERTS, 1, d_ff), w_down, b_down.reshape(N_EXPERTS, 1, D_MODEL))


def _combine_kernel(dest_hbm, out_hbm, x1_ref, gate_ref, gt_ref, y_ref,
                    idx0, idx1, rows0, rows1, isem, rsem, *, n_steps):
    i = pl.program_id(0)
    tm = x1_ref.shape[0]
    n_rows = TOP_K * tm
    idx_refs = (idx0, idx1)
    rows_refs = (rows0, rows1)

    def idx_copy(step, sl):
        return pltpu.make_async_copy(dest_hbm.at[step], idx_refs[sl], isem.at[sl])

    def row_copy(src_row, sl, r):
        dst = rows_refs[sl].at[pl.ds(pl.multiple_of(r * ROW_CHUNKS, ROW_CHUNKS), ROW_CHUNKS), :]
        return pltpu.make_async_copy(out_hbm.at[src_row], dst, rsem.at[sl])

    def issue_rows(sl):
        def body(r2, carry):
            for par in range(2):
                r = 2 * r2 + par
                row_copy(idx_refs[sl][r], sl, r).start(priority=par)
            return carry
        lax.fori_loop(0, n_rows // 2, body, 0, unroll=DMA_UNROLL)

    def wait_rows(sl):
        def body(r, carry):
            row_copy(0, sl, r).wait()
            return carry
        lax.fori_loop(0, n_rows, body, 0, unroll=2 * DMA_UNROLL)

    @pl.when(i == 0)
    def _():
        idx_copy(0, 0).start()
        idx_copy(0, 0).wait()
        issue_rows(0)
        if n_steps > 1:
            idx_copy(1, 1).start()

    for sl in range(2):
        @pl.when((i & 1) == sl)
        def _():
            @pl.when(i + 1 < n_steps)
            def _():
                idx_copy(i + 1, 1 - sl).wait()
                issue_rows(1 - sl)

            @pl.when(i + 2 < n_steps)
            def _():
                idx_copy(i + 2, sl).start()

            wait_rows(sl)
            gates = gate_ref[...]
            y = jnp.zeros((tm, D_MODEL), F32)
            for k in range(TOP_K):
                y = y + gates[:, k:k + 1] * _load_row_tiles(rows_refs[sl], k * tm, tm)
            y_ref[...] = x1_ref[...] + gt_ref[...] * y


def _moe_combine(out_rows, dest_tiles, x1, gates, gt):
    n = x1.shape[0]
    tm = TM_ROW
    n_steps = n // tm
    per_row = gt.shape[0] != 1
    mod_spec = pl.BlockSpec((tm, D_MODEL), lambda i: (i, 0)) if per_row else pl.BlockSpec((1, D_MODEL), lambda i: (0, 0))
    return pl.pallas_call(
        functools.partial(_combine_kernel, n_steps=n_steps),
        grid=(n_steps,),
        in_specs=[pl.BlockSpec(memory_space=pl.ANY), pl.BlockSpec(memory_space=pl.ANY),
                  pl.BlockSpec((tm, D_MODEL), lambda i: (i, 0)),
                  pl.BlockSpec((tm, LANES), lambda i: (i, 0)),
                  mod_spec],
        out_specs=pl.BlockSpec((tm, D_MODEL), lambda i: (i, 0)),
        out_shape=jax.ShapeDtypeStruct((n, D_MODEL), F32),
        scratch_shapes=[pltpu.SMEM((TOP_K * tm,), I32), pltpu.SMEM((TOP_K * tm,), I32),
                        pltpu.VMEM((TOP_K * tm * ROW_CHUNKS, LANES), F32),
                        pltpu.VMEM((TOP_K * tm * ROW_CHUNKS, LANES), F32),
                        pltpu.SemaphoreType.DMA((2,)), pltpu.SemaphoreType.DMA((2,))],
        compiler_params=_cparams(("arbitrary",)),
        name="moe_combine",
    )(dest_tiles, out_rows, x1, gates, gt)


def _dest_tiles(dest, tm):
    n = dest.shape[0]
    return dest.reshape(n // tm, tm, TOP_K).transpose(0, 2, 1).reshape(n // tm, TOP_K * tm)


def kernel(x_prompt, x_sample, cache_k, cache_v, state_conv, page_table, c_prompt, c_sample, rel_table, w_ada, b_ada, g_mix, w_in, conv_w, q_norm_w, k_norm_w, lam_q1, lam_k1, lam_q2, lam_k2, subln_w, w_out, g_ffn, w_router, b_router, w_gate, b_gate, w_up, b_up, w_down, b_down):
    assert w_ada.shape[0] == 1, "single-layer trunk"
    bp, t_p, _ = x_prompt.shape
    bs, t_s, _ = x_sample.shape
    assert bp == 1
    n_s = bs * t_s
    n_all = t_p + n_s
    l = 0

    n_c = bp + bs
    c_rows = -(-n_c // SUBLANES) * SUBLANES
    c_all = jnp.concatenate([c_prompt, c_sample, jnp.zeros((c_rows - n_c, D_MODEL), F32)], axis=0)
    mod = _adaln(c_all, w_ada[l], b_ada[l]).reshape(c_rows, 6, D_MODEL)
    mod_p = [mod[0:1, j, :] for j in range(6)]
    per_token = jnp.broadcast_to(mod[1:1 + bs, None, :, :], (bs, t_s, 6, D_MODEL)).reshape(n_s, 6, D_MODEL)
    mod_s = [per_token[:, j, :] for j in range(6)]

    w_in_bf = w_in[l].astype(BF16)
    w_out_bf = w_out[l].astype(BF16)
    lam_vecs = jnp.stack([lam_q1[l], lam_k1[l], lam_q2[l], lam_k2[l]])

    xp = x_prompt.reshape(t_p, D_MODEL)
    xs = x_sample.reshape(n_s, D_MODEL)
    yc_p, q_p, k_p, v_p, kb_p, vt_p, u_tail = _inproj(
        xp, mod_p[1], mod_p[0], g_mix[l], w_in_bf, conv_w[l], q_norm_w[l], k_norm_w[l], None, None,
        tm=TM_IN, seq_len=None)
    st = state_conv[l]
    prev2 = jnp.broadcast_to(st[:, None, 0, :], (bs, t_s, D_CONV)).reshape(n_s, D_CONV)
    prev1 = jnp.broadcast_to(st[:, None, 1, :], (bs, t_s, D_CONV)).reshape(n_s, D_CONV)
    yc_s, q_s, k_s, v_s, _, _, u_s = _inproj(
        xs, mod_s[1], mod_s[0], g_mix[l], w_in_bf, conv_w[l], q_norm_w[l], k_norm_w[l], prev2, prev1,
        tm=n_s, seq_len=t_s)

    o_p = _attn_prompt(q_p, kb_p, vt_p, rel_table, lam_vecs, subln_w[l])
    n_phys = cache_k.shape[1]
    o_s = _attn_sample(q_s.astype(F32).reshape(bs, t_s, D_QKH),
                       k_s.reshape(bs, t_s * N_HEADS, 2 * D_QK), v_s.reshape(bs, t_s * N_HEADS, D_V),
                       cache_k[l].reshape(n_phys * PAGE_ROWS, 2 * D_QK), cache_v[l].reshape(n_phys * PAGE_ROWS, D_V),
                       page_table, rel_table, lam_vecs, subln_w[l]).reshape(n_s, D_ATT)

    w_router_pad = jnp.pad(w_router[l], ((0, 0), (0, LANES - N_EXPERTS)))
    b_router_pad = jnp.concatenate([b_router[l], jnp.full((LANES - N_EXPERTS,), NEG_INF, F32)]).reshape(1, LANES)
    cnt0 = jnp.zeros((1, LANES), F32)
    x1_p, h2_p, idx_p, gate_p, rank_p, cnt1 = _outproj(
        xp, yc_p, o_p, mod_p[2], mod_p[4], mod_p[3], g_ffn[l], w_out_bf, w_router_pad, b_router_pad, cnt0, tm=TM_OUT)
    x1_s, h2_s, idx_s, gate_s, rank_s, cnt2 = _outproj(
        xs, yc_s, o_s, mod_s[2], mod_s[4], mod_s[3], g_ffn[l], w_out_bf, w_router_pad, b_router_pad, cnt1,
        tm=min(TM_OUT, n_s))

    counts = cnt2[0, :N_EXPERTS].astype(I32)
    padded = (counts + MOE_BLOCK - 1) // MOE_BLOCK * MOE_BLOCK
    pend = jnp.cumsum(padded).astype(I32)
    pstart = pend - padded
    n_blocks = (n_all * TOP_K) // MOE_BLOCK + N_EXPERTS
    def slots(idx, rank):
        is_e = idx[:, :TOP_K, None] == jnp.arange(N_EXPERTS, dtype=I32)
        return jnp.sum(jnp.where(is_e, pstart, 0), axis=-1) + rank[:, :TOP_K]

    dest_p = slots(idx_p, rank_p)
    dest_s = slots(idx_s, rank_s)
    block_start = jnp.arange(n_blocks, dtype=I32) * MOE_BLOCK
    blk_e = jnp.minimum(jnp.sum(pend[None, :] <= block_start[:, None], axis=1), N_EXPERTS - 1).astype(I32)
    n_used = (pend[-1] // MOE_BLOCK).astype(I32).reshape(1)
    tiles_p = _dest_tiles(dest_p, TM_ROW)
    tiles_s = _dest_tiles(dest_s, TM_ROW)

    as_tiles = lambda a: a.reshape(a.shape[0] // ROW_CHUNKS, ROW_CHUNKS, LANES)
    buf = _moe_push(as_tiles(h2_p), as_tiles(h2_s), jnp.concatenate([tiles_p, tiles_s], axis=0), pend, counts,
                    n_blocks=n_blocks)
    out_rows = as_tiles(_moe_expert(buf.reshape(n_blocks * MOE_BLOCK * ROW_CHUNKS, LANES), blk_e, n_used, pend,
                                    w_gate[l], b_gate[l], w_up[l], b_up[l], w_down[l], b_down[l]))
    y_p = _moe_combine(out_rows, tiles_p, x1_p, gate_p, mod_p[5])
    y_s = _moe_combine(out_rows, tiles_s, x1_s, gate_s, mod_s[5])

    k_prompt = k_p.reshape(1, bp, t_p, N_HEADS, 2 * D_QK)
    v_prompt = v_p.reshape(1, bp, t_p, N_HEADS, D_V)
    conv_prompt = u_tail[SUBLANES - (CONV_W - 1):, :].reshape(1, bp, CONV_W - 1, D_CONV)
    k_sample = k_s.reshape(1, bs, t_s, N_HEADS, 2 * D_QK)
    v_sample = v_s.reshape(1, bs, t_s, N_HEADS, D_V)
    conv_sample = u_s.reshape(bs, t_s, D_CONV)[:, t_s - (CONV_W - 1):, :].reshape(1, bs, CONV_W - 1, D_CONV)
    return (y_p.reshape(bp, t_p, D_MODEL), y_s.reshape(bs, t_s, D_MODEL),
            k_prompt, v_prompt, conv_prompt, k_sample, v_sample, conv_sample)
```
